```python
import math
import jax
import jax.numpy as jnp
from jax import lax
import numpy as np

D_MODEL = 1024
BATCH = 32
SEQ = 256
DEPTH = 4
DEC_BATCH = 4
DEC_SEQ = 2048
PAST_LEN = 256

GRID_W = 64
EPS = 1e-6

SSD_HEADS = 16
SSD_HEAD_DIM = 64
SSD_INNER = SSD_HEADS * SSD_HEAD_DIM
SSD_GROUPS = 2
SSD_STATE = 128
SSD_GN = SSD_GROUPS * SSD_STATE
SSD_CONV_W = 5
SSD_CONV_DIM = SSD_INNER + 2 * SSD_GN
SSD_CHUNK = 64

NA_HEADS = 16
NA_HEAD_DIM = 64
NA_INNER = NA_HEADS * NA_HEAD_DIM
NA_WIN_ROWS = 8
NA_WIN_COLS = 16
NA_QBLK_COLS = 16
NA_KBLK_COLS = 32
NA_SCALE = NA_HEAD_DIM ** -0.5
CTX_QBLK = 128

GLA_HEADS = 4
GLA_DK = D_MODEL // 2
GLA_DV = D_MODEL
GLA_DK_HEAD = GLA_DK // GLA_HEADS
GLA_DV_HEAD = GLA_DV // GLA_HEADS
GLA_LOWRANK = 16
GLA_TAU = 16.0
GLA_CHUNK = 32
GLA_QK_SCALE = GLA_DK_HEAD ** -0.5

N_BRANCH = 3
BRANCH_W = 1024

N_EXPERTS = 32
TOP_K = 4
D_FF = D_MODEL
SWIGLU_ALPHA = 1.702
SWIGLU_LIMIT = 7.0
MOE_BLK = 128

OFF_SSD_Z = 0
OFF_SSD_XBC = OFF_SSD_Z + SSD_INNER
OFF_SSD_DT = OFF_SSD_XBC + SSD_CONV_DIM
OFF_NA_QKV = OFF_SSD_DT + 2 * SSD_HEADS
OFF_GLA_Q = OFF_NA_QKV + 3 * NA_INNER
OFF_GLA_K = OFF_GLA_Q + GLA_DK
OFF_GLA_V = OFF_GLA_K + GLA_DK
OFF_GLA_R = OFF_GLA_V + GLA_DV
OFF_GLA_LR = OFF_GLA_R + GLA_DV
OFF_GATES = OFF_GLA_LR + 2 * GLA_LOWRANK
IN_COLS = OFF_GATES + N_BRANCH * D_MODEL

kernel_name = "hybrid_diffusion_ssd_natten_gla_moe_step"


def rmsnorm(x, g):
    xf = x.astype(jnp.float32)
    xf = xf * lax.rsqrt(jnp.mean(xf * xf, axis=-1, keepdims=True) + EPS)
    return (xf * g.astype(jnp.float32)).astype(x.dtype)


def flip(t):
    return jnp.flip(t, axis=1)


def dwconv_centred(x, w, b):
    pad = SSD_CONV_W // 2
    y = lax.conv_general_dilated(x, w[:, None, :].astype(x.dtype), window_strides=(1,), padding=[(pad, pad)],
                                 dimension_numbers=("NWC", "WIO", "NWC"), feature_group_count=x.shape[-1])
    return y + b.astype(x.dtype)


def ssd_scan(x, dt, a, bm, cm, s0):
    f32 = jnp.float32
    B, L, H, P = x.shape
    G, N = bm.shape[2], bm.shape[3]
    M = H // G
    Q = SSD_CHUNK
    nc = L // Q
    xq = (x.astype(f32) * dt[..., None]).reshape(B, nc, Q, G, M, P)
    bq = bm.astype(f32).reshape(B, nc, Q, G, N)
    cq = cm.astype(f32).reshape(B, nc, Q, G, N)
    cum = jnp.cumsum((dt * a).reshape(B, nc, Q, G, M), axis=2)
    tri = jnp.asarray(np.tril(np.ones((Q, Q), bool)))[:, :, None, None]
    seg = cum[:, :, :, None] - cum[:, :, None, :]
    lmat = jnp.exp(jnp.where(tri, seg, -jnp.inf))
    cb = jnp.einsum("bcign,bcjgn->bcijg", cq, bq)
    y = jnp.einsum("bcijgm,bcjgmp->bcigmp", cb[..., None] * lmat, xq)
    decay_end = jnp.exp(cum[:, :, -1:] - cum)
    chunk_states = jnp.einsum("bcjgn,bcjgm,bcjgmp->cbgmpn", bq, decay_end, xq)
    chunk_decay = jnp.exp(cum[:, :, -1]).swapaxes(0, 1)

    def step(s, inp):
        cs, cd = inp
        return s * cd[..., None, None] + cs, s

    s_fin, s_in = lax.scan(step, s0.astype(f32).reshape(B, G, M, P, N), (chunk_states, chunk_decay))
    y = y + jnp.einsum("bcign,cbgmpn,bcigm->bcigmp", cq, s_in, jnp.exp(cum))
    return y.reshape(B, L, H, P), s_fin.reshape(B, H, P, N)


def ssd_branch(proj, conv_w, conv_b, dt_bias, a_log, d_skip, norm_g, s0):
    f32 = jnp.float32
    B, L, _ = proj.shape
    z = proj[..., OFF_SSD_Z:OFF_SSD_XBC]
    xbc = jax.nn.silu(dwconv_centred(proj[..., OFF_SSD_XBC:OFF_SSD_DT], conv_w, conv_b))
    x = xbc[..., :SSD_INNER].reshape(B, L, SSD_HEADS, SSD_HEAD_DIM)
    bm = xbc[..., SSD_INNER:SSD_INNER + SSD_GN].reshape(B, L, SSD_GROUPS, SSD_STATE)
    cm = xbc[..., SSD_INNER + SSD_GN:].reshape(B, L, SSD_GROUPS, SSD_STATE)
    dt = jax.nn.softplus(proj[..., OFF_SSD_DT:OFF_NA_QKV].reshape(B, L, 2, SSD_HEADS).astype(f32)
                         + dt_bias.astype(f32))
    a = -jnp.exp(a_log.astype(f32))
    y_f, s_f = ssd_scan(x, dt[:, :, 0], a[0], bm, cm, s0[:, 0])
    y_b, s_b = ssd_scan(flip(x), flip(dt[:, :, 1]), a[1], flip(bm), flip(cm), s0[:, 1])
    y = y_f + flip(y_b) + d_skip.astype(f32)[:, None] * x.astype(f32)
    y = y.reshape(B, L, SSD_INNER).astype(proj.dtype) * jax.nn.silu(z)
    return rmsnorm(y, norm_g), jnp.stack([s_f, s_b], axis=1)


def gla_scan(q, k, v, g, s0):
    f32 = jnp.float32
    B, L, H, DK = q.shape
    DV = v.shape[-1]
    Q = GLA_CHUNK
    nc = L // Q
    q = q.astype(f32).reshape(B, nc, Q, H, DK)
    k = k.astype(f32).reshape(B, nc, Q, H, DK)
    v = v.astype(f32).reshape(B, nc, Q, H, DV)
    b = jnp.cumsum(g.astype(f32).reshape(B, nc, Q, H, DK), axis=2)
    b_end = b[:, :, -1:]
    qt = q * jnp.exp(b)
    kt = k * jnp.exp(-b)
    kend = k * jnp.exp(b_end - b)
    tri = jnp.asarray(np.tril(np.ones((Q, Q), bool)))
    att = jnp.where(tri, jnp.einsum("bcihd,bcjhd->bchij", qt, kt), 0.0)
    o = jnp.einsum("bchij,bcjhv->bcihv", att, v)
    chunk_states = jnp.einsum("bcjhd,bcjhv->cbhdv", kend, v)
    chunk_decay = jnp.exp(b_end[:, :, 0]).swapaxes(0, 1)

    def step(s, inp):
        cs, cd = inp
        return s * cd[..., None] + cs, s

    s_fin, s_in = lax.scan(step, s0.astype(f32), (chunk_states, chunk_decay))
    o = o + jnp.einsum("bcihd,cbhdv->bcihv", qt, s_in)
    return o.reshape(B, L, H, DV), s_fin


def gla_branch(proj, w_gate, b_gate, norm_g, s0):
    B, L, _ = proj.shape
    q = proj[..., OFF_GLA_Q:OFF_GLA_K].reshape(B, L, GLA_HEADS, GLA_DK_HEAD) * GLA_QK_SCALE
    k = proj[..., OFF_GLA_K:OFF_GLA_V].reshape(B, L, GLA_HEADS, GLA_DK_HEAD)
    v = proj[..., OFF_GLA_V:OFF_GLA_R].reshape(B, L, GLA_HEADS, GLA_DV_HEAD)
    r = proj[..., OFF_GLA_R:OFF_GLA_LR]
    lr = proj[..., OFF_GLA_LR:OFF_GATES].reshape(B, L, 2, GLA_LOWRANK)
    gpre = jnp.einsum("bldr,drk->bldk", lr, w_gate) + b_gate
    g = (jax.nn.log_sigmoid(gpre.astype(jnp.float32)) / GLA_TAU).reshape(B, L, 2, GLA_HEADS, GLA_DK_HEAD)
    o_f, s_f = gla_scan(q, k, v, g[:, :, 0], s0[:, 0])
    o_b, s_b = gla_scan(flip(q), flip(k), flip(v), flip(g[:, :, 1]), s0[:, 1])
    o = rmsnorm((o_f + flip(o_b)).astype(proj.dtype), norm_g).reshape(B, L, GLA_DV) * jax.nn.silu(r)
    return o, jnp.stack([s_f, s_b], axis=1)


def context_attention(q, k, v):
    B, L, H, hd = q.shape
    qb = q.reshape(B, L // CTX_QBLK, CTX_QBLK, H, hd).swapaxes(0, 1)

    def blk(qi):
        s = jnp.einsum("bqhd,bkhd->bhqk", qi, k).astype(jnp.float32) * NA_SCALE
        p = jax.nn.softmax(s, axis=-1).astype(v.dtype)
        return jnp.einsum("bhqk,bkhd->bqhd", p, v)

    return lax.map(blk, qb).swapaxes(0, 1).reshape(B, L, H, hd)


def neighbourhood_attention(q, k, v, ck, cv, rpb):
    B, S, H, hd = q.shape
    rows = S // GRID_W
    wr = min(NA_WIN_ROWS, rows)
    ncb = GRID_W // NA_QBLK_COLS
    qcol = np.arange(GRID_W).reshape(ncb, NA_QBLK_COLS)
    cstart = np.clip(qcol - NA_WIN_COLS // 2, 0, GRID_W - NA_WIN_COLS)
    kc0 = np.minimum(cstart[:, 0], GRID_W - NA_KBLK_COLS)
    kcol = kc0[:, None] + np.arange(NA_KBLK_COLS)
    col_valid = (kcol[:, None, :] >= cstart[:, :, None]) & (kcol[:, None, :] < cstart[:, :, None] + NA_WIN_COLS)
    mask = np.broadcast_to(col_valid[:, :, None, :], (ncb, NA_QBLK_COLS, wr, NA_KBLK_COLS)).reshape(
        ncb, NA_QBLK_COLS, wr * NA_KBLK_COLS)
    dx_idx = np.clip(kcol[:, None, :] - qcol[:, :, None] + NA_WIN_COLS - 1, 0, 2 * NA_WIN_COLS - 2)
    qg = q.reshape(B, rows, ncb, NA_QBLK_COLS, H, hd).transpose(1, 0, 2, 3, 4, 5)
    kg = k.reshape(B, rows, GRID_W, H, hd)
    vg = v.reshape(B, rows, GRID_W, H, hd)
    nloc = wr * NA_KBLK_COLS

    def row_step(args):
        r, qr = args
        rs = jnp.clip(r - wr // 2, 0, rows - wr)
        kw = lax.dynamic_slice_in_dim(kg, rs, wr, axis=1)[:, :, kcol]
        vw = lax.dynamic_slice_in_dim(vg, rs, wr, axis=1)[:, :, kcol]
        kw = kw.transpose(0, 2, 1, 3, 4, 5).reshape(B, ncb, nloc, H, hd)
        vw = vw.transpose(0, 2, 1, 3, 4, 5).reshape(B, ncb, nloc, H, hd)
        bias = lax.dynamic_slice_in_dim(rpb, rs - r + NA_WIN_ROWS - 1, wr, axis=1)[:, :, dx_idx]
        bias = bias.transpose(0, 2, 3, 1, 4).reshape(H, ncb, NA_QBLK_COLS, nloc).astype(jnp.float32)
        s_loc = jnp.einsum("bjqhd,bjkhd->bhjqk", qr, kw).astype(jnp.float32) * NA_SCALE + bias
        s_loc = jnp.where(mask, s_loc, -jnp.inf)
        s_ctx = jnp.einsum("bjqhd,bkhd->bhjqk", qr, ck).astype(jnp.float32) * NA_SCALE
        p = jax.nn.softmax(jnp.concatenate([s_loc, s_ctx], axis=-1), axis=-1).astype(v.dtype)
        return (jnp.einsum("bhjqk,bjkhd->bjqhd", p[..., :nloc], vw)
                + jnp.einsum("bhjqk,bkhd->bjqhd", p[..., nloc:], cv))

    o = lax.map(row_step, (jnp.arange(rows, dtype=jnp.int32), qg))
    return o.transpose(1, 0, 2, 3, 4, 5).reshape(B, S, H, hd)


def moe_ffn(x2, router_w, router_b, w_gu, b_gu, w_dn, b_dn):
    T, D = x2.shape
    logits = (x2 @ router_w + router_b).astype(jnp.float32)
    top_logit, top_idx = lax.top_k(logits, TOP_K)
    top_w = jax.nn.softmax(top_logit, axis=-1)
    TK = T * TOP_K
    flat_e = top_idx.reshape(TK)
    flat_t = jnp.arange(TK, dtype=jnp.int32) // TOP_K
    flat_w = top_w.reshape(TK)
    order = jnp.argsort(flat_e)
    e_sorted = flat_e[order]
    counts = jnp.bincount(flat_e, length=N_EXPERTS)
    padded = (counts + MOE_BLK - 1) // MOE_BLK * MOE_BLK
    pad_end = jnp.cumsum(padded)
    pad_start = pad_end - padded
    seg_start = jnp.cumsum(counts) - counts
    dest = pad_start[e_sorted] + jnp.arange(TK, dtype=jnp.int32) - seg_start[e_sorted]
    n_blocks = -(-(TK + N_EXPERTS * (MOE_BLK - 1)) // MOE_BLK)
    n_slots = n_blocks * MOE_BLK
    slot_tok = jnp.full((n_slots,), T, jnp.int32).at[dest].set(flat_t[order])
    slot_w = jnp.zeros((n_slots,), jnp.float32).at[dest].set(flat_w[order])
    block_e = jnp.minimum(jnp.searchsorted(pad_end, jnp.arange(n_blocks) * MOE_BLK, side="right"),
                          N_EXPERTS - 1)
    x_pad = jnp.concatenate([x2, jnp.zeros((1, D), x2.dtype)], axis=0)
    xb = x_pad[slot_tok].reshape(n_blocks, MOE_BLK, D)

    def expert_block(args):
        xi, e = args
        hgu = xi @ w_gu[e] + b_gu[e]
        glu = jnp.minimum(hgu[:, :D_FF], SWIGLU_LIMIT)
        lin = jnp.clip(hgu[:, D_FF:], -SWIGLU_LIMIT, SWIGLU_LIMIT)
        act = glu * jax.nn.sigmoid(SWIGLU_ALPHA * glu) * (lin + 1.0)
        return act @ w_dn[e] + b_dn[e]

    yb = lax.map(expert_block, (xb, block_e)).reshape(n_slots, D)
    y = jnp.zeros((T + 1, D), yb.dtype).at[slot_tok].add(yb * slot_w[:, None].astype(yb.dtype))
    return y[:T]


def trunk_layer(x, cvec, lp, ctx):
    B, L, _ = x.shape
    mod = jax.nn.silu(cvec) @ lp["w_ada"] + lp["b_ada"]
    shift1, scale1, gate1, shift2, scale2, gate2 = [t[..., None, :] for t in jnp.split(mod, 6, axis=-1)]
    h = rmsnorm(x, lp["norm1_g"]) * (1.0 + scale1) + shift1
    proj = h @ lp["w_in"]
    if ctx is None:
        s_ssd0 = jnp.zeros((B, 2, SSD_HEADS, SSD_HEAD_DIM, SSD_STATE), jnp.float32)
        s_gla0 = jnp.zeros((B, 2, GLA_HEADS, GLA_DK_HEAD, GLA_DV_HEAD), jnp.float32)
    else:
        ck, cv, s_ssd0, s_gla0 = ctx
    o_ssd, s_ssd = ssd_branch(proj, lp["ssd_conv_w"], lp["ssd_conv_b"], lp["ssd_dt_bias"], lp["ssd_a_log"],
                              lp["ssd_d"], lp["ssd_norm_g"], s_ssd0)
    q = proj[..., OFF_NA_QKV:OFF_NA_QKV + NA_INNER].reshape(B, L, NA_HEADS, NA_HEAD_DIM)
    k = proj[..., OFF_NA_QKV + NA_INNER:OFF_NA_QKV + 2 * NA_INNER].reshape(B, L, NA_HEADS, NA_HEAD_DIM)
    v = proj[..., OFF_NA_QKV + 2 * NA_INNER:OFF_GLA_Q].reshape(B, L, NA_HEADS, NA_HEAD_DIM)
    if ctx is None:
        o_na = context_attention(q, k, v)
    else:
        o_na = neighbourhood_attention(q, k, v, ck, cv, lp["na_rpb"])
    o_gla, s_gla = gla_branch(proj, lp["gla_w_gate"], lp["gla_b_gate"], lp["gla_norm_g"], s_gla0)
    o_all = jnp.stack([o_ssd, o_na.reshape(B, L, NA_INNER), o_gla], axis=2)
    gates = jax.nn.sigmoid(proj[..., OFF_GATES:].reshape(B, L, N_BRANCH, D_MODEL))
    merged = jnp.sum(gates * jnp.einsum("blnw,nwd->blnd", o_all, lp["w_branch"]), axis=2)
    x = x + gate1 * (merged @ lp["w_out"])
    h2 = rmsnorm(x, lp["norm2_g"]) * (1.0 + scale2) + shift2
    y = moe_ffn(h2.reshape(B * L, D_MODEL), lp["router_w"], lp["router_b"], lp["moe_w_gu"], lp["moe_b_gu"],
                lp["moe_w_dn"], lp["moe_b_dn"]).reshape(B, L, D_MODEL)
    x = x + gate2 * y
    return x, (k, v, s_ssd, s_gla)


def setup_inputs(seed: int = 0) -> dict:
    key = jax.random.key(seed)
    ks = jax.random.split(key, 32)
    f32 = jnp.float32

    def nrm(k, shape, s):
        return jax.random.normal(k, shape, f32) * s

    dt0 = jnp.exp(jax.random.uniform(ks[15], (DEPTH, 2, SSD_HEADS), f32, math.log(1e-3), math.log(1e-1)))
    return {
        "x_prompt": nrm(ks[0], (BATCH, SEQ, D_MODEL), 1.0),
        "x_sample": nrm(ks[1], (DEC_BATCH, DEC_SEQ, D_MODEL), 1.0),
        "cache_na_k": nrm(ks[2], (DEC_BATCH, DEPTH, PAST_LEN, NA_HEADS, NA_HEAD_DIM), 1.0),
        "cache_na_v": nrm(ks[3], (DEC_BATCH, DEPTH, PAST_LEN, NA_HEADS, NA_HEAD_DIM), 1.0),
        "state_ssd": nrm(ks[4], (DEC_BATCH, DEPTH, 2, SSD_HEADS, SSD_HEAD_DIM, SSD_STATE), 0.1),
        "state_gla": nrm(ks[5], (DEC_BATCH, DEPTH, 2, GLA_HEADS, GLA_DK_HEAD, GLA_DV_HEAD), 0.5),
        "c": nrm(ks[6], (DEC_BATCH, D_MODEL), 1.0),
        "c_ctx": nrm(ks[7], (D_MODEL,), 1.0),
        "w_ada": nrm(ks[8], (DEPTH, D_MODEL, 6 * D_MODEL), 0.5 * D_MODEL ** -0.5),
        "b_ada": nrm(ks[9], (DEPTH, 6 * D_MODEL), 0.02),
        "norm1_g": 1.0 + nrm(ks[10], (DEPTH, D_MODEL), 0.02),
        "norm2_g": 1.0 + nrm(ks[11], (DEPTH, D_MODEL), 0.02),
        "w_in": nrm(ks[12], (DEPTH, D_MODEL, IN_COLS), D_MODEL ** -0.5),
        "ssd_conv_w": nrm(ks[13], (DEPTH, SSD_CONV_W, SSD_CONV_DIM), SSD_CONV_W ** -0.5),
        "ssd_conv_b": nrm(ks[14], (DEPTH, SSD_CONV_DIM), 0.02),
        "ssd_dt_bias": dt0 + jnp.log(-jnp.expm1(-dt0)),
        "ssd_a_log": jnp.log(jax.random.uniform(ks[16], (DEPTH, 2, SSD_HEADS), f32, 1.0, 16.0)),
        "ssd_d": 1.0 + nrm(ks[17], (DEPTH, SSD_HEADS), 0.02),
        "ssd_norm_g": 1.0 + nrm(ks[18], (DEPTH, SSD_INNER), 0.02),
        "na_rpb": nrm(ks[19], (DEPTH, NA_HEADS, 2 * NA_WIN_ROWS - 1, 2 * NA_WIN_COLS - 1), 0.2),
        "gla_w_gate": nrm(ks[20], (DEPTH, 2, GLA_LOWRANK, GLA_DK), GLA_LOWRANK ** -0.5),
        "gla_b_gate": nrm(ks[21], (DEPTH, 2, GLA_DK), 0.1),
        "gla_norm_g": 1.0 + nrm(ks[22], (DEPTH, GLA_DV_HEAD), 0.02),
        "w_branch": nrm(ks[23], (DEPTH, N_BRANCH, BRANCH_W, D_MODEL), BRANCH_W ** -0.5),
        "w_out": nrm(ks[24], (DEPTH, D_MODEL, D_MODEL), D_MODEL ** -0.5),
        "router_w": nrm(ks[25], (DEPTH, D_MODEL, N_EXPERTS), D_MODEL ** -0.5),
        "router_b": nrm(ks[26], (DEPTH, N_EXPERTS), 0.01),
        "moe_w_gu": nrm(ks[27], (DEPTH, N_EXPERTS, D_MODEL, 2 * D_FF), D_MODEL ** -0.5),
        "moe_b_gu": nrm(ks[28], (DEPTH, N_EXPERTS, 2 * D_FF), 0.02),
        "moe_w_dn": nrm(ks[29], (DEPTH, N_EXPERTS, D_FF, D_MODEL), D_FF ** -0.5),
        "moe_b_dn": nrm(ks[30], (DEPTH, N_EXPERTS, D_MODEL), 0.02),
        "final_norm_g": 1.0 + nrm(ks[31], (D_MODEL,), 0.02),
    }


def reference(x_prompt, x_sample, cache_na_k, cache_na_v, state_ssd, state_gla, c, c_ctx,
              w_ada, b_ada, norm1_g, norm2_g, w_in, ssd_conv_w, ssd_conv_b, ssd_dt_bias, ssd_a_log, ssd_d,
              ssd_norm_g, na_rpb, gla_w_gate, gla_b_gate, gla_norm_g, w_branch, w_out,
              router_w, router_b, moe_w_gu, moe_b_gu, moe_w_dn, moe_b_dn, final_norm_g):
    xp = x_prompt
    xs = x_sample
    ks_, vs_, sss_, sgs_ = [], [], [], []
    for l in range(DEPTH):
        lp = {
            "w_ada": w_ada[l], "b_ada": b_ada[l], "norm1_g": norm1_g[l], "norm2_g": norm2_g[l],
            "w_in": w_in[l], "ssd_conv_w": ssd_conv_w[l], "ssd_conv_b": ssd_conv_b[l],
            "ssd_dt_bias": ssd_dt_bias[l], "ssd_a_log": ssd_a_log[l], "ssd_d": ssd_d[l],
            "ssd_norm_g": ssd_norm_g[l], "na_rpb": na_rpb[l], "gla_w_gate": gla_w_gate[l],
            "gla_b_gate": gla_b_gate[l], "gla_norm_g": gla_norm_g[l], "w_branch": w_branch[l],
            "w_out": w_out[l], "router_w": router_w[l], "router_b": router_b[l],
            "moe_w_gu": moe_w_gu[l], "moe_b_gu": moe_b_gu[l], "moe_w_dn": moe_w_dn[l], "moe_b_dn": moe_b_dn[l],
        }
        xp, (k_l, v_l, s_ssd_l, s_gla_l) = trunk_layer(xp, c_ctx, lp, None)
        ks_.append(k_l)
        vs_.append(v_l)
        sss_.append(s_ssd_l)
        sgs_.append(s_gla_l)
        xs, _ = trunk_layer(xs, c, lp, (cache_na_k[:, l], cache_na_v[:, l], state_ssd[:, l], state_gla[:, l]))
    y_prompt = rmsnorm(xp, final_norm_g)
    y_sample = rmsnorm(xs, final_norm_g)
    new_cache_na_k = jnp.stack(ks_, axis=1)
    new_cache_na_v = jnp.stack(vs_, axis=1)
    new_state_ssd = jnp.stack(sss_, axis=1)
    new_state_gla = jnp.stack(sgs_, axis=1)
    return (y_prompt, y_sample, new_cache_na_k, new_cache_na_v, new_state_ssd, new_state_gla)
```

```python
import functools

import numpy as np
import jax
import jax.numpy as jnp
from jax import lax
from jax.experimental import pallas as pl
from jax.experimental.pallas import tpu as pltpu

F32 = jnp.float32
BF16 = jnp.bfloat16

D_MODEL = 1024
DEPTH = 4
GRID_W = 64
EPS = 1e-6

SSD_HEADS = 16
SSD_HEAD_DIM = 64
SSD_INNER = SSD_HEADS * SSD_HEAD_DIM
SSD_GROUPS = 2
SSD_STATE = 128
SSD_GN = SSD_GROUPS * SSD_STATE
SSD_CONV_W = 5
SSD_CONV_DIM = SSD_INNER + 2 * SSD_GN
SSD_Q = 64
SSD_GW = SSD_INNER // SSD_GROUPS

NA_HEADS = 16
NA_HEAD_DIM = 64
NA_INNER = NA_HEADS * NA_HEAD_DIM
NA_WIN_ROWS = 8
NA_WIN_COLS = 16
NA_SCALE = NA_HEAD_DIM ** -0.5
NA_NEG = -1e30

GLA_HEADS = 4
GLA_DK = D_MODEL // 2
GLA_DV = D_MODEL
GLA_DK_HEAD = GLA_DK // GLA_HEADS
GLA_DV_HEAD = GLA_DV // GLA_HEADS
GLA_LOWRANK = 16
GLA_TAU = 16.0
GLA_Q = 32
GLA_QK_SCALE = GLA_DK_HEAD ** -0.5

N_BRANCH = 3
N_EXPERTS = 32
TOP_K = 4
D_FF = D_MODEL
SWIGLU_ALPHA = 1.702
SWIGLU_LIMIT = 7.0
MOE_ROWS = 512
MOE_FF_TILE = 512

OFF_SSD_XBC = SSD_INNER
OFF_SSD_DT = OFF_SSD_XBC + SSD_CONV_DIM
OFF_NA_QKV = OFF_SSD_DT + 2 * SSD_HEADS
OFF_GLA_Q = OFF_NA_QKV + 3 * NA_INNER
OFF_GLA_LR = OFF_GLA_Q + 2 * GLA_DK + 2 * GLA_DV
OFF_GATES = OFF_GLA_LR + 2 * GLA_LOWRANK
IN_COLS = OFF_GATES + N_BRANCH * D_MODEL

M_Z = 0
M_X = M_Z + SSD_INNER
M_Q = M_X + SSD_INNER
M_K = M_Q + NA_INNER
M_V = M_K + NA_INNER
M_GV = M_V + NA_INNER
M_R = M_GV + GLA_DV
M_GATES = M_R + GLA_DV
M_GQ = M_GATES + N_BRANCH * D_MODEL
M_GK = M_GQ + GLA_DK
M_B = M_GK + GLA_DK
M_C = M_B + SSD_GN
M_COLS = M_C + SSD_GN
_MAIN_SEGMENTS = (
    (0, 2 * SSD_INNER),
    (OFF_NA_QKV, 3 * NA_INNER),
    (OFF_GLA_Q + 2 * GLA_DK, 2 * GLA_DV),
    (OFF_GATES, N_BRANCH * D_MODEL),
    (OFF_GLA_Q, 2 * GLA_DK),
    (OFF_SSD_XBC + SSD_INNER, 2 * SSD_GN),
)
_SMALL_SEGMENTS = ((OFF_SSD_DT, 2 * SSD_HEADS), (OFF_GLA_LR, 2 * GLA_LOWRANK))
S_DT = 0
S_LR = 2 * SSD_HEADS
S_COLS = 128
PROJ_TN = 512
PROJ_COLS = M_COLS + PROJ_TN
PROJ_TM = 2048

VMEM_LIMIT = 56 * 1024 * 1024


def _cparams(sem):
    return pltpu.CompilerParams(dimension_semantics=sem, vmem_limit_bytes=VMEM_LIMIT)


def _silu(x):
    return x * jax.nn.sigmoid(x)


def _softplus(x):
    return jnp.maximum(x, 0.0) + jnp.log1p(jnp.exp(-jnp.abs(x)))


def _split3(x):
    hi = x.astype(BF16)
    r1 = x - hi.astype(F32)
    mid = r1.astype(BF16)
    lo = (r1 - mid.astype(F32)).astype(BF16)
    return hi, mid, lo


def _dot(a, b):
    return jnp.dot(a, b, preferred_element_type=F32)


def _dot_nt(a, b):
    return lax.dot_general(a, b, (((1,), (1,)), ((), ())), preferred_element_type=F32)


def _dot_tn(a, b):
    return lax.dot_general(a, b, (((0,), (0,)), ((), ())), preferred_element_type=F32)


def _sel_dot_l(sel_bf, x):
    hi, mid, lo = _split3(x)
    return (_dot(sel_bf, lo) + _dot(sel_bf, mid)) + _dot(sel_bf, hi)


def _sel_dot_r(x, sel_bf):
    hi, mid, lo = _split3(x)
    return (_dot(lo, sel_bf) + _dot(mid, sel_bf)) + _dot(hi, sel_bf)


def _mod_kernel(c_ref, w_ref, b_ref, o_ref):
    c = c_ref[...]
    o_ref[...] = jnp.dot(_silu(c), w_ref[...], preferred_element_type=F32,
                         precision=lax.Precision.HIGHEST) + b_ref[...]


def _modulation(cvecs, w_ada, b_ada):
    nrow = cvecs.shape[0]
    return pl.pallas_call(
        _mod_kernel,
        grid=(DEPTH, 6),
        in_specs=[
            pl.BlockSpec((nrow, D_MODEL), lambda l, j: (0, 0)),
            pl.BlockSpec((None, D_MODEL, D_MODEL), lambda l, j: (l, 0, j)),
            pl.BlockSpec((None, 1, D_MODEL), lambda l, j: (l, 0, j)),
        ],
        out_specs=pl.BlockSpec((None, nrow, D_MODEL), lambda l, j: (l, 0, j)),
        out_shape=jax.ShapeDtypeStruct((DEPTH, nrow, 6 * D_MODEL), F32),
        compiler_params=_cparams(("arbitrary", "arbitrary")),
        name="adaln_mod",
    )(cvecs, w_ada, b_ada.reshape(DEPTH, 1, 6 * D_MODEL))


def _inproj_kernel(x_ref, g_ref, shift_ref, scale_ref, w_ref, om_ref, os_ref, h_ref, *, n_main):
    j = pl.program_id(1)

    @pl.when(j == 0)
    def _():
        x = x_ref[...]
        h = x * lax.rsqrt(jnp.mean(x * x, axis=-1, keepdims=True) + EPS) * g_ref[...]
        h = h * (1.0 + scale_ref[...]) + shift_ref[...]
        h_ref[...] = h.astype(BF16)

    acc = _dot(h_ref[...], w_ref[...])

    @pl.when(j < n_main)
    def _():
        om_ref[...] = acc.astype(BF16)

    @pl.when(j == n_main)
    def _():
        os_ref[...] = acc[:, :S_COLS]


def _relayout_w_in(w):
    parts = [w[:, s:s + n] for s, n in _MAIN_SEGMENTS + _SMALL_SEGMENTS]
    parts.append(jnp.zeros((w.shape[0], PROJ_TN - 2 * SSD_HEADS - 2 * GLA_LOWRANK), w.dtype))
    return jnp.concatenate(parts, axis=1).astype(BF16)


def _in_projection(x2, mod_l, norm_g, w_perm, rows_per_mod, mod_row0):
    T = x2.shape[0]
    tm = min(PROJ_TM, rows_per_mod)
    n_main = M_COLS // PROJ_TN
    per = rows_per_mod // tm

    def mod_map(col):
        return lambda i, j: (mod_row0 + i // per, 0, col)

    return pl.pallas_call(
        functools.partial(_inproj_kernel, n_main=n_main),
        grid=(T // tm, n_main + 1),
        in_specs=[
            pl.BlockSpec((tm, D_MODEL), lambda i, j: (i, 0)),
            pl.BlockSpec((1, D_MODEL), lambda i, j: (0, 0)),
            pl.BlockSpec((None, 1, D_MODEL), mod_map(0)),
            pl.BlockSpec((None, 1, D_MODEL), mod_map(1)),
            pl.BlockSpec((D_MODEL, PROJ_TN), lambda i, j: (0, j)),
        ],
        out_specs=[
            pl.BlockSpec((tm, PROJ_TN), lambda i, j: (i, jnp.minimum(j, n_main - 1))),
            pl.BlockSpec((tm, S_COLS), lambda i, j: (i, 0)),
        ],
        out_shape=[jax.ShapeDtypeStruct((T, M_COLS), BF16), jax.ShapeDtypeStruct((T, S_COLS), F32)],
        scratch_shapes=[pltpu.VMEM((tm, D_MODEL), BF16)],
        compiler_params=_cparams(("arbitrary", "arbitrary")),
        name="in_projection",
    )(x2, norm_g.reshape(1, D_MODEL), mod_l, mod_l, w_perm)


_CONV_WIN = 128
_CONV_LEAD = 64
_CONV_OFF = 48


def _ssd_kernel(*refs, L, use_s0, emit_state):
    it = iter(refs)
    x_ref, b_ref, c_ref, ps_ref = next(it), next(it), next(it), next(it)
    cwx_ref, cwb_ref, cwc_ref = next(it), next(it), next(it)
    cbx_ref, cbb_ref, cbc_ref = next(it), next(it), next(it)
    dtb_ref, alog_ref, dsk_ref = next(it), next(it), next(it)
    s0_ref = next(it) if use_s0 else None
    y_ref = next(it)
    so_ref = next(it) if emit_state else None
    xpad, bpad, cpad, xc, bc, cc, dts, st = (next(it) for _ in range(8))

    Q = SSD_Q
    nc = L // Q
    g = pl.program_id(1)
    pad = SSD_CONV_W // 2

    for src, dst in ((x_ref, xpad), (b_ref, bpad), (c_ref, cpad)):
        w = dst.shape[1]
        dst[0:_CONV_LEAD, :] = jnp.zeros((_CONV_LEAD, w), BF16)
        dst[_CONV_LEAD + L:_CONV_LEAD + L + _CONV_LEAD, :] = jnp.zeros((_CONV_LEAD, w), BF16)

    def copy_body(c, carry):
        r0 = pl.multiple_of(c * Q, Q)
        for src, dst in ((x_ref, xpad), (b_ref, bpad), (c_ref, cpad)):
            dst[pl.ds(r0 + _CONV_LEAD, Q), :] = src[pl.ds(r0, Q), :]
        return carry

    lax.fori_loop(0, nc, copy_body, 0)

    ri = lax.broadcasted_iota(jnp.int32, (SSD_CONV_W * Q, _CONV_WIN), 0)
    ci = lax.broadcasted_iota(jnp.int32, (SSD_CONV_W * Q, _CONV_WIN), 1)
    shift_sel = (ci == (ri % Q) + (ri // Q) + (_CONV_LEAD - _CONV_OFF - pad)).astype(BF16)

    def conv_chunk(pad_ref, w_ref, bias_ref, r0):
        win = pad_ref[pl.ds(r0 + _CONV_OFF, _CONV_WIN), :]
        sh = _dot(shift_sel, win)
        acc = bias_ref[...] + w_ref[0:1, :] * sh[0:Q]
        for k in range(1, SSD_CONV_W):
            acc = acc + w_ref[k:k + 1, :] * sh[k * Q:(k + 1) * Q]
        return _silu(acc)

    def prep_body(c, carry):
        r0 = pl.multiple_of(c * Q, Q)
        xv = conv_chunk(xpad, cwx_ref, cbx_ref, r0)
        xc[pl.ds(r0, Q), :] = xv
        bc[pl.ds(r0, Q), :] = conv_chunk(bpad, cwb_ref, cbb_ref, r0)
        cc[pl.ds(r0, Q), :] = conv_chunk(cpad, cwc_ref, cbc_ref, r0)
        dts[pl.ds(r0, Q), :] = _softplus(ps_ref[pl.ds(r0, Q), :] + dtb_ref[...])
        y_ref[pl.ds(r0, Q), :] = dsk_ref[...] * xv
        return carry

    lax.fori_loop(0, nc, prep_body, 0)

    GW = SSD_GW
    hpg = SSD_HEADS // SSD_GROUPS
    er = lax.broadcasted_iota(jnp.int32, (S_COLS, GW), 0)
    ec = lax.broadcasted_iota(jnp.int32, (S_COLS, GW), 1)
    qi = lax.broadcasted_iota(jnp.int32, (Q, GW), 0)
    qj = lax.broadcasted_iota(jnp.int32, (Q, GW), 1) % Q
    diag_sel = (qi == qj).astype(F32)
    ones_q = jnp.ones((Q, Q), BF16)
    ti = lax.broadcasted_iota(jnp.int32, (Q, Q), 0)
    tj = lax.broadcasted_iota(jnp.int32, (Q, Q), 1)
    lane = lax.broadcasted_iota(jnp.int32, (Q, 128), 1)
    lo_half = lane < SSD_HEAD_DIM
    a_all = -jnp.exp(alog_ref[...])

    per_dir = []
    for d in range(2):
        base = S_DT + d * SSD_HEADS + g * hpg
        esel = (er == base + ec // SSD_HEAD_DIM).astype(BF16)
        a_exp = _sel_dot_r(jnp.broadcast_to(a_all, (8, S_COLS)), esel)[0:1]
        if d == 0:
            tri = (tj <= ti).astype(BF16)
            mask = qj <= qi
        else:
            tri = (tj >= ti).astype(BF16)
            mask = qj >= qi
        per_dir.append((esel, a_exp, tri, mask))

    for d in range(2):
        if use_s0:
            st[d] = s0_ref[d].T
        else:
            st[d] = jnp.zeros((SSD_STATE, GW), F32)

    def chunk_dir(c, d):
        esel, a_exp, tri, mask = per_dir[d]
        r0 = pl.multiple_of(c * Q, Q)
        dt_exp = _sel_dot_r(dts[pl.ds(r0, Q), :], esel)
        cum = _sel_dot_l(tri, dt_exp * a_exp)
        rowb = _sel_dot_l(ones_q, cum * diag_sel)
        lmat = jnp.exp(jnp.where(mask, cum - rowb, -jnp.inf))
        bq = bc[pl.ds(r0, Q), :].astype(BF16)
        cq = cc[pl.ds(r0, Q), :].astype(BF16)
        cb = _dot_nt(cq, jnp.concatenate([bq] * (GW // Q), axis=0))
        amat = (cb * lmat).astype(BF16)
        xq = xc[pl.ds(r0, Q), :] * dt_exp
        parts = []
        for p in range(GW // 128):
            xp = xq[:, p * 128:(p + 1) * 128]
            xbd = jnp.concatenate([jnp.where(lo_half, xp, 0.0), jnp.where(lo_half, 0.0, xp)], axis=0)
            parts.append(_dot(amat[:, p * 128:(p + 1) * 128], xbd.astype(BF16)))
        y_intra = jnp.concatenate(parts, axis=1)
        s_t = st[d]
        y_inter = _dot(cq, s_t.astype(BF16)) * jnp.exp(cum)
        cum_last = cum[Q - 1:Q] if d == 0 else cum[0:1]
        xdec = (xq * jnp.exp(cum_last - cum)).astype(BF16)
        st[d] = s_t * jnp.exp(cum_last) + _dot_tn(bq, xdec)
        y_ref[pl.ds(r0, Q), :] += y_intra + y_inter

    def scan_body(i, carry):
        chunk_dir(i, 0)
        chunk_dir(nc - 1 - i, 1)
        return carry

    lax.fori_loop(0, nc, scan_body, 0)

    if emit_state:
        for d in range(2):
            so_ref[d] = st[d].T


def _ssd_branch(pm, ps, conv_w, conv_b, dt_bias, a_log, d_skip, s0, B, L, emit_state):
    use_s0 = s0 is not None
    G, GW = SSD_GROUPS, SSD_GW
    hpg = SSD_HEADS // G
    dtb = jnp.zeros((1, S_COLS), F32).at[0, S_DT:S_DT + 2 * SSD_HEADS].set(dt_bias.reshape(-1))
    alog = jnp.zeros((1, S_COLS), F32).at[0, S_DT:S_DT + 2 * SSD_HEADS].set(a_log.reshape(-1))
    dsk = jnp.repeat(d_skip, SSD_HEAD_DIM).reshape(1, SSD_INNER)
    cb2 = conv_b.reshape(1, SSD_CONV_DIM)
    nb_x = M_X // GW
    in_specs = [
        pl.BlockSpec((L, GW), lambda b, g: (b, nb_x + g)),
        pl.BlockSpec((L, SSD_STATE), lambda b, g: (b, M_B // SSD_STATE + g)),
        pl.BlockSpec((L, SSD_STATE), lambda b, g: (b, M_C // SSD_STATE + g)),
        pl.BlockSpec((L, S_COLS), lambda b, g: (b, 0)),
        pl.BlockSpec((SSD_CONV_W, GW), lambda b, g: (0, g)),
        pl.BlockSpec((SSD_CONV_W, SSD_STATE), lambda b, g: (0, SSD_INNER // SSD_STATE + g)),
        pl.BlockSpec((SSD_CONV_W, SSD_STATE), lambda b, g: (0, (SSD_INNER + SSD_GN) // SSD_STATE + g)),
        pl.BlockSpec((1, GW), lambda b, g: (0, g)),
        pl.BlockSpec((1, SSD_STATE), lambda b, g: (0, SSD_INNER // SSD_STATE + g)),
        pl.BlockSpec((1, SSD_STATE), lambda b, g: (0, (SSD_INNER + SSD_GN) // SSD_STATE + g)),
        pl.BlockSpec((1, S_COLS), lambda b, g: (0, 0)),
        pl.BlockSpec((1, S_COLS), lambda b, g: (0, 0)),
        pl.BlockSpec((1, GW), lambda b, g: (0, g)),
    ]
    args = [pm, pm, pm, ps, conv_w, conv_w, conv_w, cb2, cb2, cb2, dtb, alog, dsk]
    state_spec = pl.BlockSpec((None, 2, None, GW, SSD_STATE), lambda b, g: (b, 0, g, 0, 0))
    if use_s0:
        in_specs.append(state_spec)
        args.append(s0.reshape(B, 2, G, GW, SSD_STATE))
    out_specs = [pl.BlockSpec((L, GW), lambda b, g: (b, g))]
    out_shape = [jax.ShapeDtypeStruct((B * L, SSD_INNER), F32)]
    if emit_state:
        out_specs.append(state_spec)
        out_shape.append(jax.ShapeDtypeStruct((B, 2, G, GW, SSD_STATE), F32))
    plen = L + 2 * _CONV_LEAD
    outs = pl.pallas_call(
        functools.partial(_ssd_kernel, L=L, use_s0=use_s0, emit_state=emit_state),
        grid=(B, G),
        in_specs=in_specs,
        out_specs=out_specs,
        out_shape=out_shape,
        scratch_shapes=[
            pltpu.VMEM((plen, GW), BF16), pltpu.VMEM((plen, SSD_STATE), BF16), pltpu.VMEM((plen, SSD_STATE), BF16),
            pltpu.VMEM((L, GW), F32), pltpu.VMEM((L, SSD_STATE), F32), pltpu.VMEM((L, SSD_STATE), F32),
            pltpu.VMEM((L, S_COLS), F32), pltpu.VMEM((2, SSD_STATE, GW), F32),
        ],
        compiler_params=_cparams(("arbitrary", "arbitrary")),
        name="ssd_scan",
    )(*args)
    y = outs[0]
    state = outs[1].reshape(B, 2, SSD_HEADS, SSD_HEAD_DIM, SSD_STATE) if emit_state else None
    return y, state


def _ctx_attn_kernel(q_ref, k_ref, v_ref, o_ref):
    q = q_ref[...]
    k = k_ref[...]
    v = v_ref[...]
    lane = lax.broadcasted_iota(jnp.int32, q.shape, 1)
    outs = []
    for hh in range(2):
        sel = (lane < NA_HEAD_DIM) if hh == 0 else (lane >= NA_HEAD_DIM)
        qm = jnp.where(sel, q, jnp.zeros_like(q))
        s = _dot_nt(qm, k) * NA_SCALE
        m = jnp.max(s, axis=-1, keepdims=True)
        p = jnp.exp(s - m)
        den = jnp.sum(p, axis=-1, keepdims=True)
        outs.append(_dot(p.astype(BF16), v) / den)
    lane_o = lax.broadcasted_iota(jnp.int32, outs[0].shape, 1)
    o_ref[...] = jnp.where(lane_o < NA_HEAD_DIM, outs[0], outs[1]).astype(o_ref.dtype)


def _context_attention(pm, B, L):
    nq, nk, nv = M_Q // 128, M_K // 128, M_V // 128
    return pl.pallas_call(
        _ctx_attn_kernel,
        grid=(B, NA_HEADS // 2),
        in_specs=[
            pl.BlockSpec((L, 128), lambda b, h: (b, nq + h)),
            pl.BlockSpec((L, 128), lambda b, h: (b, nk + h)),
            pl.BlockSpec((L, 128), lambda b, h: (b, nv + h)),
        ],
        out_specs=pl.BlockSpec((L, 128), lambda b, h: (b, h)),
        out_shape=jax.ShapeDtypeStruct((B * L, NA_INNER), BF16),
        compiler_params=_cparams(("arbitrary", "arbitrary")),
        name="context_attention",
    )(pm, pm, pm)


def _na_bias_table(rpb):
    qc = np.arange(GRID_W)
    cstart = np.clip(qc - NA_WIN_COLS // 2, 0, GRID_W - NA_WIN_COLS)
    kc = np.arange(GRID_W)
    valid = (kc[None, :] >= cstart[:, None]) & (kc[None, :] < cstart[:, None] + NA_WIN_COLS)
    dx = np.clip(kc[None, :] - qc[:, None] + NA_WIN_COLS - 1, 0, 2 * NA_WIN_COLS - 2)
    d0 = np.arange(NA_WIN_ROWS)
    kr = np.arange(NA_WIN_ROWS)
    row_idx = (d0[:, None, None, None] + kr[None, None, :, None]) + np.zeros((1, GRID_W, 1, GRID_W), np.int64)
    col_idx = np.broadcast_to(dx[None, :, None, :], row_idx.shape)
    t = rpb[:, row_idx, col_idx]
    t = jnp.where(jnp.asarray(valid)[None, None, :, None, :], t, NA_NEG)
    return t.reshape(rpb.shape[0], NA_WIN_ROWS, GRID_W, NA_WIN_ROWS * GRID_W).astype(F32)


def _na_kernel(q_ref, k_ref, v_ref, ck_ref, cv_ref, bias_ref, o_ref, ckb, cvb, *, rows):
    ckb[...] = ck_ref[...].astype(BF16)
    cvb[...] = cv_ref[...].astype(BF16)
    wr = NA_WIN_ROWS
    nloc = wr * GRID_W
    lane = lax.broadcasted_iota(jnp.int32, (GRID_W, 128), 1)

    def row_body(r, carry):
        rs = jnp.clip(r - wr // 2, 0, rows - wr)
        d0 = rs - r + wr - 1
        q = q_ref[pl.ds(pl.multiple_of(r * GRID_W, GRID_W), GRID_W), :]
        k0 = pl.multiple_of(rs * GRID_W, GRID_W)
        kw = k_ref[pl.ds(k0, nloc), :]
        vw = v_ref[pl.ds(k0, nloc), :]
        outs = []
        for hh in range(2):
            sel = (lane < NA_HEAD_DIM) if hh == 0 else (lane >= NA_HEAD_DIM)
            qm = jnp.where(sel, q, jnp.zeros_like(q))
            s_loc = _dot_nt(qm, kw) * NA_SCALE + bias_ref[hh, d0]
            s_ctx = _dot_nt(qm, ckb[...]) * NA_SCALE
            m = jnp.maximum(jnp.max(s_loc, axis=-1, keepdims=True), jnp.max(s_ctx, axis=-1, keepdims=True))
            p_loc = jnp.exp(s_loc - m)
            p_ctx = jnp.exp(s_ctx - m)
            den = jnp.sum(p_loc, axis=-1, keepdims=True) + jnp.sum(p_ctx, axis=-1, keepdims=True)
            outs.append((_dot(p_loc.astype(BF16), vw) + _dot(p_ctx.astype(BF16), cvb[...])) / den)
        o = jnp.where(lane < NA_HEAD_DIM, outs[0], outs[1])
        o_ref[pl.ds(pl.multiple_of(r * GRID_W, GRID_W), GRID_W), :] = o.astype(o_ref.dtype)
        return carry

    lax.fori_loop(0, rows, row_body, 0)


def _neighbourhood_attention(pm, ck, cv, bias_tab, B, S):
    rows = S // GRID_W
    assert rows >= NA_WIN_ROWS
    Lc = ck.shape[1]
    nq, nk, nv = M_Q // 128, M_K // 128, M_V // 128
    return pl.pallas_call(
        functools.partial(_na_kernel, rows=rows),
        grid=(NA_HEADS // 2, B),
        in_specs=[
            pl.BlockSpec((S, 128), lambda h, b: (b, nq + h)),
            pl.BlockSpec((S, 128), lambda h, b: (b, nk + h)),
            pl.BlockSpec((S, 128), lambda h, b: (b, nv + h)),
            pl.BlockSpec((None, Lc, 128), lambda h, b: (b, 0, h)),
            pl.BlockSpec((None, Lc, 128), lambda h, b: (b, 0, h)),
            pl.BlockSpec((2, NA_WIN_ROWS, GRID_W, NA_WIN_ROWS * GRID_W), lambda h, b: (h, 0, 0, 0)),
        ],
        out_specs=pl.BlockSpec((S, 128), lambda h, b: (b, h)),
        out_shape=jax.ShapeDtypeStruct((B * S, NA_INNER), BF16),
        scratch_shapes=[pltpu.VMEM((Lc, 128), BF16), pltpu.VMEM((Lc, 128), BF16)],
        compiler_params=_cparams(("arbitrary", "arbitrary")),
        name="neighbourhood_attention",
    )(pm, pm, pm, ck, cv, bias_tab)


def _gla_kernel(*refs, L, use_s0, emit_state):
    it = iter(refs)
    q_ref, k_ref, v_ref, r_ref, ps_ref = (next(it) for _ in range(5))
    wg0_ref, wg1_ref, bg0_ref, bg1_ref, ng_ref = (next(it) for _ in range(5))
    s0_ref = next(it) if use_s0 else None
    o_ref = next(it)
    so_ref = next(it) if emit_state else None
    gs, acc, st = next(it), next(it), next(it)

    Q = GLA_Q
    nc = L // Q
    RC = 256 if L % 256 == 0 else L
    nrc = L // RC

    def gate_body(c, carry):
        r0 = pl.multiple_of(c * RC, RC)
        lr = ps_ref[pl.ds(r0, RC), :].astype(BF16)
        for d, (wg, bg) in enumerate(((wg0_ref, bg0_ref), (wg1_ref, bg1_ref))):
            gpre = _dot(lr, wg[...].astype(BF16)) + bg[...]
            gs[d, pl.ds(r0, RC), :] = (jnp.minimum(gpre, 0.0) - jnp.log1p(jnp.exp(-jnp.abs(gpre)))) / GLA_TAU
        acc[pl.ds(r0, RC), :] = jnp.zeros((RC, GLA_DV_HEAD), F32)
        return carry

    lax.fori_loop(0, nrc, gate_body, 0)

    ti = lax.broadcasted_iota(jnp.int32, (Q, Q), 0)
    tj = lax.broadcasted_iota(jnp.int32, (Q, Q), 1)
    tris = ((tj <= ti), (tj >= ti))
    for d in range(2):
        if use_s0:
            st[d] = s0_ref[d].T
        else:
            st[d] = jnp.zeros((GLA_DV_HEAD, GLA_DK_HEAD), F32)

    def chunk_dir(c, d):
        r0 = pl.multiple_of(c * Q, Q)
        mask = tris[d]
        bcum = _sel_dot_l(mask.astype(BF16), gs[d, pl.ds(r0, Q), :])
        b_end = bcum[Q - 1:Q] if d == 0 else bcum[0:1]
        qc = q_ref[pl.ds(r0, Q), :].astype(F32) * GLA_QK_SCALE
        kc = k_ref[pl.ds(r0, Q), :].astype(F32)
        vc = v_ref[pl.ds(r0, Q), :]
        qt = (qc * jnp.exp(bcum)).astype(BF16)
        kt = (kc * jnp.exp(-bcum)).astype(BF16)
        kend = (kc * jnp.exp(b_end - bcum)).astype(BF16)
        att = jnp.where(mask, _dot_nt(qt, kt), 0.0)
        s_t = st[d]
        o = _dot(att.astype(BF16), vc) + _dot_nt(qt, s_t.astype(BF16))
        st[d] = s_t * jnp.exp(b_end) + _dot_tn(vc, kend)
        acc[pl.ds(r0, Q), :] += o

    def scan_body(i, carry):
        chunk_dir(i, 0)
        chunk_dir(nc - 1 - i, 1)
        return carry

    lax.fori_loop(0, nc, scan_body, 0)

    def fin_body(c, carry):
        r0 = pl.multiple_of(c * RC, RC)
        o = acc[pl.ds(r0, RC), :]
        o = o * lax.rsqrt(jnp.mean(o * o, axis=-1, keepdims=True) + EPS) * ng_ref[...]
        o_ref[pl.ds(r0, RC), :] = (o * _silu(r_ref[pl.ds(r0, RC), :].astype(F32))).astype(o_ref.dtype)
        return carry

    lax.fori_loop(0, nrc, fin_body, 0)

    if emit_state:
        for d in range(2):
            so_ref[d] = st[d].T


def _gla_branch(pm, ps, w_gate, b_gate, norm_g, s0, B, L, emit_state):
    use_s0 = s0 is not None
    H, DK, DV = GLA_HEADS, GLA_DK_HEAD, GLA_DV_HEAD
    wg = jnp.zeros((S_COLS, 2 * GLA_DK), F32)
    for d in range(2):
        wg = wg.at[S_LR + d * GLA_LOWRANK:S_LR + (d + 1) * GLA_LOWRANK, d * GLA_DK:(d + 1) * GLA_DK].set(w_gate[d])
    bg = b_gate.reshape(1, 2 * GLA_DK)
    in_specs = [
        pl.BlockSpec((L, DK), lambda b, h: (b, M_GQ // DK + h)),
        pl.BlockSpec((L, DK), lambda b, h: (b, M_GK // DK + h)),
        pl.BlockSpec((L, DV), lambda b, h: (b, M_GV // DV + h)),
        pl.BlockSpec((L, DV), lambda b, h: (b, M_R // DV + h)),
        pl.BlockSpec((L, S_COLS), lambda b, h: (b, 0)),
        pl.BlockSpec((S_COLS, DK), lambda b, h: (0, h)),
        pl.BlockSpec((S_COLS, DK), lambda b, h: (0, H + h)),
        pl.BlockSpec((1, DK), lambda b, h: (0, h)),
        pl.BlockSpec((1, DK), lambda b, h: (0, H + h)),
        pl.BlockSpec((1, DV), lambda b, h: (0, 0)),
    ]
    args = [pm, pm, pm, pm, ps, wg, wg, bg, bg, norm_g.reshape(1, DV)]
    state_spec = pl.BlockSpec((None, 2, None, DK, DV), lambda b, h: (b, 0, h, 0, 0))
    if use_s0:
        in_specs.append(state_spec)
        args.append(s0)
    out_specs = [pl.BlockSpec((L, DV), lambda b, h: (b, h))]
    out_shape = [jax.ShapeDtypeStruct((B * L, GLA_DV), BF16)]
    if emit_state:
        out_specs.append(state_spec)
        out_shape.append(jax.ShapeDtypeStruct((B, 2, H, DK, DV), F32))
    outs = pl.pallas_call(
        functools.partial(_gla_kernel, L=L, use_s0=use_s0, emit_state=emit_state),
        grid=(B, H),
        in_specs=in_specs,
        out_specs=out_specs,
        out_shape=out_shape,
        scratch_shapes=[pltpu.VMEM((2, L, DK), F32), pltpu.VMEM((L, DV), F32), pltpu.VMEM((2, DV, DK), F32)],
        compiler_params=_cparams(("arbitrary", "arbitrary")),
        name="gla_scan",
    )(*args)
    return outs[0], (outs[1] if emit_state else None)


def _merge_kernel(x_ref, yssd_ref, z_ref, ona_ref, ogla_ref, g0_ref, g1_ref, g2_ref,
                  gate1_ref, shift2_ref, scale2_ref, sng_ref, n2g_ref,
                  wb_ref, wo_ref, rw_ref, rb_ref, xo_ref, h2_ref, lg_ref):
    y = yssd_ref[...] * _silu(z_ref[...].astype(F32))
    y = y * lax.rsqrt(jnp.mean(y * y, axis=-1, keepdims=True) + EPS) * sng_ref[...]
    m = jax.nn.sigmoid(g0_ref[...].astype(F32)) * _dot(y.astype(BF16), wb_ref[0])
    m = m + jax.nn.sigmoid(g1_ref[...].astype(F32)) * _dot(ona_ref[...], wb_ref[1])
    m = m + jax.nn.sigmoid(g2_ref[...].astype(F32)) * _dot(ogla_ref[...], wb_ref[2])
    x = x_ref[...] + gate1_ref[...] * _dot(m.astype(BF16), wo_ref[...])
    xo_ref[...] = x
    h2 = x * lax.rsqrt(jnp.mean(x * x, axis=-1, keepdims=True) + EPS) * n2g_ref[...]
    h2 = h2 * (1.0 + scale2_ref[...]) + shift2_ref[...]
    h2_ref[...] = h2.astype(BF16)
    lg_ref[...] = jnp.dot(h2, rw_ref[...], preferred_element_type=F32,
                          precision=lax.Precision.HIGHEST) + rb_ref[...]


def _merge(x2, y_ssd, pm, o_na, o_gla, mod_l, ssd_norm_g, norm2_g, wb_bf, wo_bf, rw_pad, rb_pad,
           rows_per_mod, mod_row0):
    T = x2.shape[0]
    tm = min(256, rows_per_mod)
    per = rows_per_mod // tm
    D = D_MODEL
    ng = M_GATES // D

    def mod_map(col):
        return lambda i: (mod_row0 + i // per, 0, col)

    row = lambda i: (i, 0)
    const2 = lambda i: (0, 0)
    return pl.pallas_call(
        _merge_kernel,
        grid=(T // tm,),
        in_specs=[
            pl.BlockSpec((tm, D), row),
            pl.BlockSpec((tm, D), row),
            pl.BlockSpec((tm, D), lambda i: (i, M_Z // D)),
            pl.BlockSpec((tm, D), row),
            pl.BlockSpec((tm, D), row),
            pl.BlockSpec((tm, D), lambda i: (i, ng)),
            pl.BlockSpec((tm, D), lambda i: (i, ng + 1)),
            pl.BlockSpec((tm, D), lambda i: (i, ng + 2)),
            pl.BlockSpec((None, 1, D), mod_map(2)),
            pl.BlockSpec((None, 1, D), mod_map(3)),
            pl.BlockSpec((None, 1, D), mod_map(4)),
            pl.BlockSpec((1, D), const2),
            pl.BlockSpec((1, D), const2),
            pl.BlockSpec((N_BRANCH, D, D), lambda i: (0, 0, 0)),
            pl.BlockSpec((D, D), const2),
            pl.BlockSpec((D, 128), const2),
            pl.BlockSpec((1, 128), const2),
        ],
        out_specs=[pl.BlockSpec((tm, D), row), pl.BlockSpec((tm, D), row), pl.BlockSpec((tm, 128), row)],
        out_shape=[jax.ShapeDtypeStruct((T, D), F32), jax.ShapeDtypeStruct((T, D), BF16),
                   jax.ShapeDtypeStruct((T, 128), F32)],
        compiler_params=_cparams(("arbitrary",)),
        name="branch_merge",
    )(x2, y_ssd, pm, o_na, o_gla, pm, pm, pm, mod_l, mod_l, mod_l,
      ssd_norm_g.reshape(1, D), norm2_g.reshape(1, D), wb_bf, wo_bf, rw_pad, rb_pad)


def _moe_kernel(be_ref, nused_ref, x_ref, wgu_ref, bgu_ref, wdn_ref, bdn_ref, o_ref, wgu_bf, wdn_bf):
    b = pl.program_id(0)
    used = b < nused_ref[0]
    prev = be_ref[jnp.maximum(b - 1, 0)]
    new_expert = jnp.logical_or(b == 0, be_ref[b] != prev)

    @pl.when(jnp.logical_and(used, new_expert))
    def _():
        rc = 128

        def cast_body(i, carry):
            r0 = pl.multiple_of(i * rc, rc)
            wgu_bf[pl.ds(r0, rc), :] = wgu_ref[pl.ds(r0, rc), :].astype(BF16)
            wdn_bf[pl.ds(r0, rc), :] = wdn_ref[pl.ds(r0, rc), :].astype(BF16)
            return carry

        lax.fori_loop(0, D_MODEL // rc, cast_body, 0)

    @pl.when(used)
    def _():
        x = x_ref[...]
        acc = jnp.zeros((MOE_ROWS, D_MODEL), F32) + bdn_ref[...]
        for f in range(D_FF // MOE_FF_TILE):
            c0 = f * MOE_FF_TILE
            glu = _dot(x, wgu_bf[:, c0:c0 + MOE_FF_TILE]) + bgu_ref[:, c0:c0 + MOE_FF_TILE]
            lin = _dot(x, wgu_bf[:, D_FF + c0:D_FF + c0 + MOE_FF_TILE]) + bgu_ref[:, D_FF + c0:D_FF + c0 + MOE_FF_TILE]
            glu = jnp.minimum(glu, SWIGLU_LIMIT)
            lin = jnp.clip(lin, -SWIGLU_LIMIT, SWIGLU_LIMIT)
            act = glu * jax.nn.sigmoid(SWIGLU_ALPHA * glu) * (lin + 1.0)
            acc = acc + _dot(act.astype(BF16), wdn_bf[c0:c0 + MOE_FF_TILE, :])
        o_ref[...] = acc.astype(o_ref.dtype)

    @pl.when(jnp.logical_not(used))
    def _():
        o_ref[...] = jnp.zeros(o_ref.shape, o_ref.dtype)


def _moe_experts(xs, block_e, n_used, w_gu, b_gu, w_dn, b_dn, layer):
    n_blocks = xs.shape[0] // MOE_ROWS
    D = D_MODEL
    grid_spec = pltpu.PrefetchScalarGridSpec(
        num_scalar_prefetch=2,
        grid=(n_blocks,),
        in_specs=[
            pl.BlockSpec((MOE_ROWS, D), lambda b, be, nu: (b, 0)),
            pl.BlockSpec((None, None, D, 2 * D_FF), lambda b, be, nu: (layer, be[b], 0, 0)),
            pl.BlockSpec((None, None, 1, 2 * D_FF), lambda b, be, nu: (layer, be[b], 0, 0)),
            pl.BlockSpec((None, None, D_FF, D), lambda b, be, nu: (layer, be[b], 0, 0)),
            pl.BlockSpec((None, None, 1, D), lambda b, be, nu: (layer, be[b], 0, 0)),
        ],
        out_specs=pl.BlockSpec((MOE_ROWS, D), lambda b, be, nu: (b, 0)),
        scratch_shapes=[pltpu.VMEM((D, 2 * D_FF), BF16), pltpu.VMEM((D_FF, D), BF16)],
    )
    return pl.pallas_call(
        _moe_kernel,
        grid_spec=grid_spec,
        out_shape=jax.ShapeDtypeStruct(xs.shape, BF16),
        compiler_params=_cparams(("arbitrary",)),
        name="moe_experts",
    )(block_e, n_used, xs, w_gu, b_gu.reshape(DEPTH, N_EXPERTS, 1, 2 * D_FF), w_dn,
      b_dn.reshape(DEPTH, N_EXPERTS, 1, D))


def _moe_ffn(h2, logits, w_gu, b_gu, w_dn, b_dn, layer):
    T = h2.shape[0]
    TK = T * TOP_K
    top_logit, top_idx = lax.top_k(logits, TOP_K)
    top_w = jax.nn.softmax(top_logit, axis=-1)
    flat_e = top_idx.reshape(TK).astype(jnp.int32)
    onehot = (flat_e[:, None] == jnp.arange(N_EXPERTS, dtype=jnp.int32)[None, :]).astype(jnp.int32)
    csum = jnp.cumsum(onehot, axis=0)
    rank = jnp.sum(onehot * csum, axis=1) - 1
    counts = csum[-1]
    padded = (counts + MOE_ROWS - 1) // MOE_ROWS * MOE_ROWS
    pad_end = jnp.cumsum(padded)
    pad_start = pad_end - padded
    dest = pad_start[flat_e] + rank
    n_blocks = -(-(TK + N_EXPERTS * (MOE_ROWS - 1)) // MOE_ROWS)
    n_slots = n_blocks * MOE_ROWS
    flat_t = jnp.arange(TK, dtype=jnp.int32) // TOP_K
    slot_tok = jnp.full((n_slots,), T, jnp.int32).at[dest].set(flat_t)
    block_e = jnp.minimum(jnp.searchsorted(pad_end, jnp.arange(n_blocks, dtype=jnp.int32) * MOE_ROWS, side="right"),
                          N_EXPERTS - 1).astype(jnp.int32)
    n_used = (pad_end[-1] // MOE_ROWS).astype(jnp.int32).reshape(1)
    x_pad = jnp.concatenate([h2, jnp.zeros((1, D_MODEL), h2.dtype)], axis=0)
    xs = x_pad[slot_tok]
    yb = _moe_experts(xs, block_e, n_used, w_gu, b_gu, w_dn, b_dn, layer)
    yk = yb[dest].reshape(T, TOP_K, D_MODEL).astype(F32)
    return jnp.sum(yk * top_w[:, :, None], axis=1)


def _final_norm_kernel(x_ref, g_ref, o_ref):
    x = x_ref[...]
    o_ref[...] = x * lax.rsqrt(jnp.mean(x * x, axis=-1, keepdims=True) + EPS) * g_ref[...]


def _final_norm(x2, g):
    T = x2.shape[0]
    tm = 512
    return pl.pallas_call(
        _final_norm_kernel,
        grid=(T // tm,),
        in_specs=[pl.BlockSpec((tm, D_MODEL), lambda i: (i, 0)), pl.BlockSpec((1, D_MODEL), lambda i: (0, 0))],
        out_specs=pl.BlockSpec((tm, D_MODEL), lambda i: (i, 0)),
        out_shape=jax.ShapeDtypeStruct((T, D_MODEL), F32),
        compiler_params=_cparams(("arbitrary",)),
        name="final_norm",
    )(x2, g.reshape(1, D_MODEL))


def _mixer_half(x2, B, L, mod_l, mod_row0, lw, ctx):
    latent = ctx is not None
    rows_per_mod = L if latent else B * L
    pm, ps = _in_projection(x2, mod_l, lw["norm1_g"], lw["w_in"], rows_per_mod, mod_row0)
    if latent:
        ck, cv, s_ssd0, s_gla0 = ctx
    else:
        s_ssd0 = s_gla0 = None
    y_ssd, s_ssd = _ssd_branch(pm, ps, lw["ssd_conv_w"], lw["ssd_conv_b"], lw["ssd_dt_bias"], lw["ssd_a_log"],
                               lw["ssd_d"], s_ssd0, B, L, emit_state=not latent)
    if latent:
        o_na = _neighbourhood_attention(pm, ck, cv, lw["na_bias"], B, L)
    else:
        o_na = _context_attention(pm, B, L)
    o_gla, s_gla = _gla_branch(pm, ps, lw["gla_w_gate"], lw["gla_b_gate"], lw["gla_norm_g"], s_gla0, B, L,
                               emit_state=not latent)
    x_new, h2, logits = _merge(x2, y_ssd, pm, o_na, o_gla, mod_l, lw["ssd_norm_g"], lw["norm2_g"],
                               lw["w_branch"], lw["w_out"], lw["router_w"], lw["router_b"], rows_per_mod, mod_row0)
    return x_new, h2, logits[:, :N_EXPERTS], pm, s_ssd, s_gla


def kernel(x_prompt, x_sample, cache_na_k, cache_na_v, state_ssd, state_gla, c, c_ctx, w_ada, b_ada, norm1_g, norm2_g, w_in, ssd_conv_w, ssd_conv_b, ssd_dt_bias, ssd_a_log, ssd_d, ssd_norm_g, na_rpb, gla_w_gate, gla_b_gate, gla_norm_g, w_branch, w_out, router_w, router_b, moe_w_gu, moe_b_gu, moe_w_dn, moe_b_dn, final_norm_g):
    Bp, Lp, D = x_prompt.shape
    Bs, Ls, _ = x_sample.shape
    Tp, Ts = Bp * Lp, Bs * Ls
    Lc = cache_na_k.shape[2]

    cvecs = jnp.concatenate([c_ctx[None], c, jnp.zeros((8 - 1 - Bs, D), F32)], axis=0)
    mod = _modulation(cvecs, w_ada, b_ada).reshape(DEPTH, 8, 1, 6 * D)

    xp = x_prompt.reshape(Tp, D)
    xs = x_sample.reshape(Ts, D)
    ks_, vs_, sss_, sgs_ = [], [], [], []
    for l in range(DEPTH):
        lw = {
            "norm1_g": norm1_g[l], "norm2_g": norm2_g[l], "w_in": _relayout_w_in(w_in[l]),
            "ssd_conv_w": ssd_conv_w[l], "ssd_conv_b": ssd_conv_b[l], "ssd_dt_bias": ssd_dt_bias[l],
            "ssd_a_log": ssd_a_log[l], "ssd_d": ssd_d[l], "ssd_norm_g": ssd_norm_g[l],
            "na_bias": _na_bias_table(na_rpb[l]),
            "gla_w_gate": gla_w_gate[l], "gla_b_gate": gla_b_gate[l], "gla_norm_g": gla_norm_g[l],
            "w_branch": w_branch[l].astype(BF16), "w_out": w_out[l].astype(BF16),
            "router_w": jnp.pad(router_w[l], ((0, 0), (0, 128 - N_EXPERTS))),
            "router_b": jnp.pad(router_b[l], (0, 128 - N_EXPERTS)).reshape(1, 128),
        }
        mod_l = mod[l]
        xp, h2p, lgp, pmp, s_ssd, s_gla = _mixer_half(xp, Bp, Lp, mod_l, 0, lw, None)
        ctx = (cache_na_k[:, l].reshape(Bs, Lc, NA_INNER), cache_na_v[:, l].reshape(Bs, Lc, NA_INNER),
               state_ssd[:, l], state_gla[:, l])
        xs, h2s, lgs, _, _, _ = _mixer_half(xs, Bs, Ls, mod_l, 1, lw, ctx)
        ks_.append(pmp[:, M_K:M_K + NA_INNER].astype(F32).reshape(Bp, Lp, NA_HEADS, NA_HEAD_DIM))
        vs_.append(pmp[:, M_V:M_V + NA_INNER].astype(F32).reshape(Bp, Lp, NA_HEADS, NA_HEAD_DIM))
        sss_.append(s_ssd)
        sgs_.append(s_gla)

        y = _moe_ffn(jnp.concatenate([h2p, h2s], axis=0), jnp.concatenate([lgp, lgs], axis=0),
                     moe_w_gu, moe_b_gu, moe_w_dn, moe_b_dn, l)
        gate2_p = mod_l[0, :, 5 * D:]
        gate2_s = mod_l[1:1 + Bs, :, 5 * D:]
        xp = xp + gate2_p * y[:Tp]
        xs = (xs.reshape(Bs, Ls, D) + gate2_s * y[Tp:].reshape(Bs, Ls, D)).reshape(Ts, D)

    y_prompt = _final_norm(xp, final_norm_g).reshape(Bp, Lp, D)
    y_sample = _final_norm(xs, final_norm_g).reshape(Bs, Ls, D)
    return (y_prompt, y_sample, jnp.stack(ks_, axis=1), jnp.stack(vs_, axis=1),
            jnp.stack(sss_, axis=1), jnp.stack(sgs_, axis=1))
```

```python
import functools

import numpy as np
import jax
import jax.numpy as jnp
from jax import lax
from jax.experimental import pallas as pl
from jax.experimental.pallas import tpu as pltpu

F32 = jnp.float32
BF16 = jnp.bfloat16

D_MODEL = 1024
DEPTH = 4
GRID_W = 64
EPS = 1e-6

SSD_HEADS = 16
SSD_HEAD_DIM = 64
SSD_INNER = SSD_HEADS * SSD_HEAD_DIM
SSD_GROUPS = 2
SSD_STATE = 128
SSD_GN = SSD_GROUPS * SSD_STATE
SSD_CONV_W = 5
SSD_CONV_DIM = SSD_INNER + 2 * SSD_GN
SSD_Q = 64
SSD_GW = SSD_INNER // SSD_GROUPS

NA_HEADS = 16
NA_HEAD_DIM = 64
NA_INNER = NA_HEADS * NA_HEAD_DIM
NA_WIN_ROWS = 8
NA_WIN_COLS = 16
NA_SCALE = NA_HEAD_DIM ** -0.5
NA_NEG = -1e30

GLA_HEADS = 4
GLA_DK = D_MODEL // 2
GLA_DV = D_MODEL
GLA_DK_HEAD = GLA_DK // GLA_HEADS
GLA_DV_HEAD = GLA_DV // GLA_HEADS
GLA_LOWRANK = 16
GLA_TAU = 16.0
GLA_Q = 32
GLA_QK_SCALE = GLA_DK_HEAD ** -0.5

N_BRANCH = 3
N_EXPERTS = 32
TOP_K = 4
D_FF = D_MODEL
SWIGLU_ALPHA = 1.702
SWIGLU_LIMIT = 7.0
MOE_ROWS = 512
MOE_FF_TILE = 512
MOE_DMA_UNROLL = 8

SSD_UNROLL = 2
GLA_UNROLL = 4
NA_UNROLL = 4

OFF_SSD_XBC = SSD_INNER
OFF_SSD_DT = OFF_SSD_XBC + SSD_CONV_DIM
OFF_NA_QKV = OFF_SSD_DT + 2 * SSD_HEADS
OFF_GLA_Q = OFF_NA_QKV + 3 * NA_INNER
OFF_GLA_LR = OFF_GLA_Q + 2 * GLA_DK + 2 * GLA_DV
OFF_GATES = OFF_GLA_LR + 2 * GLA_LOWRANK
IN_COLS = OFF_GATES + N_BRANCH * D_MODEL

M_Z = 0
M_X = M_Z + SSD_INNER
M_Q = M_X + SSD_INNER
M_K = M_Q + NA_INNER
M_V = M_K + NA_INNER
M_GV = M_V + NA_INNER
M_R = M_GV + GLA_DV
M_GATES = M_R + GLA_DV
M_GQ = M_GATES + N_BRANCH * D_MODEL
M_GK = M_GQ + GLA_DK
M_B = M_GK + GLA_DK
M_C = M_B + SSD_GN
M_COLS = M_C + SSD_GN
_MAIN_SEGMENTS = (
    (0, 2 * SSD_INNER),
    (OFF_NA_QKV, 3 * NA_INNER),
    (OFF_GLA_Q + 2 * GLA_DK, 2 * GLA_DV),
    (OFF_GATES, N_BRANCH * D_MODEL),
    (OFF_GLA_Q, 2 * GLA_DK),
    (OFF_SSD_XBC + SSD_INNER, 2 * SSD_GN),
)
_SMALL_SEGMENTS = ((OFF_SSD_DT, 2 * SSD_HEADS), (OFF_GLA_LR, 2 * GLA_LOWRANK))
S_DT = 0
S_LR = 2 * SSD_HEADS
S_COLS = 128
PROJ_TN = 512
PROJ_COLS = M_COLS + PROJ_TN
PROJ_TM = 2048

VMEM_LIMIT = 56 * 1024 * 1024


def _cparams(sem):
    return pltpu.CompilerParams(dimension_semantics=sem, vmem_limit_bytes=VMEM_LIMIT)


def _silu(x):
    return x * jax.nn.sigmoid(x)


def _softplus(x):
    return jnp.maximum(x, 0.0) + jnp.log1p(jnp.exp(-jnp.abs(x)))


def _split3(x):
    hi = x.astype(BF16)
    r1 = x - hi.astype(F32)
    mid = r1.astype(BF16)
    lo = (r1 - mid.astype(F32)).astype(BF16)
    return hi, mid, lo


def _dot(a, b):
    return jnp.dot(a, b, preferred_element_type=F32)


def _dot_nt(a, b):
    return lax.dot_general(a, b, (((1,), (1,)), ((), ())), preferred_element_type=F32)


def _dot_tn(a, b):
    return lax.dot_general(a, b, (((0,), (0,)), ((), ())), preferred_element_type=F32)


def _sel_dot_l(sel_bf, x):
    hi, mid, lo = _split3(x)
    return (_dot(sel_bf, lo) + _dot(sel_bf, mid)) + _dot(sel_bf, hi)


def _sel_dot_r(x, sel_bf):
    hi, mid, lo = _split3(x)
    return (_dot(lo, sel_bf) + _dot(mid, sel_bf)) + _dot(hi, sel_bf)


def _mod_kernel(c_ref, w_ref, b_ref, o_ref):
    c = c_ref[...]
    o_ref[...] = jnp.dot(_silu(c), w_ref[...], preferred_element_type=F32,
                         precision=lax.Precision.HIGHEST) + b_ref[...]


def _modulation(cvecs, w_ada, b_ada):
    nrow = cvecs.shape[0]
    return pl.pallas_call(
        _mod_kernel,
        grid=(DEPTH, 6),
        in_specs=[
            pl.BlockSpec((nrow, D_MODEL), lambda l, j: (0, 0)),
            pl.BlockSpec((None, D_MODEL, D_MODEL), lambda l, j: (l, 0, j)),
            pl.BlockSpec((None, 1, D_MODEL), lambda l, j: (l, 0, j)),
        ],
        out_specs=pl.BlockSpec((None, nrow, D_MODEL), lambda l, j: (l, 0, j)),
        out_shape=jax.ShapeDtypeStruct((DEPTH, nrow, 6 * D_MODEL), F32),
        compiler_params=_cparams(("arbitrary", "arbitrary")),
        name="adaln_mod",
    )(cvecs, w_ada, b_ada.reshape(DEPTH, 1, 6 * D_MODEL))


def _inproj_kernel(x_ref, g_ref, shift_ref, scale_ref, w_ref, om_ref, os_ref, h_ref, *, n_main):
    j = pl.program_id(1)

    @pl.when(j == 0)
    def _():
        x = x_ref[...]
        h = x * lax.rsqrt(jnp.mean(x * x, axis=-1, keepdims=True) + EPS) * g_ref[...]
        h = h * (1.0 + scale_ref[...]) + shift_ref[...]
        h_ref[...] = h.astype(BF16)

    acc = _dot(h_ref[...], w_ref[...])

    @pl.when(j < n_main)
    def _():
        om_ref[...] = acc.astype(BF16)

    @pl.when(j == n_main)
    def _():
        os_ref[...] = acc[:, :S_COLS]


def _relayout_w_in(w):
    parts = [w[:, s:s + n] for s, n in _MAIN_SEGMENTS + _SMALL_SEGMENTS]
    parts.append(jnp.zeros((w.shape[0], PROJ_TN - 2 * SSD_HEADS - 2 * GLA_LOWRANK), w.dtype))
    return jnp.concatenate(parts, axis=1).astype(BF16)


def _in_projection(x2, mod_l, norm_g, w_perm, rows_per_mod, mod_row0):
    T = x2.shape[0]
    tm = min(PROJ_TM, rows_per_mod)
    n_main = M_COLS // PROJ_TN
    per = rows_per_mod // tm

    def mod_map(col):
        return lambda i, j: (mod_row0 + i // per, 0, col)

    return pl.pallas_call(
        functools.partial(_inproj_kernel, n_main=n_main),
        grid=(T // tm, n_main + 1),
        in_specs=[
            pl.BlockSpec((tm, D_MODEL), lambda i, j: (i, 0)),
            pl.BlockSpec((1, D_MODEL), lambda i, j: (0, 0)),
            pl.BlockSpec((None, 1, D_MODEL), mod_map(0)),
            pl.BlockSpec((None, 1, D_MODEL), mod_map(1)),
            pl.BlockSpec((D_MODEL, PROJ_TN), lambda i, j: (0, j)),
        ],
        out_specs=[
            pl.BlockSpec((tm, PROJ_TN), lambda i, j: (i, jnp.minimum(j, n_main - 1))),
            pl.BlockSpec((tm, S_COLS), lambda i, j: (i, 0)),
        ],
        out_shape=[jax.ShapeDtypeStruct((T, M_COLS), BF16), jax.ShapeDtypeStruct((T, S_COLS), F32)],
        scratch_shapes=[pltpu.VMEM((tm, D_MODEL), BF16)],
        compiler_params=_cparams(("arbitrary", "arbitrary")),
        name="in_projection",
    )(x2, norm_g.reshape(1, D_MODEL), mod_l, mod_l, w_perm)


_CONV_WIN = 128
_CONV_LEAD = 64
_CONV_OFF = 48


def _ssd_kernel(*refs, L, use_s0, emit_state):
    it = iter(refs)
    x_ref, b_ref, c_ref, ps_ref = next(it), next(it), next(it), next(it)
    cwx_ref, cwb_ref, cwc_ref = next(it), next(it), next(it)
    cbx_ref, cbb_ref, cbc_ref = next(it), next(it), next(it)
    dtb_ref, alog_ref, dsk_ref = next(it), next(it), next(it)
    s0_ref = next(it) if use_s0 else None
    y_ref = next(it)
    so_ref = next(it) if emit_state else None
    xpad, bpad, cpad, xc, bc, cc, dts, st = (next(it) for _ in range(8))

    Q = SSD_Q
    nc = L // Q
    g = pl.program_id(1)
    pad = SSD_CONV_W // 2

    for src, dst in ((x_ref, xpad), (b_ref, bpad), (c_ref, cpad)):
        w = dst.shape[1]
        dst[0:_CONV_LEAD, :] = jnp.zeros((_CONV_LEAD, w), BF16)
        dst[_CONV_LEAD + L:_CONV_LEAD + L + _CONV_LEAD, :] = jnp.zeros((_CONV_LEAD, w), BF16)

    def copy_body(c, carry):
        r0 = pl.multiple_of(c * Q, Q)
        for src, dst in ((x_ref, xpad), (b_ref, bpad), (c_ref, cpad)):
            dst[pl.ds(r0 + _CONV_LEAD, Q), :] = src[pl.ds(r0, Q), :]
        return carry

    lax.fori_loop(0, nc, copy_body, 0)

    ri = lax.broadcasted_iota(jnp.int32, (SSD_CONV_W * Q, _CONV_WIN), 0)
    ci = lax.broadcasted_iota(jnp.int32, (SSD_CONV_W * Q, _CONV_WIN), 1)
    shift_sel = (ci == (ri % Q) + (ri // Q) + (_CONV_LEAD - _CONV_OFF - pad)).astype(BF16)

    def conv_chunk(pad_ref, w_ref, bias_ref, r0):
        win = pad_ref[pl.ds(r0 + _CONV_OFF, _CONV_WIN), :]
        sh = _dot(shift_sel, win)
        acc = bias_ref[...] + w_ref[0:1, :] * sh[0:Q]
        for k in range(1, SSD_CONV_W):
            acc = acc + w_ref[k:k + 1, :] * sh[k * Q:(k + 1) * Q]
        return _silu(acc)

    def prep_body(c, carry):
        r0 = pl.multiple_of(c * Q, Q)
        xv = conv_chunk(xpad, cwx_ref, cbx_ref, r0)
        xc[pl.ds(r0, Q), :] = xv
        bc[pl.ds(r0, Q), :] = conv_chunk(bpad, cwb_ref, cbb_ref, r0)
        cc[pl.ds(r0, Q), :] = conv_chunk(cpad, cwc_ref, cbc_ref, r0)
        dts[pl.ds(r0, Q), :] = _softplus(ps_ref[pl.ds(r0, Q), :] + dtb_ref[...])
        y_ref[pl.ds(r0, Q), :] = dsk_ref[...] * xv
        return carry

    lax.fori_loop(0, nc, prep_body, 0)

    GW = SSD_GW
    hpg = SSD_HEADS // SSD_GROUPS
    er = lax.broadcasted_iota(jnp.int32, (S_COLS, GW), 0)
    ec = lax.broadcasted_iota(jnp.int32, (S_COLS, GW), 1)
    qi = lax.broadcasted_iota(jnp.int32, (Q, GW), 0)
    qj = lax.broadcasted_iota(jnp.int32, (Q, GW), 1) % Q
    diag_sel = (qi == qj).astype(F32)
    ones_q = jnp.ones((Q, Q), BF16)
    ti = lax.broadcasted_iota(jnp.int32, (Q, Q), 0)
    tj = lax.broadcasted_iota(jnp.int32, (Q, Q), 1)
    lane = lax.broadcasted_iota(jnp.int32, (Q, 128), 1)
    lo_half = lane < SSD_HEAD_DIM
    a_all = -jnp.exp(alog_ref[...])

    per_dir = []
    for d in range(2):
        base = S_DT + d * SSD_HEADS + g * hpg
        esel = (er == base + ec // SSD_HEAD_DIM).astype(BF16)
        a_exp = _sel_dot_r(jnp.broadcast_to(a_all, (8, S_COLS)), esel)[0:1]
        if d == 0:
            tri = (tj <= ti).astype(BF16)
            mask = qj <= qi
        else:
            tri = (tj >= ti).astype(BF16)
            mask = qj >= qi
        per_dir.append((esel, a_exp, tri, mask))

    for d in range(2):
        if use_s0:
            st[d] = s0_ref[d].T
        else:
            st[d] = jnp.zeros((SSD_STATE, GW), F32)

    def chunk_dir(c, d):
        esel, a_exp, tri, mask = per_dir[d]
        r0 = pl.multiple_of(c * Q, Q)
        dt_exp = _sel_dot_r(dts[pl.ds(r0, Q), :], esel)
        cum = _sel_dot_l(tri, dt_exp * a_exp)
        rowb = _sel_dot_l(ones_q, cum * diag_sel)
        lmat = jnp.exp(jnp.where(mask, cum - rowb, -jnp.inf))
        bq = bc[pl.ds(r0, Q), :].astype(BF16)
        cq = cc[pl.ds(r0, Q), :].astype(BF16)
        cb = _dot_nt(cq, jnp.concatenate([bq] * (GW // Q), axis=0))
        amat = (cb * lmat).astype(BF16)
        xq = xc[pl.ds(r0, Q), :] * dt_exp
        parts = []
        for p in range(GW // 128):
            xp = xq[:, p * 128:(p + 1) * 128]
            xbd = jnp.concatenate([jnp.where(lo_half, xp, 0.0), jnp.where(lo_half, 0.0, xp)], axis=0)
            parts.append(_dot(amat[:, p * 128:(p + 1) * 128], xbd.astype(BF16)))
        y_intra = jnp.concatenate(parts, axis=1)
        s_t = st[d]
        y_inter = _dot(cq, s_t.astype(BF16)) * jnp.exp(cum)
        cum_last = cum[Q - 1:Q] if d == 0 else cum[0:1]
        xdec = (xq * jnp.exp(cum_last - cum)).astype(BF16)
        st[d] = s_t * jnp.exp(cum_last) + _dot_tn(bq, xdec)
        y_ref[pl.ds(r0, Q), :] += y_intra + y_inter

    def scan_body(i, carry):
        chunk_dir(i, 0)
        chunk_dir(nc - 1 - i, 1)
        return carry

    lax.fori_loop(0, nc, scan_body, 0, unroll=SSD_UNROLL)

    if emit_state:
        for d in range(2):
            so_ref[d] = st[d].T


def _ssd_branch(pm, ps, conv_w, conv_b, dt_bias, a_log, d_skip, s0, B, L, emit_state):
    use_s0 = s0 is not None
    G, GW = SSD_GROUPS, SSD_GW
    hpg = SSD_HEADS // G
    dtb = jnp.zeros((1, S_COLS), F32).at[0, S_DT:S_DT + 2 * SSD_HEADS].set(dt_bias.reshape(-1))
    alog = jnp.zeros((1, S_COLS), F32).at[0, S_DT:S_DT + 2 * SSD_HEADS].set(a_log.reshape(-1))
    dsk = jnp.repeat(d_skip, SSD_HEAD_DIM).reshape(1, SSD_INNER)
    cb2 = conv_b.reshape(1, SSD_CONV_DIM)
    nb_x = M_X // GW
    in_specs = [
        pl.BlockSpec((L, GW), lambda b, g: (b, nb_x + g)),
        pl.BlockSpec((L, SSD_STATE), lambda b, g: (b, M_B // SSD_STATE + g)),
        pl.BlockSpec((L, SSD_STATE), lambda b, g: (b, M_C // SSD_STATE + g)),
        pl.BlockSpec((L, S_COLS), lambda b, g: (b, 0)),
        pl.BlockSpec((SSD_CONV_W, GW), lambda b, g: (0, g)),
        pl.BlockSpec((SSD_CONV_W, SSD_STATE), lambda b, g: (0, SSD_INNER // SSD_STATE + g)),
        pl.BlockSpec((SSD_CONV_W, SSD_STATE), lambda b, g: (0, (SSD_INNER + SSD_GN) // SSD_STATE + g)),
        pl.BlockSpec((1, GW), lambda b, g: (0, g)),
        pl.BlockSpec((1, SSD_STATE), lambda b, g: (0, SSD_INNER // SSD_STATE + g)),
        pl.BlockSpec((1, SSD_STATE), lambda b, g: (0, (SSD_INNER + SSD_GN) // SSD_STATE + g)),
        pl.BlockSpec((1, S_COLS), lambda b, g: (0, 0)),
        pl.BlockSpec((1, S_COLS), lambda b, g: (0, 0)),
        pl.BlockSpec((1, GW), lambda b, g: (0, g)),
    ]
    args = [pm, pm, pm, ps, conv_w, conv_w, conv_w, cb2, cb2, cb2, dtb, alog, dsk]
    state_spec = pl.BlockSpec((None, 2, None, GW, SSD_STATE), lambda b, g: (b, 0, g, 0, 0))
    if use_s0:
        in_specs.append(state_spec)
        args.append(s0.reshape(B, 2, G, GW, SSD_STATE))
    out_specs = [pl.BlockSpec((L, GW), lambda b, g: (b, g))]
    out_shape = [jax.ShapeDtypeStruct((B * L, SSD_INNER), F32)]
    if emit_state:
        out_specs.append(state_spec)
        out_shape.append(jax.ShapeDtypeStruct((B, 2, G, GW, SSD_STATE), F32))
    plen = L + 2 * _CONV_LEAD
    outs = pl.pallas_call(
        functools.partial(_ssd_kernel, L=L, use_s0=use_s0, emit_state=emit_state),
        grid=(B, G),
        in_specs=in_specs,
        out_specs=out_specs,
        out_shape=out_shape,
        scratch_shapes=[
            pltpu.VMEM((plen, GW), BF16), pltpu.VMEM((plen, SSD_STATE), BF16), pltpu.VMEM((plen, SSD_STATE), BF16),
            pltpu.VMEM((L, GW), F32), pltpu.VMEM((L, SSD_STATE), F32), pltpu.VMEM((L, SSD_STATE), F32),
            pltpu.VMEM((L, S_COLS), F32), pltpu.VMEM((2, SSD_STATE, GW), F32),
        ],
        compiler_params=_cparams(("arbitrary", "arbitrary")),
        name="ssd_scan",
    )(*args)
    y = outs[0]
    state = outs[1].reshape(B, 2, SSD_HEADS, SSD_HEAD_DIM, SSD_STATE) if emit_state else None
    return y, state


def _ctx_attn_kernel(q_ref, k_ref, v_ref, o_ref):
    q = q_ref[...]
    k = k_ref[...]
    v = v_ref[...]
    lane = lax.broadcasted_iota(jnp.int32, q.shape, 1)
    outs = []
    for hh in range(2):
        sel = (lane < NA_HEAD_DIM) if hh == 0 else (lane >= NA_HEAD_DIM)
        qm = jnp.where(sel, q, jnp.zeros_like(q))
        s = _dot_nt(qm, k) * NA_SCALE
        m = jnp.max(s, axis=-1, keepdims=True)
        p = jnp.exp(s - m)
        den = jnp.sum(p, axis=-1, keepdims=True)
        outs.append(_dot(p.astype(BF16), v) / den)
    lane_o = lax.broadcasted_iota(jnp.int32, outs[0].shape, 1)
    o_ref[...] = jnp.where(lane_o < NA_HEAD_DIM, outs[0], outs[1]).astype(o_ref.dtype)


def _context_attention(pm, B, L):
    nq, nk, nv = M_Q // 128, M_K // 128, M_V // 128
    return pl.pallas_call(
        _ctx_attn_kernel,
        grid=(B, NA_HEADS // 2),
        in_specs=[
            pl.BlockSpec((L, 128), lambda b, h: (b, nq + h)),
            pl.BlockSpec((L, 128), lambda b, h: (b, nk + h)),
            pl.BlockSpec((L, 128), lambda b, h: (b, nv + h)),
        ],
        out_specs=pl.BlockSpec((L, 128), lambda b, h: (b, h)),
        out_shape=jax.ShapeDtypeStruct((B * L, NA_INNER), BF16),
        compiler_params=_cparams(("arbitrary", "arbitrary")),
        name="context_attention",
    )(pm, pm, pm)


def _na_bias_table(rpb):
    qc = np.arange(GRID_W)
    cstart = np.clip(qc - NA_WIN_COLS // 2, 0, GRID_W - NA_WIN_COLS)
    kc = np.arange(GRID_W)
    valid = (kc[None, :] >= cstart[:, None]) & (kc[None, :] < cstart[:, None] + NA_WIN_COLS)
    dx = np.clip(kc[None, :] - qc[:, None] + NA_WIN_COLS - 1, 0, 2 * NA_WIN_COLS - 2)
    d0 = np.arange(NA_WIN_ROWS)
    kr = np.arange(NA_WIN_ROWS)
    row_idx = (d0[:, None, None, None] + kr[None, None, :, None]) + np.zeros((1, GRID_W, 1, GRID_W), np.int64)
    col_idx = np.broadcast_to(dx[None, :, None, :], row_idx.shape)
    t = rpb[:, row_idx, col_idx]
    t = jnp.where(jnp.asarray(valid)[None, None, :, None, :], t, NA_NEG)
    return t.reshape(rpb.shape[0], NA_WIN_ROWS, GRID_W, NA_WIN_ROWS * GRID_W).astype(F32)


def _na_kernel(q_ref, k_ref, v_ref, ck_ref, cv_ref, bias_ref, o_ref, ckb, cvb, *, rows):
    ckb[...] = ck_ref[...].astype(BF16)
    cvb[...] = cv_ref[...].astype(BF16)
    wr = NA_WIN_ROWS
    nloc = wr * GRID_W
    lane = lax.broadcasted_iota(jnp.int32, (GRID_W, 128), 1)

    def row_body(r, carry):
        rs = jnp.clip(r - wr // 2, 0, rows - wr)
        d0 = rs - r + wr - 1
        q = q_ref[pl.ds(pl.multiple_of(r * GRID_W, GRID_W), GRID_W), :]
        k0 = pl.multiple_of(rs * GRID_W, GRID_W)
        kw = k_ref[pl.ds(k0, nloc), :]
        vw = v_ref[pl.ds(k0, nloc), :]
        outs = []
        for hh in range(2):
            sel = (lane < NA_HEAD_DIM) if hh == 0 else (lane >= NA_HEAD_DIM)
            qm = jnp.where(sel, q, jnp.zeros_like(q))
            s_loc = _dot_nt(qm, kw) * NA_SCALE + bias_ref[hh, d0]
            s_ctx = _dot_nt(qm, ckb[...]) * NA_SCALE
            m = jnp.maximum(jnp.max(s_loc, axis=-1, keepdims=True), jnp.max(s_ctx, axis=-1, keepdims=True))
            p_loc = jnp.exp(s_loc - m)
            p_ctx = jnp.exp(s_ctx - m)
            den = jnp.sum(p_loc, axis=-1, keepdims=True) + jnp.sum(p_ctx, axis=-1, keepdims=True)
            outs.append((_dot(p_loc.astype(BF16), vw) + _dot(p_ctx.astype(BF16), cvb[...])) / den)
        o = jnp.where(lane < NA_HEAD_DIM, outs[0], outs[1])
        o_ref[pl.ds(pl.multiple_of(r * GRID_W, GRID_W), GRID_W), :] = o.astype(o_ref.dtype)
        return carry

    lax.fori_loop(0, rows, row_body, 0, unroll=NA_UNROLL)


def _neighbourhood_attention(pm, ck, cv, bias_tab, B, S):
    rows = S // GRID_W
    assert rows >= NA_WIN_ROWS
    Lc = ck.shape[1]
    nq, nk, nv = M_Q // 128, M_K // 128, M_V // 128
    return pl.pallas_call(
        functools.partial(_na_kernel, rows=rows),
        grid=(NA_HEADS // 2, B),
        in_specs=[
            pl.BlockSpec((S, 128), lambda h, b: (b, nq + h)),
            pl.BlockSpec((S, 128), lambda h, b: (b, nk + h)),
            pl.BlockSpec((S, 128), lambda h, b: (b, nv + h)),
            pl.BlockSpec((None, Lc, 128), lambda h, b: (b, 0, h)),
            pl.BlockSpec((None, Lc, 128), lambda h, b: (b, 0, h)),
            pl.BlockSpec((2, NA_WIN_ROWS, GRID_W, NA_WIN_ROWS * GRID_W), lambda h, b: (h, 0, 0, 0)),
        ],
        out_specs=pl.BlockSpec((S, 128), lambda h, b: (b, h)),
        out_shape=jax.ShapeDtypeStruct((B * S, NA_INNER), BF16),
        scratch_shapes=[pltpu.VMEM((Lc, 128), BF16), pltpu.VMEM((Lc, 128), BF16)],
        compiler_params=_cparams(("arbitrary", "arbitrary")),
        name="neighbourhood_attention",
    )(pm, pm, pm, ck, cv, bias_tab)


def _gla_kernel(*refs, L, use_s0, emit_state):
    it = iter(refs)
    q_ref, k_ref, v_ref, r_ref, ps_ref = (next(it) for _ in range(5))
    wg0_ref, wg1_ref, bg0_ref, bg1_ref, ng_ref = (next(it) for _ in range(5))
    s0_ref = next(it) if use_s0 else None
    o_ref = next(it)
    so_ref = next(it) if emit_state else None
    gs, acc, st = next(it), next(it), next(it)

    Q = GLA_Q
    nc = L // Q
    RC = 256 if L % 256 == 0 else L
    nrc = L // RC

    def gate_body(c, carry):
        r0 = pl.multiple_of(c * RC, RC)
        lr = ps_ref[pl.ds(r0, RC), :].astype(BF16)
        for d, (wg, bg) in enumerate(((wg0_ref, bg0_ref), (wg1_ref, bg1_ref))):
            gpre = _dot(lr, wg[...].astype(BF16)) + bg[...]
            gs[d, pl.ds(r0, RC), :] = (jnp.minimum(gpre, 0.0) - jnp.log1p(jnp.exp(-jnp.abs(gpre)))) / GLA_TAU
        acc[pl.ds(r0, RC), :] = jnp.zeros((RC, GLA_DV_HEAD), F32)
        return carry

    lax.fori_loop(0, nrc, gate_body, 0)

    ti = lax.broadcasted_iota(jnp.int32, (Q, Q), 0)
    tj = lax.broadcasted_iota(jnp.int32, (Q, Q), 1)
    tris = ((tj <= ti), (tj >= ti))
    for d in range(2):
        if use_s0:
            st[d] = s0_ref[d].T
        else:
            st[d] = jnp.zeros((GLA_DV_HEAD, GLA_DK_HEAD), F32)

    def chunk_dir(c, d):
        r0 = pl.multiple_of(c * Q, Q)
        mask = tris[d]
        bcum = _sel_dot_l(mask.astype(BF16), gs[d, pl.ds(r0, Q), :])
        b_end = bcum[Q - 1:Q] if d == 0 else bcum[0:1]
        qc = q_ref[pl.ds(r0, Q), :].astype(F32) * GLA_QK_SCALE
        kc = k_ref[pl.ds(r0, Q), :].astype(F32)
        vc = v_ref[pl.ds(r0, Q), :]
        qt = (qc * jnp.exp(bcum)).astype(BF16)
        kt = (kc * jnp.exp(-bcum)).astype(BF16)
        kend = (kc * jnp.exp(b_end - bcum)).astype(BF16)
        att = jnp.where(mask, _dot_nt(qt, kt), 0.0)
        s_t = st[d]
        o = _dot(att.astype(BF16), vc) + _dot_nt(qt, s_t.astype(BF16))
        st[d] = s_t * jnp.exp(b_end) + _dot_tn(vc, kend)
        acc[pl.ds(r0, Q), :] += o

    def scan_body(i, carry):
        chunk_dir(i, 0)
        chunk_dir(nc - 1 - i, 1)
        return carry

    lax.fori_loop(0, nc, scan_body, 0, unroll=GLA_UNROLL)

    def fin_body(c, carry):
        r0 = pl.multiple_of(c * RC, RC)
        o = acc[pl.ds(r0, RC), :]
        o = o * lax.rsqrt(jnp.mean(o * o, axis=-1, keepdims=True) + EPS) * ng_ref[...]
        o_ref[pl.ds(r0, RC), :] = (o * _silu(r_ref[pl.ds(r0, RC), :].astype(F32))).astype(o_ref.dtype)
        return carry

    lax.fori_loop(0, nrc, fin_body, 0)

    if emit_state:
        for d in range(2):
            so_ref[d] = st[d].T


def _gla_branch(pm, ps, w_gate, b_gate, norm_g, s0, B, L, emit_state):
    use_s0 = s0 is not None
    H, DK, DV = GLA_HEADS, GLA_DK_HEAD, GLA_DV_HEAD
    wg = jnp.zeros((S_COLS, 2 * GLA_DK), F32)
    for d in range(2):
        wg = wg.at[S_LR + d * GLA_LOWRANK:S_LR + (d + 1) * GLA_LOWRANK, d * GLA_DK:(d + 1) * GLA_DK].set(w_gate[d])
    bg = b_gate.reshape(1, 2 * GLA_DK)
    in_specs = [
        pl.BlockSpec((L, DK), lambda b, h: (b, M_GQ // DK + h)),
        pl.BlockSpec((L, DK), lambda b, h: (b, M_GK // DK + h)),
        pl.BlockSpec((L, DV), lambda b, h: (b, M_GV // DV + h)),
        pl.BlockSpec((L, DV), lambda b, h: (b, M_R // DV + h)),
        pl.BlockSpec((L, S_COLS), lambda b, h: (b, 0)),
        pl.BlockSpec((S_COLS, DK), lambda b, h: (0, h)),
        pl.BlockSpec((S_COLS, DK), lambda b, h: (0, H + h)),
        pl.BlockSpec((1, DK), lambda b, h: (0, h)),
        pl.BlockSpec((1, DK), lambda b, h: (0, H + h)),
        pl.BlockSpec((1, DV), lambda b, h: (0, 0)),
    ]
    args = [pm, pm, pm, pm, ps, wg, wg, bg, bg, norm_g.reshape(1, DV)]
    state_spec = pl.BlockSpec((None, 2, None, DK, DV), lambda b, h: (b, 0, h, 0, 0))
    if use_s0:
        in_specs.append(state_spec)
        args.append(s0)
    out_specs = [pl.BlockSpec((L, DV), lambda b, h: (b, h))]
    out_shape = [jax.ShapeDtypeStruct((B * L, GLA_DV), BF16)]
    if emit_state:
        out_specs.append(state_spec)
        out_shape.append(jax.ShapeDtypeStruct((B, 2, H, DK, DV), F32))
    outs = pl.pallas_call(
        functools.partial(_gla_kernel, L=L, use_s0=use_s0, emit_state=emit_state),
        grid=(B, H),
        in_specs=in_specs,
        out_specs=out_specs,
        out_shape=out_shape,
        scratch_shapes=[pltpu.VMEM((2, L, DK), F32), pltpu.VMEM((L, DV), F32), pltpu.VMEM((2, DV, DK), F32)],
        compiler_params=_cparams(("arbitrary", "arbitrary")),
        name="gla_scan",
    )(*args)
    return outs[0], (outs[1] if emit_state else None)


def _merge_kernel(*refs, aliased):
    (x_ref, yssd_ref, z_ref, ona_ref, ogla_ref, g0_ref, g1_ref, g2_ref,
     gate1_ref, shift2_ref, scale2_ref, sng_ref, n2g_ref, wb_ref, wo_ref, rw_ref, rb_ref) = refs[:17]
    xo_ref, h2_ref, lg_ref = refs[-3:]
    y = yssd_ref[...] * _silu(z_ref[...].astype(F32))
    y = y * lax.rsqrt(jnp.mean(y * y, axis=-1, keepdims=True) + EPS) * sng_ref[...]
    m = jax.nn.sigmoid(g0_ref[...].astype(F32)) * _dot(y.astype(BF16), wb_ref[0])
    m = m + jax.nn.sigmoid(g1_ref[...].astype(F32)) * _dot(ona_ref[...], wb_ref[1])
    m = m + jax.nn.sigmoid(g2_ref[...].astype(F32)) * _dot(ogla_ref[...], wb_ref[2])
    x = x_ref[...] + gate1_ref[...] * _dot(m.astype(BF16), wo_ref[...])
    xo_ref[...] = x
    h2 = x * lax.rsqrt(jnp.mean(x * x, axis=-1, keepdims=True) + EPS) * n2g_ref[...]
    h2 = h2 * (1.0 + scale2_ref[...]) + shift2_ref[...]
    h2_ref[...] = h2
    lg_ref[...] = jnp.dot(h2, rw_ref[...], preferred_element_type=F32,
                          precision=lax.Precision.HIGHEST) + rb_ref[...]


def _merge(x2, y_ssd, pm, o_na, o_gla, mod_l, ssd_norm_g, norm2_g, wb_bf, wo_bf, rw_pad, rb_pad,
           rows_per_mod, mod_row0, t_all, row0, shared):
    T = x2.shape[0]
    tm = min(256, rows_per_mod)
    per = rows_per_mod // tm
    D = D_MODEL
    ng = M_GATES // D
    blk0 = row0 // tm

    def mod_map(col):
        return lambda i: (mod_row0 + i // per, 0, col)

    row = lambda i: (i, 0)
    row_off = lambda i: (blk0 + i, 0)
    const2 = lambda i: (0, 0)
    in_specs = [
        pl.BlockSpec((tm, D), row),
        pl.BlockSpec((tm, D), row),
        pl.BlockSpec((tm, D), lambda i: (i, M_Z // D)),
        pl.BlockSpec((tm, D), row),
        pl.BlockSpec((tm, D), row),
        pl.BlockSpec((tm, D), lambda i: (i, ng)),
        pl.BlockSpec((tm, D), lambda i: (i, ng + 1)),
        pl.BlockSpec((tm, D), lambda i: (i, ng + 2)),
        pl.BlockSpec((None, 1, D), mod_map(2)),
        pl.BlockSpec((None, 1, D), mod_map(3)),
        pl.BlockSpec((None, 1, D), mod_map(4)),
        pl.BlockSpec((1, D), const2),
        pl.BlockSpec((1, D), const2),
        pl.BlockSpec((N_BRANCH, D, D), lambda i: (0, 0, 0)),
        pl.BlockSpec((D, D), const2),
        pl.BlockSpec((D, 128), const2),
        pl.BlockSpec((1, 128), const2),
    ]
    args = [x2, y_ssd, pm, o_na, o_gla, pm, pm, pm, mod_l, mod_l, mod_l,
            ssd_norm_g.reshape(1, D), norm2_g.reshape(1, D), wb_bf, wo_bf, rw_pad, rb_pad]
    aliases = {}
    if shared is not None:
        in_specs += [pl.BlockSpec(memory_space=pl.ANY), pl.BlockSpec(memory_space=pl.ANY)]
        aliases = {len(args): 1, len(args) + 1: 2}
        args += list(shared)
    return pl.pallas_call(
        functools.partial(_merge_kernel, aliased=shared is not None),
        grid=(T // tm,),
        in_specs=in_specs,
        out_specs=[pl.BlockSpec((tm, D), row), pl.BlockSpec((tm, D), row_off), pl.BlockSpec((tm, 128), row_off)],
        out_shape=[jax.ShapeDtypeStruct((T, D), F32), jax.ShapeDtypeStruct((t_all, D), F32),
                   jax.ShapeDtypeStruct((t_all, 128), F32)],
        input_output_aliases=aliases,
        compiler_params=_cparams(("arbitrary",)),
        name="branch_merge",
    )(*args)


def _moe_kernel(be_ref, nused_ref, tok_ref, tokn_ref, dst_ref, h2_hbm,
                wgu_ref, bgu_ref, wdn_ref, bdn_ref, y_hbm, xbuf, obuf, wgu_bf, wdn_bf, gsem, ssem):
    b = pl.program_id(0)
    nused = nused_ref[0]
    slot = b % 2

    def start_gather(idx_ref, s):
        def body(i, carry):
            pltpu.make_async_copy(h2_hbm.at[pl.ds(idx_ref[0, i], 1), :], xbuf.at[s, pl.ds(i, 1), :],
                                  gsem.at[s]).start()
            return carry
        lax.fori_loop(0, MOE_ROWS, body, 0, unroll=MOE_DMA_UNROLL)

    def wait_gather(s):
        pltpu.make_async_copy(h2_hbm.at[pl.ds(0, MOE_ROWS), :], xbuf.at[s], gsem.at[s]).wait()

    def wait_scatter():
        pltpu.make_async_copy(obuf, y_hbm.at[pl.ds(0, MOE_ROWS), :], ssem.at[0]).wait()

    @pl.when(b == 0)
    def _():
        start_gather(tok_ref, 0)

    @pl.when(b < nused)
    def _():
        wait_gather(slot)

        @pl.when(b + 1 < nused)
        def _():
            start_gather(tokn_ref, 1 - slot)

        prev = be_ref[jnp.maximum(b - 1, 0)]

        @pl.when(jnp.logical_or(b == 0, be_ref[b] != prev))
        def _():
            rc = 128

            def cast_body(i, carry):
                r0 = pl.multiple_of(i * rc, rc)
                wgu_bf[pl.ds(r0, rc), :] = wgu_ref[pl.ds(r0, rc), :].astype(BF16)
                wdn_bf[pl.ds(r0, rc), :] = wdn_ref[pl.ds(r0, rc), :].astype(BF16)
                return carry

            lax.fori_loop(0, D_MODEL // rc, cast_body, 0)

        x = xbuf[slot].astype(BF16)
        acc = jnp.zeros((MOE_ROWS, D_MODEL), F32) + bdn_ref[...]
        for f in range(D_FF // MOE_FF_TILE):
            c0 = f * MOE_FF_TILE
            glu = _dot(x, wgu_bf[:, c0:c0 + MOE_FF_TILE]) + bgu_ref[:, c0:c0 + MOE_FF_TILE]
            lin = _dot(x, wgu_bf[:, D_FF + c0:D_FF + c0 + MOE_FF_TILE]) + bgu_ref[:, D_FF + c0:D_FF + c0 + MOE_FF_TILE]
            glu = jnp.minimum(glu, SWIGLU_LIMIT)
            lin = jnp.clip(lin, -SWIGLU_LIMIT, SWIGLU_LIMIT)
            act = glu * jax.nn.sigmoid(SWIGLU_ALPHA * glu) * (lin + 1.0)
            acc = acc + _dot(act.astype(BF16), wdn_bf[c0:c0 + MOE_FF_TILE, :])

        @pl.when(b > 0)
        def _():
            wait_scatter()

        obuf[...] = acc

        def sc_body(i, carry):
            pltpu.make_async_copy(obuf.at[pl.ds(i, 1), :], y_hbm.at[pl.ds(dst_ref[0, i], 1), :], ssem.at[0]).start()
            return carry

        lax.fori_loop(0, MOE_ROWS, sc_body, 0, unroll=MOE_DMA_UNROLL)

        @pl.when(b == nused - 1)
        def _():
            wait_scatter()


def _moe_experts(h2_all, slot_tok, slot_dst, block_e, n_used, w_gu, b_gu, w_dn, b_dn, layer):
    n_blocks = slot_tok.shape[0]
    T, D = h2_all.shape
    smem_blk = lambda f: pl.BlockSpec((None, 1, MOE_ROWS), f, memory_space=pltpu.SMEM)
    grid_spec = pltpu.PrefetchScalarGridSpec(
        num_scalar_prefetch=2,
        grid=(n_blocks,),
        in_specs=[
            smem_blk(lambda b, be, nu: (b, 0, 0)),
            smem_blk(lambda b, be, nu: (jnp.minimum(b + 1, n_blocks - 1), 0, 0)),
            smem_blk(lambda b, be, nu: (b, 0, 0)),
            pl.BlockSpec(memory_space=pl.ANY),
            pl.BlockSpec((None, None, D, 2 * D_FF), lambda b, be, nu: (layer, be[b], 0, 0)),
            pl.BlockSpec((None, None, 1, 2 * D_FF), lambda b, be, nu: (layer, be[b], 0, 0)),
            pl.BlockSpec((None, None, D_FF, D), lambda b, be, nu: (layer, be[b], 0, 0)),
            pl.BlockSpec((None, None, 1, D), lambda b, be, nu: (layer, be[b], 0, 0)),
        ],
        out_specs=pl.BlockSpec(memory_space=pl.ANY),
        scratch_shapes=[pltpu.VMEM((2, MOE_ROWS, D), F32), pltpu.VMEM((MOE_ROWS, D), F32),
                        pltpu.VMEM((D, 2 * D_FF), BF16), pltpu.VMEM((D_FF, D), BF16),
                        pltpu.SemaphoreType.DMA((2,)), pltpu.SemaphoreType.DMA((1,))],
    )
    return pl.pallas_call(
        _moe_kernel,
        grid_spec=grid_spec,
        out_shape=jax.ShapeDtypeStruct((TOP_K * T + MOE_ROWS, D), F32),
        compiler_params=_cparams(("arbitrary",)),
        name="moe_experts",
    )(block_e, n_used, slot_tok, slot_tok, slot_dst, h2_all, w_gu,
      b_gu.reshape(DEPTH, N_EXPERTS, 1, 2 * D_FF), w_dn, b_dn.reshape(DEPTH, N_EXPERTS, 1, D))


def _moe_route(logits):
    T = logits.shape[0]
    TK = T * TOP_K
    top_logit, top_idx = lax.top_k(logits, TOP_K)
    top_w = jax.nn.softmax(top_logit, axis=-1)
    flat_e = top_idx.reshape(TK).astype(jnp.int32)
    onehot = (flat_e[:, None] == jnp.arange(N_EXPERTS, dtype=jnp.int32)[None, :]).astype(jnp.int32)
    csum = jnp.cumsum(onehot, axis=0)
    rank = jnp.sum(onehot * csum, axis=1) - 1
    counts = csum[-1]
    padded = (counts + MOE_ROWS - 1) // MOE_ROWS * MOE_ROWS
    pad_end = jnp.cumsum(padded)
    pad_start = pad_end - padded
    dest = pad_start[flat_e] + rank
    n_blocks = -(-(TK + N_EXPERTS * (MOE_ROWS - 1)) // MOE_ROWS)
    n_slots = n_blocks * MOE_ROWS
    flat = jnp.arange(TK, dtype=jnp.int32)
    slot_tok = jnp.zeros((n_slots,), jnp.int32).at[dest].set(flat // TOP_K)
    spare = TK + jnp.arange(n_slots, dtype=jnp.int32) % MOE_ROWS
    slot_dst = spare.at[dest].set((flat % TOP_K) * T + flat // TOP_K)
    blk_start = jnp.arange(n_blocks, dtype=jnp.int32) * MOE_ROWS
    block_e = jnp.minimum(jnp.searchsorted(pad_end, blk_start, side="right"), N_EXPERTS - 1).astype(jnp.int32)
    n_used = (pad_end[-1] // MOE_ROWS).astype(jnp.int32).reshape(1)
    top_w_pad = jnp.pad(top_w, ((0, 0), (0, 128 - TOP_K)))
    return (top_w_pad, slot_tok.reshape(n_blocks, 1, MOE_ROWS), slot_dst.reshape(n_blocks, 1, MOE_ROWS),
            block_e, n_used)


def _combine_kernel(x_ref, y0_ref, y1_ref, y2_ref, y3_ref, w_ref, gate_ref, o_ref):
    w = w_ref[...]
    y = w[:, 0:1] * y0_ref[...]
    for k, y_ref in enumerate((y1_ref, y2_ref, y3_ref), start=1):
        y = y + w[:, k:k + 1] * y_ref[...]
    o_ref[...] = x_ref[...] + gate_ref[...] * y


def _moe_combine(x2, y_rows, top_w_pad, mod_l, rows_per_mod, mod_row0, row0, t_all):
    assert TOP_K == 4
    T, D = x2.shape
    tm = min(512, rows_per_mod)
    per = rows_per_mod // tm
    blk0 = row0 // tm
    nblk = t_all // tm

    def y_map(k):
        return lambda i: (k * nblk + blk0 + i, 0)

    return pl.pallas_call(
        _combine_kernel,
        grid=(T // tm,),
        in_specs=[pl.BlockSpec((tm, D), lambda i: (i, 0))]
        + [pl.BlockSpec((tm, D), y_map(k)) for k in range(TOP_K)]
        + [pl.BlockSpec((tm, 128), lambda i: (blk0 + i, 0)),
           pl.BlockSpec((None, 1, D), lambda i: (mod_row0 + i // per, 0, 5))],
        out_specs=pl.BlockSpec((tm, D), lambda i: (i, 0)),
        out_shape=jax.ShapeDtypeStruct((T, D), F32),
        compiler_params=_cparams(("arbitrary",)),
        name="moe_combine",
    )(x2, y_rows, y_rows, y_rows, y_rows, top_w_pad, mod_l)


def _final_norm_kernel(x_ref, g_ref, o_ref):
    x = x_ref[...]
    o_ref[...] = x * lax.rsqrt(jnp.mean(x * x, axis=-1, keepdims=True) + EPS) * g_ref[...]


def _final_norm(x2, g):
    T = x2.shape[0]
    tm = 512
    return pl.pallas_call(
        _final_norm_kernel,
        grid=(T // tm,),
        in_specs=[pl.BlockSpec((tm, D_MODEL), lambda i: (i, 0)), pl.BlockSpec((1, D_MODEL), lambda i: (0, 0))],
        out_specs=pl.BlockSpec((tm, D_MODEL), lambda i: (i, 0)),
        out_shape=jax.ShapeDtypeStruct((T, D_MODEL), F32),
        compiler_params=_cparams(("arbitrary",)),
        name="final_norm",
    )(x2, g.reshape(1, D_MODEL))


def _mixer_half(x2, B, L, mod_l, mod_row0, lw, ctx, t_all, row0, shared):
    latent = ctx is not None
    rows_per_mod = L if latent else B * L
    pm, ps = _in_projection(x2, mod_l, lw["norm1_g"], lw["w_in"], rows_per_mod, mod_row0)
    if latent:
        ck, cv, s_ssd0, s_gla0 = ctx
    else:
        s_ssd0 = s_gla0 = None
    y_ssd, s_ssd = _ssd_branch(pm, ps, lw["ssd_conv_w"], lw["ssd_conv_b"], lw["ssd_dt_bias"], lw["ssd_a_log"],
                               lw["ssd_d"], s_ssd0, B, L, emit_state=not latent)
    if latent:
        o_na = _neighbourhood_attention(pm, ck, cv, lw["na_bias"], B, L)
    else:
        o_na = _context_attention(pm, B, L)
    o_gla, s_gla = _gla_branch(pm, ps, lw["gla_w_gate"], lw["gla_b_gate"], lw["gla_norm_g"], s_gla0, B, L,
                               emit_state=not latent)
    x_new, h2, logits = _merge(x2, y_ssd, pm, o_na, o_gla, mod_l, lw["ssd_norm_g"], lw["norm2_g"],
                               lw["w_branch"], lw["w_out"], lw["router_w"], lw["router_b"], rows_per_mod, mod_row0,
                               t_all, row0, shared)
    return x_new, h2, logits, pm, s_ssd, s_gla


def kernel(x_prompt, x_sample, cache_na_k, cache_na_v, state_ssd, state_gla, c, c_ctx, w_ada, b_ada, norm1_g, norm2_g, w_in, ssd_conv_w, ssd_conv_b, ssd_dt_bias, ssd_a_log, ssd_d, ssd_norm_g, na_rpb, gla_w_gate, gla_b_gate, gla_norm_g, w_branch, w_out, router_w, router_b, moe_w_gu, moe_b_gu, moe_w_dn, moe_b_dn, final_norm_g):
    Bp, Lp, D = x_prompt.shape
    Bs, Ls, _ = x_sample.shape
    Tp, Ts = Bp * Lp, Bs * Ls
    Lc = cache_na_k.shape[2]

    cvecs = jnp.concatenate([c_ctx[None], c, jnp.zeros((8 - 1 - Bs, D), F32)], axis=0)
    mod = _modulation(cvecs, w_ada, b_ada).reshape(DEPTH, 8, 1, 6 * D)

    xp = x_prompt.reshape(Tp, D)
    xs = x_sample.reshape(Ts, D)
    ks_, vs_, sss_, sgs_ = [], [], [], []
    for l in range(DEPTH):
        lw = {
            "norm1_g": norm1_g[l], "norm2_g": norm2_g[l], "w_in": _relayout_w_in(w_in[l]),
            "ssd_conv_w": ssd_conv_w[l], "ssd_conv_b": ssd_conv_b[l], "ssd_dt_bias": ssd_dt_bias[l],
            "ssd_a_log": ssd_a_log[l], "ssd_d": ssd_d[l], "ssd_norm_g": ssd_norm_g[l],
            "na_bias": _na_bias_table(na_rpb[l]),
            "gla_w_gate": gla_w_gate[l], "gla_b_gate": gla_b_gate[l], "gla_norm_g": gla_norm_g[l],
            "w_branch": w_branch[l].astype(BF16), "w_out": w_out[l].astype(BF16),
            "router_w": jnp.pad(router_w[l], ((0, 0), (0, 128 - N_EXPERTS))),
            "router_b": jnp.pad(router_b[l], (0, 128 - N_EXPERTS)).reshape(1, 128),
        }
        mod_l = mod[l]
        t_all = Tp + Ts
        xp, h2a, lga, pmp, s_ssd, s_gla = _mixer_half(xp, Bp, Lp, mod_l, 0, lw, None, t_all, 0, None)
        ctx = (cache_na_k[:, l].reshape(Bs, Lc, NA_INNER), cache_na_v[:, l].reshape(Bs, Lc, NA_INNER),
               state_ssd[:, l], state_gla[:, l])
        xs, h2a, lga, _, _, _ = _mixer_half(xs, Bs, Ls, mod_l, 1, lw, ctx, t_all, Tp, (h2a, lga))
        ks_.append(pmp[:, M_K:M_K + NA_INNER].astype(F32).reshape(Bp, Lp, NA_HEADS, NA_HEAD_DIM))
        vs_.append(pmp[:, M_V:M_V + NA_INNER].astype(F32).reshape(Bp, Lp, NA_HEADS, NA_HEAD_DIM))
        sss_.append(s_ssd)
        sgs_.append(s_gla)

        top_w, slot_tok, slot_dst, block_e, n_used = _moe_route(lga[:, :N_EXPERTS])
        y_rows = _moe_experts(h2a, slot_tok, slot_dst, block_e, n_used, moe_w_gu, moe_b_gu, moe_w_dn, moe_b_dn, l)
        xp = _moe_combine(xp, y_rows, top_w, mod_l, Tp, 0, 0, t_all)
        xs = _moe_combine(xs, y_rows, top_w, mod_l, Ls, 1, Tp, t_all)

    y_prompt = _final_norm(xp, final_norm_g).reshape(Bp, Lp, D)
    y_sample = _final_norm(xs, final_norm_g).reshape(Bs, Ls, D)
    return (y_prompt, y_sample, jnp.stack(ks_, axis=1), jnp.stack(vs_, axis=1),
            jnp.stack(sss_, axis=1), jnp.stack(sgs_, axis=1))
```

```python
import functools

import numpy as np
import jax
import jax.numpy as jnp
from jax import lax
from jax.experimental import pallas as pl
from jax.experimental.pallas import tpu as pltpu

F32 = jnp.float32
BF16 = jnp.bfloat16

D_MODEL = 1024
DEPTH = 4
GRID_W = 64
EPS = 1e-6

SSD_HEADS = 16
SSD_HEAD_DIM = 64
SSD_INNER = SSD_HEADS * SSD_HEAD_DIM
SSD_GROUPS = 2
SSD_STATE = 128
SSD_GN = SSD_GROUPS * SSD_STATE
SSD_CONV_W = 5
SSD_CONV_DIM = SSD_INNER + 2 * SSD_GN
SSD_Q = 64
SSD_GW = SSD_INNER // SSD_GROUPS

NA_HEADS = 16
NA_HEAD_DIM = 64
NA_INNER = NA_HEADS * NA_HEAD_DIM
NA_WIN_ROWS = 8
NA_WIN_COLS = 16
NA_SCALE = NA_HEAD_DIM ** -0.5
NA_NEG = -1e30

GLA_HEADS = 4
GLA_DK = D_MODEL // 2
GLA_DV = D_MODEL
GLA_DK_HEAD = GLA_DK // GLA_HEADS
GLA_DV_HEAD = GLA_DV // GLA_HEADS
GLA_LOWRANK = 16
GLA_TAU = 16.0
GLA_Q = 32
GLA_QK_SCALE = GLA_DK_HEAD ** -0.5

N_BRANCH = 3
N_EXPERTS = 32
TOP_K = 4
D_FF = D_MODEL
SWIGLU_ALPHA = 1.702
SWIGLU_LIMIT = 7.0
MOE_ROWS = 512
MOE_FF_TILE = 512
MOE_DMA_UNROLL = 8

SSD_UNROLL = 2
GLA_UNROLL = 4
NA_UNROLL = 4

OFF_SSD_XBC = SSD_INNER
OFF_SSD_DT = OFF_SSD_XBC + SSD_CONV_DIM
OFF_NA_QKV = OFF_SSD_DT + 2 * SSD_HEADS
OFF_GLA_Q = OFF_NA_QKV + 3 * NA_INNER
OFF_GLA_LR = OFF_GLA_Q + 2 * GLA_DK + 2 * GLA_DV
OFF_GATES = OFF_GLA_LR + 2 * GLA_LOWRANK
IN_COLS = OFF_GATES + N_BRANCH * D_MODEL

M_Z = 0
M_X = M_Z + SSD_INNER
M_Q = M_X + SSD_INNER
M_K = M_Q + NA_INNER
M_V = M_K + NA_INNER
M_GV = M_V + NA_INNER
M_R = M_GV + GLA_DV
M_GATES = M_R + GLA_DV
M_GQ = M_GATES + N_BRANCH * D_MODEL
M_GK = M_GQ + GLA_DK
M_B = M_GK + GLA_DK
M_C = M_B + SSD_GN
M_COLS = M_C + SSD_GN
_MAIN_SEGMENTS = (
    (0, 2 * SSD_INNER),
    (OFF_NA_QKV, 3 * NA_INNER),
    (OFF_GLA_Q + 2 * GLA_DK, 2 * GLA_DV),
    (OFF_GATES, N_BRANCH * D_MODEL),
    (OFF_GLA_Q, 2 * GLA_DK),
    (OFF_SSD_XBC + SSD_INNER, 2 * SSD_GN),
)
_SMALL_SEGMENTS = ((OFF_SSD_DT, 2 * SSD_HEADS), (OFF_GLA_LR, 2 * GLA_LOWRANK))
S_DT = 0
S_LR = 2 * SSD_HEADS
S_COLS = 128
PROJ_TN = 512
PROJ_COLS = M_COLS + PROJ_TN
PROJ_TM = 2048

VMEM_LIMIT = 56 * 1024 * 1024


def _cparams(sem):
    return pltpu.CompilerParams(dimension_semantics=sem, vmem_limit_bytes=VMEM_LIMIT)


def _silu(x):
    return x * jax.nn.sigmoid(x)


def _softplus(x):
    return jnp.maximum(x, 0.0) + jnp.log1p(jnp.exp(-jnp.abs(x)))


def _split3(x):
    hi = x.astype(BF16)
    r1 = x - hi.astype(F32)
    mid = r1.astype(BF16)
    lo = (r1 - mid.astype(F32)).astype(BF16)
    return hi, mid, lo


def _dot(a, b):
    return jnp.dot(a, b, preferred_element_type=F32)


def _dot_nt(a, b):
    return lax.dot_general(a, b, (((1,), (1,)), ((), ())), preferred_element_type=F32)


def _dot_tn(a, b):
    return lax.dot_general(a, b, (((0,), (0,)), ((), ())), preferred_element_type=F32)


def _sel_dot_l(sel_bf, x):
    hi, mid, lo = _split3(x)
    return (_dot(sel_bf, lo) + _dot(sel_bf, mid)) + _dot(sel_bf, hi)


def _sel_dot_r(x, sel_bf):
    hi, mid, lo = _split3(x)
    return (_dot(lo, sel_bf) + _dot(mid, sel_bf)) + _dot(hi, sel_bf)


def _mod_kernel(c_ref, w_ref, b_ref, o_ref):
    c = c_ref[...]
    o_ref[...] = jnp.dot(_silu(c), w_ref[...], preferred_element_type=F32,
                         precision=lax.Precision.HIGHEST) + b_ref[...]


def _modulation(cvecs, w_ada, b_ada):
    nrow = cvecs.shape[0]
    return pl.pallas_call(
        _mod_kernel,
        grid=(DEPTH, 6),
        in_specs=[
            pl.BlockSpec((nrow, D_MODEL), lambda l, j: (0, 0)),
            pl.BlockSpec((None, D_MODEL, D_MODEL), lambda l, j: (l, 0, j)),
            pl.BlockSpec((None, 1, D_MODEL), lambda l, j: (l, 0, j)),
        ],
        out_specs=pl.BlockSpec((None, nrow, D_MODEL), lambda l, j: (l, 0, j)),
        out_shape=jax.ShapeDtypeStruct((DEPTH, nrow, 6 * D_MODEL), F32),
        compiler_params=_cparams(("arbitrary", "arbitrary")),
        name="adaln_mod",
    )(cvecs, w_ada, b_ada.reshape(DEPTH, 1, 6 * D_MODEL))


def _inproj_kernel(x_ref, g_ref, shift_ref, scale_ref, w_ref, om_ref, os_ref, h_ref, *, n_main):
    j = pl.program_id(1)

    @pl.when(j == 0)
    def _():
        x = x_ref[...]
        h = x * lax.rsqrt(jnp.mean(x * x, axis=-1, keepdims=True) + EPS) * g_ref[...]
        h = h * (1.0 + scale_ref[...]) + shift_ref[...]
        h_ref[...] = h.astype(BF16)

    acc = _dot(h_ref[...], w_ref[...])

    @pl.when(j < n_main)
    def _():
        om_ref[...] = acc.astype(BF16)

    @pl.when(j == n_main)
    def _():
        os_ref[...] = acc[:, :S_COLS]


def _relayout_w_in(w):
    parts = [w[:, s:s + n] for s, n in _MAIN_SEGMENTS + _SMALL_SEGMENTS]
    parts.append(jnp.zeros((w.shape[0], PROJ_TN - 2 * SSD_HEADS - 2 * GLA_LOWRANK), w.dtype))
    return jnp.concatenate(parts, axis=1).astype(BF16)


def _in_projection(x2, mod_l, norm_g, w_perm, rows_per_mod, mod_row0):
    T = x2.shape[0]
    tm = min(PROJ_TM, rows_per_mod)
    n_main = M_COLS // PROJ_TN
    per = rows_per_mod // tm

    def mod_map(col):
        return lambda i, j: (mod_row0 + i // per, 0, col)

    return pl.pallas_call(
        functools.partial(_inproj_kernel, n_main=n_main),
        grid=(T // tm, n_main + 1),
        in_specs=[
            pl.BlockSpec((tm, D_MODEL), lambda i, j: (i, 0)),
            pl.BlockSpec((1, D_MODEL), lambda i, j: (0, 0)),
            pl.BlockSpec((None, 1, D_MODEL), mod_map(0)),
            pl.BlockSpec((None, 1, D_MODEL), mod_map(1)),
            pl.BlockSpec((D_MODEL, PROJ_TN), lambda i, j: (0, j)),
        ],
        out_specs=[
            pl.BlockSpec((tm, PROJ_TN), lambda i, j: (i, jnp.minimum(j, n_main - 1))),
            pl.BlockSpec((tm, S_COLS), lambda i, j: (i, 0)),
        ],
        out_shape=[jax.ShapeDtypeStruct((T, M_COLS), BF16), jax.ShapeDtypeStruct((T, S_COLS), F32)],
        scratch_shapes=[pltpu.VMEM((tm, D_MODEL), BF16)],
        compiler_params=_cparams(("arbitrary", "arbitrary")),
        name="in_projection",
    )(x2, norm_g.reshape(1, D_MODEL), mod_l, mod_l, w_perm)


_CONV_WIN = 128
_CONV_LEAD = 64
_CONV_OFF = 48


def _ssd_kernel(*refs, L, use_s0, emit_state):
    it = iter(refs)
    x_ref, b_ref, c_ref, ps_ref = next(it), next(it), next(it), next(it)
    cwx_ref, cwb_ref, cwc_ref = next(it), next(it), next(it)
    cbx_ref, cbb_ref, cbc_ref = next(it), next(it), next(it)
    dtb_ref, alog_ref, dsk_ref = next(it), next(it), next(it)
    s0_ref = next(it) if use_s0 else None
    y_ref = next(it)
    so_ref = next(it) if emit_state else None
    xpad, bpad, cpad, xc, bc, cc, dts, st = (next(it) for _ in range(8))

    Q = SSD_Q
    nc = L // Q
    g = pl.program_id(1)
    pad = SSD_CONV_W // 2

    for src, dst in ((x_ref, xpad), (b_ref, bpad), (c_ref, cpad)):
        w = dst.shape[1]
        dst[0:_CONV_LEAD, :] = jnp.zeros((_CONV_LEAD, w), BF16)
        dst[_CONV_LEAD + L:_CONV_LEAD + L + _CONV_LEAD, :] = jnp.zeros((_CONV_LEAD, w), BF16)

    def copy_body(c, carry):
        r0 = pl.multiple_of(c * Q, Q)
        for src, dst in ((x_ref, xpad), (b_ref, bpad), (c_ref, cpad)):
            dst[pl.ds(r0 + _CONV_LEAD, Q), :] = src[pl.ds(r0, Q), :]
        return carry

    lax.fori_loop(0, nc, copy_body, 0)

    ri = lax.broadcasted_iota(jnp.int32, (SSD_CONV_W * Q, _CONV_WIN), 0)
    ci = lax.broadcasted_iota(jnp.int32, (SSD_CONV_W * Q, _CONV_WIN), 1)
    shift_sel = (ci == (ri % Q) + (ri // Q) + (_CONV_LEAD - _CONV_OFF - pad)).astype(BF16)

    def conv_chunk(pad_ref, w_ref, bias_ref, r0):
        win = pad_ref[pl.ds(r0 + _CONV_OFF, _CONV_WIN), :]
        sh = _dot(shift_sel, win)
        acc = bias_ref[...] + w_ref[0:1, :] * sh[0:Q]
        for k in range(1, SSD_CONV_W):
            acc = acc + w_ref[k:k + 1, :] * sh[k * Q:(k + 1) * Q]
        return _silu(acc)

    def prep_body(c, carry):
        r0 = pl.multiple_of(c * Q, Q)
        xv = conv_chunk(xpad, cwx_ref, cbx_ref, r0)
        xc[pl.ds(r0, Q), :] = xv
        bc[pl.ds(r0, Q), :] = conv_chunk(bpad, cwb_ref, cbb_ref, r0)
        cc[pl.ds(r0, Q), :] = conv_chunk(cpad, cwc_ref, cbc_ref, r0)
        dts[pl.ds(r0, Q), :] = _softplus(ps_ref[pl.ds(r0, Q), :] + dtb_ref[...])
        y_ref[pl.ds(r0, Q), :] = dsk_ref[...] * xv
        return carry

    lax.fori_loop(0, nc, prep_body, 0)

    GW = SSD_GW
    hpg = SSD_HEADS // SSD_GROUPS
    er = lax.broadcasted_iota(jnp.int32, (S_COLS, GW), 0)
    ec = lax.broadcasted_iota(jnp.int32, (S_COLS, GW), 1)
    qi = lax.broadcasted_iota(jnp.int32, (Q, GW), 0)
    qj = lax.broadcasted_iota(jnp.int32, (Q, GW), 1) % Q
    diag_sel = (qi == qj).astype(F32)
    ones_q = jnp.ones((Q, Q), BF16)
    ti = lax.broadcasted_iota(jnp.int32, (Q, Q), 0)
    tj = lax.broadcasted_iota(jnp.int32, (Q, Q), 1)
    lane = lax.broadcasted_iota(jnp.int32, (Q, 128), 1)
    lo_half = lane < SSD_HEAD_DIM
    a_all = -jnp.exp(alog_ref[...])

    per_dir = []
    for d in range(2):
        base = S_DT + d * SSD_HEADS + g * hpg
        esel = (er == base + ec // SSD_HEAD_DIM).astype(BF16)
        a_exp = _sel_dot_r(jnp.broadcast_to(a_all, (8, S_COLS)), esel)[0:1]
        if d == 0:
            tri = (tj <= ti).astype(BF16)
            mask = qj <= qi
        else:
            tri = (tj >= ti).astype(BF16)
            mask = qj >= qi
        per_dir.append((esel, a_exp, tri, mask))

    for d in range(2):
        if use_s0:
            st[d] = s0_ref[d].T
        else:
            st[d] = jnp.zeros((SSD_STATE, GW), F32)

    def chunk_dir(c, d):
        esel, a_exp, tri, mask = per_dir[d]
        r0 = pl.multiple_of(c * Q, Q)
        dt_exp = _sel_dot_r(dts[pl.ds(r0, Q), :], esel)
        cum = _sel_dot_l(tri, dt_exp * a_exp)
        rowb = _sel_dot_l(ones_q, cum * diag_sel)
        lmat = jnp.exp(jnp.where(mask, cum - rowb, -jnp.inf))
        bq = bc[pl.ds(r0, Q), :].astype(BF16)
        cq = cc[pl.ds(r0, Q), :].astype(BF16)
        cb = _dot_nt(cq, jnp.concatenate([bq] * (GW // Q), axis=0))
        amat = (cb * lmat).astype(BF16)
        xq = xc[pl.ds(r0, Q), :] * dt_exp
        parts = []
        for p in range(GW // 128):
            xp = xq[:, p * 128:(p + 1) * 128]
            xbd = jnp.concatenate([jnp.where(lo_half, xp, 0.0), jnp.where(lo_half, 0.0, xp)], axis=0)
            parts.append(_dot(amat[:, p * 128:(p + 1) * 128], xbd.astype(BF16)))
        y_intra = jnp.concatenate(parts, axis=1)
        s_t = st[d]
        y_inter = _dot(cq, s_t.astype(BF16)) * jnp.exp(cum)
        cum_last = cum[Q - 1:Q] if d == 0 else cum[0:1]
        xdec = (xq * jnp.exp(cum_last - cum)).astype(BF16)
        st[d] = s_t * jnp.exp(cum_last) + _dot_tn(bq, xdec)
        y_ref[pl.ds(r0, Q), :] += y_intra + y_inter

    def scan_body(i, carry):
        chunk_dir(i, 0)
        chunk_dir(nc - 1 - i, 1)
        return carry

    lax.fori_loop(0, nc, scan_body, 0, unroll=SSD_UNROLL)

    if emit_state:
        for d in range(2):
            so_ref[d] = st[d].T


def _ssd_branch(pm, ps, conv_w, conv_b, dt_bias, a_log, d_skip, s0, B, L, emit_state):
    use_s0 = s0 is not None
    G, GW = SSD_GROUPS, SSD_GW
    hpg = SSD_HEADS // G
    dtb = jnp.zeros((1, S_COLS), F32).at[0, S_DT:S_DT + 2 * SSD_HEADS].set(dt_bias.reshape(-1))
    alog = jnp.zeros((1, S_COLS), F32).at[0, S_DT:S_DT + 2 * SSD_HEADS].set(a_log.reshape(-1))
    dsk = jnp.repeat(d_skip, SSD_HEAD_DIM).reshape(1, SSD_INNER)
    cb2 = conv_b.reshape(1, SSD_CONV_DIM)
    nb_x = M_X // GW
    in_specs = [
        pl.BlockSpec((L, GW), lambda b, g: (b, nb_x + g)),
        pl.BlockSpec((L, SSD_STATE), lambda b, g: (b, M_B // SSD_STATE + g)),
        pl.BlockSpec((L, SSD_STATE), lambda b, g: (b, M_C // SSD_STATE + g)),
        pl.BlockSpec((L, S_COLS), lambda b, g: (b, 0)),
        pl.BlockSpec((SSD_CONV_W, GW), lambda b, g: (0, g)),
        pl.BlockSpec((SSD_CONV_W, SSD_STATE), lambda b, g: (0, SSD_INNER // SSD_STATE + g)),
        pl.BlockSpec((SSD_CONV_W, SSD_STATE), lambda b, g: (0, (SSD_INNER + SSD_GN) // SSD_STATE + g)),
        pl.BlockSpec((1, GW), lambda b, g: (0, g)),
        pl.BlockSpec((1, SSD_STATE), lambda b, g: (0, SSD_INNER // SSD_STATE + g)),
        pl.BlockSpec((1, SSD_STATE), lambda b, g: (0, (SSD_INNER + SSD_GN) // SSD_STATE + g)),
        pl.BlockSpec((1, S_COLS), lambda b, g: (0, 0)),
        pl.BlockSpec((1, S_COLS), lambda b, g: (0, 0)),
        pl.BlockSpec((1, GW), lambda b, g: (0, g)),
    ]
    args = [pm, pm, pm, ps, conv_w, conv_w, conv_w, cb2, cb2, cb2, dtb, alog, dsk]
    state_spec = pl.BlockSpec((None, 2, None, GW, SSD_STATE), lambda b, g: (b, 0, g, 0, 0))
    if use_s0:
        in_specs.append(state_spec)
        args.append(s0.reshape(B, 2, G, GW, SSD_STATE))
    out_specs = [pl.BlockSpec((L, GW), lambda b, g: (b, g))]
    out_shape = [jax.ShapeDtypeStruct((B * L, SSD_INNER), F32)]
    if emit_state:
        out_specs.append(state_spec)
        out_shape.append(jax.ShapeDtypeStruct((B, 2, G, GW, SSD_STATE), F32))
    plen = L + 2 * _CONV_LEAD
    outs = pl.pallas_call(
        functools.partial(_ssd_kernel, L=L, use_s0=use_s0, emit_state=emit_state),
        grid=(B, G),
        in_specs=in_specs,
        out_specs=out_specs,
        out_shape=out_shape,
        scratch_shapes=[
            pltpu.VMEM((plen, GW), BF16), pltpu.VMEM((plen, SSD_STATE), BF16), pltpu.VMEM((plen, SSD_STATE), BF16),
            pltpu.VMEM((L, GW), F32), pltpu.VMEM((L, SSD_STATE), F32), pltpu.VMEM((L, SSD_STATE), F32),
            pltpu.VMEM((L, S_COLS), F32), pltpu.VMEM((2, SSD_STATE, GW), F32),
        ],
        compiler_params=_cparams(("arbitrary", "arbitrary")),
        name="ssd_scan",
    )(*args)
    y = outs[0]
    state = outs[1].reshape(B, 2, SSD_HEADS, SSD_HEAD_DIM, SSD_STATE) if emit_state else None
    return y, state


def _ctx_attn_kernel(q_ref, k_ref, v_ref, o_ref):
    q = q_ref[...]
    L = q.shape[0]
    lane = lax.broadcasted_iota(jnp.int32, q.shape, 1)
    lo = lane < NA_HEAD_DIM
    zero = jnp.zeros_like(q)
    q2 = jnp.concatenate([jnp.where(lo, q, zero), jnp.where(lo, zero, q)], axis=0)
    s = _dot_nt(q2, k_ref[...]) * NA_SCALE
    m = jnp.max(s, axis=-1, keepdims=True)
    p = jnp.exp(s - m)
    den = jnp.sum(p, axis=-1, keepdims=True)
    o2 = _dot(p.astype(BF16), v_ref[...]) / den
    o_ref[...] = jnp.where(lo, o2[:L], o2[L:]).astype(o_ref.dtype)


def _context_attention(pm, B, L):
    nq, nk, nv = M_Q // 128, M_K // 128, M_V // 128
    return pl.pallas_call(
        _ctx_attn_kernel,
        grid=(B, NA_HEADS // 2),
        in_specs=[
            pl.BlockSpec((L, 128), lambda b, h: (b, nq + h)),
            pl.BlockSpec((L, 128), lambda b, h: (b, nk + h)),
            pl.BlockSpec((L, 128), lambda b, h: (b, nv + h)),
        ],
        out_specs=pl.BlockSpec((L, 128), lambda b, h: (b, h)),
        out_shape=jax.ShapeDtypeStruct((B * L, NA_INNER), BF16),
        compiler_params=_cparams(("arbitrary", "arbitrary")),
        name="context_attention",
    )(pm, pm, pm)


def _na_bias_table(rpb):
    qc = np.arange(GRID_W)
    cstart = np.clip(qc - NA_WIN_COLS // 2, 0, GRID_W - NA_WIN_COLS)
    kc = np.arange(GRID_W)
    valid = (kc[None, :] >= cstart[:, None]) & (kc[None, :] < cstart[:, None] + NA_WIN_COLS)
    dx = np.clip(kc[None, :] - qc[:, None] + NA_WIN_COLS - 1, 0, 2 * NA_WIN_COLS - 2)
    d0 = np.arange(NA_WIN_ROWS)
    kr = np.arange(NA_WIN_ROWS)
    row_idx = (d0[:, None, None, None] + kr[None, None, :, None]) + np.zeros((1, GRID_W, 1, GRID_W), np.int64)
    col_idx = np.broadcast_to(dx[None, :, None, :], row_idx.shape)
    t = rpb[:, row_idx, col_idx]
    t = jnp.where(jnp.asarray(valid)[None, None, :, None, :], t, NA_NEG)
    H = rpb.shape[0]
    t = t.reshape(H // 2, 2, NA_WIN_ROWS, GRID_W, NA_WIN_ROWS * GRID_W).astype(F32)
    return t.transpose(0, 2, 1, 3, 4).reshape(H // 2, NA_WIN_ROWS, 2 * GRID_W, NA_WIN_ROWS * GRID_W)


def _na_kernel(q_ref, k_ref, v_ref, ck_ref, cv_ref, bias_ref, o_ref, ckb, cvb, *, rows):
    ckb[...] = ck_ref[...].astype(BF16)
    cvb[...] = cv_ref[...].astype(BF16)
    wr = NA_WIN_ROWS
    nloc = wr * GRID_W
    lo = lax.broadcasted_iota(jnp.int32, (GRID_W, 128), 1) < NA_HEAD_DIM

    def row_body(r, carry):
        rs = jnp.clip(r - wr // 2, 0, rows - wr)
        d0 = rs - r + wr - 1
        q = q_ref[pl.ds(pl.multiple_of(r * GRID_W, GRID_W), GRID_W), :]
        k0 = pl.multiple_of(rs * GRID_W, GRID_W)
        kw = k_ref[pl.ds(k0, nloc), :]
        vw = v_ref[pl.ds(k0, nloc), :]
        zero = jnp.zeros_like(q)
        q2 = jnp.concatenate([jnp.where(lo, q, zero), jnp.where(lo, zero, q)], axis=0)
        s_loc = _dot_nt(q2, kw) * NA_SCALE + bias_ref[d0]
        s_ctx = _dot_nt(q2, ckb[...]) * NA_SCALE
        m = jnp.maximum(jnp.max(s_loc, axis=-1, keepdims=True), jnp.max(s_ctx, axis=-1, keepdims=True))
        p_loc = jnp.exp(s_loc - m)
        p_ctx = jnp.exp(s_ctx - m)
        den = jnp.sum(p_loc, axis=-1, keepdims=True) + jnp.sum(p_ctx, axis=-1, keepdims=True)
        o2 = (_dot(p_loc.astype(BF16), vw) + _dot(p_ctx.astype(BF16), cvb[...])) / den
        o = jnp.where(lo, o2[:GRID_W], o2[GRID_W:])
        o_ref[pl.ds(pl.multiple_of(r * GRID_W, GRID_W), GRID_W), :] = o.astype(o_ref.dtype)
        return carry

    lax.fori_loop(0, rows, row_body, 0, unroll=NA_UNROLL)


def _neighbourhood_attention(pm, ck, cv, bias_tab, B, S):
    rows = S // GRID_W
    assert rows >= NA_WIN_ROWS
    Lc = ck.shape[1]
    nq, nk, nv = M_Q // 128, M_K // 128, M_V // 128
    return pl.pallas_call(
        functools.partial(_na_kernel, rows=rows),
        grid=(NA_HEADS // 2, B),
        in_specs=[
            pl.BlockSpec((S, 128), lambda h, b: (b, nq + h)),
            pl.BlockSpec((S, 128), lambda h, b: (b, nk + h)),
            pl.BlockSpec((S, 128), lambda h, b: (b, nv + h)),
            pl.BlockSpec((None, Lc, 128), lambda h, b: (b, 0, h)),
            pl.BlockSpec((None, Lc, 128), lambda h, b: (b, 0, h)),
            pl.BlockSpec((None, NA_WIN_ROWS, 2 * GRID_W, NA_WIN_ROWS * GRID_W), lambda h, b: (h, 0, 0, 0)),
        ],
        out_specs=pl.BlockSpec((S, 128), lambda h, b: (b, h)),
        out_shape=jax.ShapeDtypeStruct((B * S, NA_INNER), BF16),
        scratch_shapes=[pltpu.VMEM((Lc, 128), BF16), pltpu.VMEM((Lc, 128), BF16)],
        compiler_params=_cparams(("arbitrary", "arbitrary")),
        name="neighbourhood_attention",
    )(pm, pm, pm, ck, cv, bias_tab)


def _gla_kernel(*refs, L, use_s0, emit_state):
    it = iter(refs)
    q_ref, k_ref, v_ref, r_ref, ps_ref, wg_ref, bg_ref, ng_ref = (next(it) for _ in range(8))
    s0_ref = next(it) if use_s0 else None
    o_ref = next(it)
    so_ref = next(it) if emit_state else None
    acc, st = next(it), next(it)

    Q, H, DK, DV = GLA_Q, GLA_HEADS, GLA_DK_HEAD, GLA_DV_HEAD
    HQ = H * Q
    nc = L // Q

    def zero_body(c, carry):
        acc[c] = jnp.zeros((HQ, DV), F32)
        return carry

    lax.fori_loop(0, nc, zero_body, 0)

    for d in range(2):
        for h in range(H):
            if use_s0:
                st[d, :, h * DK:(h + 1) * DK] = s0_ref[d, h].T
            else:
                st[d, :, h * DK:(h + 1) * DK] = jnp.zeros((DV, DK), F32)

    ti = lax.broadcasted_iota(jnp.int32, (Q, Q), 0)
    tj = lax.broadcasted_iota(jnp.int32, (Q, Q), 1)
    tris = ((tj <= ti).astype(BF16), (tj >= ti).astype(BF16))
    row_k = lax.broadcasted_iota(jnp.int32, (HQ, H * DK), 0)
    lane_k = lax.broadcasted_iota(jnp.int32, (HQ, H * DK), 1)
    own_k = (row_k // Q) == (lane_k // DK)
    row_a = lax.broadcasted_iota(jnp.int32, (HQ, HQ), 0)
    lane_a = lax.broadcasted_iota(jnp.int32, (HQ, HQ), 1)
    own_a = (row_a // Q) == (lane_a // Q)
    ri = lax.broadcasted_iota(jnp.int32, (HQ, Q), 0) % Q
    rj = lax.broadcasted_iota(jnp.int32, (HQ, Q), 1)
    causal = ((rj <= ri), (rj >= ri))
    rep_r = lax.broadcasted_iota(jnp.int32, (Q, HQ), 0)
    rep_c = lax.broadcasted_iota(jnp.int32, (Q, HQ), 1) % Q
    rep = (rep_r == rep_c).astype(BF16)

    def chunk_dir(c, d):
        r0 = pl.multiple_of(c * Q, Q)
        lr = ps_ref[pl.ds(r0, Q), :].astype(BF16)
        gpre = (_dot(lr, wg_ref[:, d * GLA_DK:(d + 1) * GLA_DK].astype(BF16))
                + bg_ref[:, d * GLA_DK:(d + 1) * GLA_DK])
        g = (jnp.minimum(gpre, 0.0) - jnp.log1p(jnp.exp(-jnp.abs(gpre)))) / GLA_TAU
        bcum = _sel_dot_l(tris[d], g)
        b_end = bcum[Q - 1:Q] if d == 0 else bcum[0:1]
        qc = q_ref[pl.ds(r0, Q), :].astype(F32) * GLA_QK_SCALE
        kc = k_ref[pl.ds(r0, Q), :].astype(F32)
        vc = v_ref[pl.ds(r0, Q), :]
        qt = (qc * jnp.exp(bcum)).astype(BF16)
        kt = (kc * jnp.exp(-bcum)).astype(BF16)
        kend = (kc * jnp.exp(b_end - bcum)).astype(BF16)
        zk = jnp.zeros((HQ, H * DK), BF16)
        q4 = jnp.where(own_k, jnp.concatenate([qt] * H, axis=0), zk)
        k4 = jnp.where(own_k, jnp.concatenate([kend] * H, axis=0), zk)
        v4 = jnp.concatenate([vc[:, h * DV:(h + 1) * DV] for h in range(H)], axis=0)
        att = jnp.where(causal[d], _dot_nt(q4, kt), 0.0)
        a4 = jnp.where(own_a, _dot(att.astype(BF16), rep), 0.0).astype(BF16)
        s_t = st[d]
        o4 = _dot(a4, v4) + _dot_nt(q4, s_t.astype(BF16))
        st[d] = s_t * jnp.exp(b_end) + _dot_tn(v4, k4)
        acc[c] += o4

    def scan_body(i, carry):
        chunk_dir(i, 0)
        chunk_dir(nc - 1 - i, 1)
        return carry

    lax.fori_loop(0, nc, scan_body, 0, unroll=GLA_UNROLL)

    def fin_body(c, carry):
        r0 = pl.multiple_of(c * Q, Q)
        o = acc[c]
        o = o * lax.rsqrt(jnp.mean(o * o, axis=-1, keepdims=True) + EPS) * ng_ref[...]
        for h in range(H):
            gate = _silu(r_ref[pl.ds(r0, Q), h * DV:(h + 1) * DV].astype(F32))
            o_ref[pl.ds(r0, Q), h * DV:(h + 1) * DV] = (o[h * Q:(h + 1) * Q] * gate).astype(o_ref.dtype)
        return carry

    lax.fori_loop(0, nc, fin_body, 0)

    if emit_state:
        for d in range(2):
            for h in range(H):
                so_ref[d, h] = st[d, :, h * DK:(h + 1) * DK].T


def _gla_branch(pm, ps, w_gate, b_gate, norm_g, s0, B, L, emit_state):
    use_s0 = s0 is not None
    H, DK, DV = GLA_HEADS, GLA_DK_HEAD, GLA_DV_HEAD
    wg = jnp.zeros((S_COLS, 2 * GLA_DK), F32)
    for d in range(2):
        wg = wg.at[S_LR + d * GLA_LOWRANK:S_LR + (d + 1) * GLA_LOWRANK, d * GLA_DK:(d + 1) * GLA_DK].set(w_gate[d])
    bg = b_gate.reshape(1, 2 * GLA_DK)
    in_specs = [
        pl.BlockSpec((L, GLA_DK), lambda b: (b, M_GQ // GLA_DK)),
        pl.BlockSpec((L, GLA_DK), lambda b: (b, M_GK // GLA_DK)),
        pl.BlockSpec((L, GLA_DV), lambda b: (b, M_GV // GLA_DV)),
        pl.BlockSpec((L, GLA_DV), lambda b: (b, M_R // GLA_DV)),
        pl.BlockSpec((L, S_COLS), lambda b: (b, 0)),
        pl.BlockSpec((S_COLS, 2 * GLA_DK), lambda b: (0, 0)),
        pl.BlockSpec((1, 2 * GLA_DK), lambda b: (0, 0)),
        pl.BlockSpec((1, DV), lambda b: (0, 0)),
    ]
    args = [pm, pm, pm, pm, ps, wg, bg, norm_g.reshape(1, DV)]
    state_spec = pl.BlockSpec((None, 2, H, DK, DV), lambda b: (b, 0, 0, 0, 0))
    if use_s0:
        in_specs.append(state_spec)
        args.append(s0)
    out_specs = [pl.BlockSpec((L, GLA_DV), lambda b: (b, 0))]
    out_shape = [jax.ShapeDtypeStruct((B * L, GLA_DV), BF16)]
    if emit_state:
        out_specs.append(state_spec)
        out_shape.append(jax.ShapeDtypeStruct((B, 2, H, DK, DV), F32))
    outs = pl.pallas_call(
        functools.partial(_gla_kernel, L=L, use_s0=use_s0, emit_state=emit_state),
        grid=(B,),
        in_specs=in_specs,
        out_specs=out_specs,
        out_shape=out_shape,
        scratch_shapes=[pltpu.VMEM((L // GLA_Q, H * GLA_Q, DV), F32), pltpu.VMEM((2, DV, H * DK), F32)],
        compiler_params=_cparams(("arbitrary",)),
        name="gla_scan",
    )(*args)
    return outs[0], (outs[1] if emit_state else None)


def _merge_kernel(*refs, aliased):
    (x_ref, yssd_ref, z_ref, ona_ref, ogla_ref, g0_ref, g1_ref, g2_ref,
     gate1_ref, shift2_ref, scale2_ref, sng_ref, n2g_ref, wb_ref, wo_ref, rw_ref, rb_ref) = refs[:17]
    xo_ref, h2_ref, lg_ref = refs[-3:]
    y = yssd_ref[...] * _silu(z_ref[...].astype(F32))
    y = y * lax.rsqrt(jnp.mean(y * y, axis=-1, keepdims=True) + EPS) * sng_ref[...]
    m = jax.nn.sigmoid(g0_ref[...].astype(F32)) * _dot(y.astype(BF16), wb_ref[0])
    m = m + jax.nn.sigmoid(g1_ref[...].astype(F32)) * _dot(ona_ref[...], wb_ref[1])
    m = m + jax.nn.sigmoid(g2_ref[...].astype(F32)) * _dot(ogla_ref[...], wb_ref[2])
    x = x_ref[...] + gate1_ref[...] * _dot(m.astype(BF16), wo_ref[...])
    xo_ref[...] = x
    h2 = x * lax.rsqrt(jnp.mean(x * x, axis=-1, keepdims=True) + EPS) * n2g_ref[...]
    h2 = h2 * (1.0 + scale2_ref[...]) + shift2_ref[...]
    h2_ref[...] = h2
    lg_ref[...] = jnp.dot(h2, rw_ref[...], preferred_element_type=F32,
                          precision=lax.Precision.HIGHEST) + rb_ref[...]


def _merge(x2, y_ssd, pm, o_na, o_gla, mod_l, ssd_norm_g, norm2_g, wb_bf, wo_bf, rw_pad, rb_pad,
           rows_per_mod, mod_row0, t_all, row0, shared):
    T = x2.shape[0]
    tm = min(256, rows_per_mod)
    per = rows_per_mod // tm
    D = D_MODEL
    ng = M_GATES // D
    blk0 = row0 // tm

    def mod_map(col):
        return lambda i: (mod_row0 + i // per, 0, col)

    row = lambda i: (i, 0)
    row_off = lambda i: (blk0 + i, 0)
    const2 = lambda i: (0, 0)
    in_specs = [
        pl.BlockSpec((tm, D), row),
        pl.BlockSpec((tm, D), row),
        pl.BlockSpec((tm, D), lambda i: (i, M_Z // D)),
        pl.BlockSpec((tm, D), row),
        pl.BlockSpec((tm, D), row),
        pl.BlockSpec((tm, D), lambda i: (i, ng)),
        pl.BlockSpec((tm, D), lambda i: (i, ng + 1)),
        pl.BlockSpec((tm, D), lambda i: (i, ng + 2)),
        pl.BlockSpec((None, 1, D), mod_map(2)),
        pl.BlockSpec((None, 1, D), mod_map(3)),
        pl.BlockSpec((None, 1, D), mod_map(4)),
        pl.BlockSpec((1, D), const2),
        pl.BlockSpec((1, D), const2),
        pl.BlockSpec((N_BRANCH, D, D), lambda i: (0, 0, 0)),
        pl.BlockSpec((D, D), const2),
        pl.BlockSpec((D, 128), const2),
        pl.BlockSpec((1, 128), const2),
    ]
    args = [x2, y_ssd, pm, o_na, o_gla, pm, pm, pm, mod_l, mod_l, mod_l,
            ssd_norm_g.reshape(1, D), norm2_g.reshape(1, D), wb_bf, wo_bf, rw_pad, rb_pad]
    aliases = {}
    if shared is not None:
        in_specs += [pl.BlockSpec(memory_space=pl.ANY), pl.BlockSpec(memory_space=pl.ANY)]
        aliases = {len(args): 1, len(args) + 1: 2}
        args += list(shared)
    return pl.pallas_call(
        functools.partial(_merge_kernel, aliased=shared is not None),
        grid=(T // tm,),
        in_specs=in_specs,
        out_specs=[pl.BlockSpec((tm, D), row), pl.BlockSpec((tm, D), row_off), pl.BlockSpec((tm, 128), row_off)],
        out_shape=[jax.ShapeDtypeStruct((T, D), F32), jax.ShapeDtypeStruct((t_all, D), F32),
                   jax.ShapeDtypeStruct((t_all, 128), F32)],
        input_output_aliases=aliases,
        compiler_params=_cparams(("arbitrary",)),
        name="branch_merge",
    )(*args)


def _moe_kernel(be_ref, nused_ref, dst_ref, dstn_ref, h2_hbm,
                wgu_ref, bgu_ref, wdn_ref, bdn_ref, y_hbm, xbuf, obuf, wgu_bf, wdn_bf, gsem, ssem):
    b = pl.program_id(0)
    nused = nused_ref[0]
    slot = b % 2
    n_tok = h2_hbm.shape[0]

    def token_of(row):
        if n_tok & (n_tok - 1) == 0:
            return jnp.bitwise_and(row, n_tok - 1)
        return lax.rem(row, n_tok)

    def start_gather(idx_ref, s):
        def body(i, carry):
            pltpu.make_async_copy(h2_hbm.at[pl.ds(token_of(idx_ref[0, i]), 1), :], xbuf.at[s, pl.ds(i, 1), :],
                                  gsem.at[s]).start()
            return carry
        lax.fori_loop(0, MOE_ROWS, body, 0, unroll=MOE_DMA_UNROLL)

    def wait_gather(s):
        pltpu.make_async_copy(h2_hbm.at[pl.ds(0, MOE_ROWS), :], xbuf.at[s], gsem.at[s]).wait()

    def wait_scatter():
        pltpu.make_async_copy(obuf, y_hbm.at[pl.ds(0, MOE_ROWS), :], ssem.at[0]).wait()

    @pl.when(b == 0)
    def _():
        start_gather(dst_ref, 0)

    @pl.when(b < nused)
    def _():
        wait_gather(slot)

        @pl.when(b + 1 < nused)
        def _():
            start_gather(dstn_ref, 1 - slot)

        prev = be_ref[jnp.maximum(b - 1, 0)]

        @pl.when(jnp.logical_or(b == 0, be_ref[b] != prev))
        def _():
            rc = 128

            def cast_body(i, carry):
                r0 = pl.multiple_of(i * rc, rc)
                wgu_bf[pl.ds(r0, rc), :] = wgu_ref[pl.ds(r0, rc), :].astype(BF16)
                wdn_bf[pl.ds(r0, rc), :] = wdn_ref[pl.ds(r0, rc), :].astype(BF16)
                return carry

            lax.fori_loop(0, D_MODEL // rc, cast_body, 0)

        x = xbuf[slot].astype(BF16)
        acc = jnp.zeros((MOE_ROWS, D_MODEL), F32) + bdn_ref[...]
        for f in range(D_FF // MOE_FF_TILE):
            c0 = f * MOE_FF_TILE
            glu = _dot(x, wgu_bf[:, c0:c0 + MOE_FF_TILE]) + bgu_ref[:, c0:c0 + MOE_FF_TILE]
            lin = _dot(x, wgu_bf[:, D_FF + c0:D_FF + c0 + MOE_FF_TILE]) + bgu_ref[:, D_FF + c0:D_FF + c0 + MOE_FF_TILE]
            glu = jnp.minimum(glu, SWIGLU_LIMIT)
            lin = jnp.clip(lin, -SWIGLU_LIMIT, SWIGLU_LIMIT)
            act = glu * jax.nn.sigmoid(SWIGLU_ALPHA * glu) * (lin + 1.0)
            acc = acc + _dot(act.astype(BF16), wdn_bf[c0:c0 + MOE_FF_TILE, :])

        @pl.when(b > 0)
        def _():
            wait_scatter()

        obuf[...] = acc

        def sc_body(i, carry):
            pltpu.make_async_copy(obuf.at[pl.ds(i, 1), :], y_hbm.at[pl.ds(dst_ref[0, i], 1), :], ssem.at[0]).start()
            return carry

        lax.fori_loop(0, MOE_ROWS, sc_body, 0, unroll=MOE_DMA_UNROLL)

        @pl.when(b == nused - 1)
        def _():
            wait_scatter()


def _moe_experts(h2_all, slot_dst, block_e, n_used, w_gu, b_gu, w_dn, b_dn, layer):
    n_blocks = slot_dst.shape[0]
    T, D = h2_all.shape
    smem_blk = lambda f: pl.BlockSpec((None, 1, MOE_ROWS), f, memory_space=pltpu.SMEM)
    grid_spec = pltpu.PrefetchScalarGridSpec(
        num_scalar_prefetch=2,
        grid=(n_blocks,),
        in_specs=[
            smem_blk(lambda b, be, nu: (b, 0, 0)),
            smem_blk(lambda b, be, nu: (jnp.minimum(b + 1, n_blocks - 1), 0, 0)),
            pl.BlockSpec(memory_space=pl.ANY),
            pl.BlockSpec((None, None, D, 2 * D_FF), lambda b, be, nu: (layer, be[b], 0, 0)),
            pl.BlockSpec((None, None, 1, 2 * D_FF), lambda b, be, nu: (layer, be[b], 0, 0)),
            pl.BlockSpec((None, None, D_FF, D), lambda b, be, nu: (layer, be[b], 0, 0)),
            pl.BlockSpec((None, None, 1, D), lambda b, be, nu: (layer, be[b], 0, 0)),
        ],
        out_specs=pl.BlockSpec(memory_space=pl.ANY),
        scratch_shapes=[pltpu.VMEM((2, MOE_ROWS, D), F32), pltpu.VMEM((MOE_ROWS, D), F32),
                        pltpu.VMEM((D, 2 * D_FF), BF16), pltpu.VMEM((D_FF, D), BF16),
                        pltpu.SemaphoreType.DMA((2,)), pltpu.SemaphoreType.DMA((1,))],
    )
    return pl.pallas_call(
        _moe_kernel,
        grid_spec=grid_spec,
        out_shape=jax.ShapeDtypeStruct((TOP_K * T + MOE_ROWS, D), F32),
        compiler_params=_cparams(("arbitrary",)),
        name="moe_experts",
    )(block_e, n_used, slot_dst, slot_dst, h2_all, w_gu,
      b_gu.reshape(DEPTH, N_EXPERTS, 1, 2 * D_FF), w_dn, b_dn.reshape(DEPTH, N_EXPERTS, 1, D))


def _moe_route(logits):
    T = logits.shape[0]
    TK = T * TOP_K
    top_logit, top_idx = lax.top_k(logits, TOP_K)
    top_w = jax.nn.softmax(top_logit, axis=-1)
    flat_e = top_idx.reshape(TK).astype(jnp.int32)
    onehot = (flat_e[:, None] == jnp.arange(N_EXPERTS, dtype=jnp.int32)[None, :]).astype(jnp.int32)
    csum = jnp.cumsum(onehot, axis=0)
    rank = jnp.sum(onehot * csum, axis=1) - 1
    counts = csum[-1]
    padded = (counts + MOE_ROWS - 1) // MOE_ROWS * MOE_ROWS
    pad_end = jnp.cumsum(padded)
    pad_start = pad_end - padded
    dest = jnp.sum(onehot * pad_start[None, :], axis=1) + rank
    n_blocks = -(-(TK + N_EXPERTS * (MOE_ROWS - 1)) // MOE_ROWS)
    n_slots = n_blocks * MOE_ROWS
    flat = jnp.arange(TK, dtype=jnp.int32)
    spare = TK + jnp.arange(n_slots, dtype=jnp.int32) % MOE_ROWS
    slot_dst = spare.at[dest].set((flat % TOP_K) * T + flat // TOP_K)
    blk_start = jnp.arange(n_blocks, dtype=jnp.int32) * MOE_ROWS
    block_e = jnp.sum((blk_start[:, None] >= pad_end[None, :]).astype(jnp.int32), axis=1)
    block_e = jnp.minimum(block_e, N_EXPERTS - 1).astype(jnp.int32)
    n_used = (pad_end[-1] // MOE_ROWS).astype(jnp.int32).reshape(1)
    top_w_pad = jnp.pad(top_w, ((0, 0), (0, 128 - TOP_K)))
    return top_w_pad, slot_dst.reshape(n_blocks, 1, MOE_ROWS), block_e, n_used


def _combine_kernel(x_ref, y0_ref, y1_ref, y2_ref, y3_ref, w_ref, gate_ref, o_ref):
    w = w_ref[...]
    y = w[:, 0:1] * y0_ref[...]
    for k, y_ref in enumerate((y1_ref, y2_ref, y3_ref), start=1):
        y = y + w[:, k:k + 1] * y_ref[...]
    o_ref[...] = x_ref[...] + gate_ref[...] * y


def _moe_combine(x2, y_rows, top_w_pad, mod_l, rows_per_mod, mod_row0, row0, t_all):
    assert TOP_K == 4
    T, D = x2.shape
    tm = min(512, rows_per_mod)
    per = rows_per_mod // tm
    blk0 = row0 // tm
    nblk = t_all // tm

    def y_map(k):
        return lambda i: (k * nblk + blk0 + i, 0)

    return pl.pallas_call(
        _combine_kernel,
        grid=(T // tm,),
        in_specs=[pl.BlockSpec((tm, D), lambda i: (i, 0))]
        + [pl.BlockSpec((tm, D), y_map(k)) for k in range(TOP_K)]
        + [pl.BlockSpec((tm, 128), lambda i: (blk0 + i, 0)),
           pl.BlockSpec((None, 1, D), lambda i: (mod_row0 + i // per, 0, 5))],
        out_specs=pl.BlockSpec((tm, D), lambda i: (i, 0)),
        out_shape=jax.ShapeDtypeStruct((T, D), F32),
        compiler_params=_cparams(("arbitrary",)),
        name="moe_combine",
    )(x2, y_rows, y_rows, y_rows, y_rows, top_w_pad, mod_l)


def _final_norm_kernel(x_ref, g_ref, o_ref):
    x = x_ref[...]
    o_ref[...] = x * lax.rsqrt(jnp.mean(x * x, axis=-1, keepdims=True) + EPS) * g_ref[...]


def _final_norm(x2, g):
    T = x2.shape[0]
    tm = 512
    return pl.pallas_call(
        _final_norm_kernel,
        grid=(T // tm,),
        in_specs=[pl.BlockSpec((tm, D_MODEL), lambda i: (i, 0)), pl.BlockSpec((1, D_MODEL), lambda i: (0, 0))],
        out_specs=pl.BlockSpec((tm, D_MODEL), lambda i: (i, 0)),
        out_shape=jax.ShapeDtypeStruct((T, D_MODEL), F32),
        compiler_params=_cparams(("arbitrary",)),
        name="final_norm",
    )(x2, g.reshape(1, D_MODEL))


def _mixer_half(x2, B, L, mod_l, mod_row0, lw, ctx, t_all, row0, shared):
    latent = ctx is not None
    rows_per_mod = L if latent else B * L
    pm, ps = _in_projection(x2, mod_l, lw["norm1_g"], lw["w_in"], rows_per_mod, mod_row0)
    if latent:
        ck, cv, s_ssd0, s_gla0 = ctx
    else:
        s_ssd0 = s_gla0 = None
    y_ssd, s_ssd = _ssd_branch(pm, ps, lw["ssd_conv_w"], lw["ssd_conv_b"], lw["ssd_dt_bias"], lw["ssd_a_log"],
                               lw["ssd_d"], s_ssd0, B, L, emit_state=not latent)
    if latent:
        o_na = _neighbourhood_attention(pm, ck, cv, lw["na_bias"], B, L)
    else:
        o_na = _context_attention(pm, B, L)
    o_gla, s_gla = _gla_branch(pm, ps, lw["gla_w_gate"], lw["gla_b_gate"], lw["gla_norm_g"], s_gla0, B, L,
                               emit_state=not latent)
    x_new, h2, logits = _merge(x2, y_ssd, pm, o_na, o_gla, mod_l, lw["ssd_norm_g"], lw["norm2_g"],
                               lw["w_branch"], lw["w_out"], lw["router_w"], lw["router_b"], rows_per_mod, mod_row0,
                               t_all, row0, shared)
    return x_new, h2, logits, pm, s_ssd, s_gla


def kernel(x_prompt, x_sample, cache_na_k, cache_na_v, state_ssd, state_gla, c, c_ctx, w_ada, b_ada, norm1_g, norm2_g, w_in, ssd_conv_w, ssd_conv_b, ssd_dt_bias, ssd_a_log, ssd_d, ssd_norm_g, na_rpb, gla_w_gate, gla_b_gate, gla_norm_g, w_branch, w_out, router_w, router_b, moe_w_gu, moe_b_gu, moe_w_dn, moe_b_dn, final_norm_g):
    Bp, Lp, D = x_prompt.shape
    Bs, Ls, _ = x_sample.shape
    Tp, Ts = Bp * Lp, Bs * Ls
    Lc = cache_na_k.shape[2]

    cvecs = jnp.concatenate([c_ctx[None], c, jnp.zeros((8 - 1 - Bs, D), F32)], axis=0)
    mod = _modulation(cvecs, w_ada, b_ada).reshape(DEPTH, 8, 1, 6 * D)

    xp = x_prompt.reshape(Tp, D)
    xs = x_sample.reshape(Ts, D)
    ks_, vs_, sss_, sgs_ = [], [], [], []
    for l in range(DEPTH):
        lw = {
            "norm1_g": norm1_g[l], "norm2_g": norm2_g[l], "w_in": _relayout_w_in(w_in[l]),
            "ssd_conv_w": ssd_conv_w[l], "ssd_conv_b": ssd_conv_b[l], "ssd_dt_bias": ssd_dt_bias[l],
            "ssd_a_log": ssd_a_log[l], "ssd_d": ssd_d[l], "ssd_norm_g": ssd_norm_g[l],
            "na_bias": _na_bias_table(na_rpb[l]),
            "gla_w_gate": gla_w_gate[l], "gla_b_gate": gla_b_gate[l], "gla_norm_g": gla_norm_g[l],
            "w_branch": w_branch[l].astype(BF16), "w_out": w_out[l].astype(BF16),
            "router_w": jnp.pad(router_w[l], ((0, 0), (0, 128 - N_EXPERTS))),
            "router_b": jnp.pad(router_b[l], (0, 128 - N_EXPERTS)).reshape(1, 128),
        }
        mod_l = mod[l]
        t_all = Tp + Ts
        xp, h2a, lga, pmp, s_ssd, s_gla = _mixer_half(xp, Bp, Lp, mod_l, 0, lw, None, t_all, 0, None)
        ctx = (cache_na_k[:, l].reshape(Bs, Lc, NA_INNER), cache_na_v[:, l].reshape(Bs, Lc, NA_INNER),
               state_ssd[:, l], state_gla[:, l])
        xs, h2a, lga, _, _, _ = _mixer_half(xs, Bs, Ls, mod_l, 1, lw, ctx, t_all, Tp, (h2a, lga))
        ks_.append(pmp[:, M_K:M_K + NA_INNER].astype(F32).reshape(Bp, Lp, NA_HEADS, NA_HEAD_DIM))
        vs_.append(pmp[:, M_V:M_V + NA_INNER].astype(F32).reshape(Bp, Lp, NA_HEADS, NA_HEAD_DIM))
        sss_.append(s_ssd)
        sgs_.append(s_gla)

        top_w, slot_dst, block_e, n_used = _moe_route(lga[:, :N_EXPERTS])
        y_rows = _moe_experts(h2a, slot_dst, block_e, n_used, moe_w_gu, moe_b_gu, moe_w_dn, moe_b_dn, l)
        xp = _moe_combine(xp, y_rows, top_w, mod_l, Tp, 0, 0, t_all)
        xs = _moe_combine(xs, y_rows, top_w, mod_l, Ls, 1, Tp, t_all)

    y_prompt = _final_norm(xp, final_norm_g).reshape(Bp, Lp, D)
    y_sample = _final_norm(xs, final_norm_g).reshape(Bs, Ls, D)
    return (y_prompt, y_sample, jnp.stack(ks_, axis=1), jnp.stack(vs_, axis=1),
            jnp.stack(sss_, axis=1), jnp.stack(sgs_, axis=1))
```

```python
import functools

import numpy as np
import jax
import jax.numpy as jnp
from jax import lax
from jax.experimental import pallas as pl
from jax.experimental.pallas import tpu as pltpu

F32 = jnp.float32
BF16 = jnp.bfloat16

D_MODEL = 1024
DEPTH = 4
GRID_W = 64
EPS = 1e-6

SSD_HEADS = 16
SSD_HEAD_DIM = 64
SSD_INNER = SSD_HEADS * SSD_HEAD_DIM
SSD_GROUPS = 2
SSD_STATE = 128
SSD_GN = SSD_GROUPS * SSD_STATE
SSD_CONV_W = 5
SSD_CONV_DIM = SSD_INNER + 2 * SSD_GN
SSD_Q = 64
SSD_GW = SSD_INNER // SSD_GROUPS

NA_HEADS = 16
NA_HEAD_DIM = 64
NA_INNER = NA_HEADS * NA_HEAD_DIM
NA_WIN_ROWS = 8
NA_WIN_COLS = 16
NA_SCALE = NA_HEAD_DIM ** -0.5
NA_NEG = -1e30

GLA_HEADS = 4
GLA_DK = D_MODEL // 2
GLA_DV = D_MODEL
GLA_DK_HEAD = GLA_DK // GLA_HEADS
GLA_DV_HEAD = GLA_DV // GLA_HEADS
GLA_LOWRANK = 16
GLA_TAU = 16.0
GLA_Q = 32
GLA_QK_SCALE = GLA_DK_HEAD ** -0.5

N_BRANCH = 3
N_EXPERTS = 32
TOP_K = 4
D_FF = D_MODEL
SWIGLU_ALPHA = 1.702
SWIGLU_LIMIT = 7.0
MOE_ROWS = 512
MOE_FF_TILE = 512
MOE_DMA_UNROLL = 8

SSD_UNROLL = 2
GLA_UNROLL = 4
NA_UNROLL = 4

OFF_SSD_XBC = SSD_INNER
OFF_SSD_DT = OFF_SSD_XBC + SSD_CONV_DIM
OFF_NA_QKV = OFF_SSD_DT + 2 * SSD_HEADS
OFF_GLA_Q = OFF_NA_QKV + 3 * NA_INNER
OFF_GLA_LR = OFF_GLA_Q + 2 * GLA_DK + 2 * GLA_DV
OFF_GATES = OFF_GLA_LR + 2 * GLA_LOWRANK
IN_COLS = OFF_GATES + N_BRANCH * D_MODEL

M_Z = 0
M_X = M_Z + SSD_INNER
M_Q = M_X + SSD_INNER
M_K = M_Q + NA_INNER
M_V = M_K + NA_INNER
M_GV = M_V + NA_INNER
M_R = M_GV + GLA_DV
M_GATES = M_R + GLA_DV
M_GQ = M_GATES + N_BRANCH * D_MODEL
M_GK = M_GQ + GLA_DK
M_B = M_GK + GLA_DK
M_C = M_B + SSD_GN
M_COLS = M_C + SSD_GN
_MAIN_SEGMENTS = (
    (0, 2 * SSD_INNER),
    (OFF_NA_QKV, 3 * NA_INNER),
    (OFF_GLA_Q + 2 * GLA_DK, 2 * GLA_DV),
    (OFF_GATES, N_BRANCH * D_MODEL),
    (OFF_GLA_Q, 2 * GLA_DK),
    (OFF_SSD_XBC + SSD_INNER, 2 * SSD_GN),
)
_SMALL_SEGMENTS = ((OFF_SSD_DT, 2 * SSD_HEADS), (OFF_GLA_LR, 2 * GLA_LOWRANK))
S_DT = 0
S_LR = 2 * SSD_HEADS
S_COLS = 128
PROJ_TN = 512
PROJ_COLS = M_COLS + PROJ_TN
PROJ_TM = 2048

VMEM_LIMIT = 56 * 1024 * 1024


def _cparams(sem):
    return pltpu.CompilerParams(dimension_semantics=sem, vmem_limit_bytes=VMEM_LIMIT)


def _silu(x):
    return x * jax.nn.sigmoid(x)


def _softplus(x):
    return jnp.maximum(x, 0.0) + jnp.log1p(jnp.exp(-jnp.abs(x)))


def _split3(x):
    hi = x.astype(BF16)
    r1 = x - hi.astype(F32)
    mid = r1.astype(BF16)
    lo = (r1 - mid.astype(F32)).astype(BF16)
    return hi, mid, lo


def _dot(a, b):
    return jnp.dot(a, b, preferred_element_type=F32)


def _dot_nt(a, b):
    return lax.dot_general(a, b, (((1,), (1,)), ((), ())), preferred_element_type=F32)


def _dot_tn(a, b):
    return lax.dot_general(a, b, (((0,), (0,)), ((), ())), preferred_element_type=F32)


def _sel_dot_l(sel_bf, x):
    hi, mid, lo = _split3(x)
    return (_dot(sel_bf, lo) + _dot(sel_bf, mid)) + _dot(sel_bf, hi)


def _sel_dot_r(x, sel_bf):
    hi, mid, lo = _split3(x)
    return (_dot(lo, sel_bf) + _dot(mid, sel_bf)) + _dot(hi, sel_bf)


def _mod_kernel(c_ref, w_ref, b_ref, o_ref):
    c = c_ref[...]
    o_ref[...] = jnp.dot(_silu(c), w_ref[...], preferred_element_type=F32,
                         precision=lax.Precision.HIGHEST) + b_ref[...]


def _modulation(cvecs, w_ada, b_ada):
    nrow = cvecs.shape[0]
    return pl.pallas_call(
        _mod_kernel,
        grid=(DEPTH, 6),
        in_specs=[
            pl.BlockSpec((nrow, D_MODEL), lambda l, j: (0, 0)),
            pl.BlockSpec((None, D_MODEL, D_MODEL), lambda l, j: (l, 0, j)),
            pl.BlockSpec((None, 1, D_MODEL), lambda l, j: (l, 0, j)),
        ],
        out_specs=pl.BlockSpec((None, nrow, D_MODEL), lambda l, j: (l, 0, j)),
        out_shape=jax.ShapeDtypeStruct((DEPTH, nrow, 6 * D_MODEL), F32),
        compiler_params=_cparams(("arbitrary", "arbitrary")),
        name="adaln_mod",
    )(cvecs, w_ada, b_ada.reshape(DEPTH, 1, 6 * D_MODEL))


def _inproj_kernel(x_ref, g_ref, shift_ref, scale_ref, w_ref, om_ref, os_ref, h_ref, *, n_main):
    j = pl.program_id(1)

    @pl.when(j == 0)
    def _():
        x = x_ref[...]
        h = x * lax.rsqrt(jnp.mean(x * x, axis=-1, keepdims=True) + EPS) * g_ref[...]
        h = h * (1.0 + scale_ref[...]) + shift_ref[...]
        h_ref[...] = h.astype(BF16)

    acc = _dot(h_ref[...], w_ref[...])

    @pl.when(j < n_main)
    def _():
        om_ref[...] = acc.astype(BF16)

    @pl.when(j == n_main)
    def _():
        os_ref[...] = acc[:, :S_COLS]


def _relayout_w_in(w):
    parts = [w[:, s:s + n] for s, n in _MAIN_SEGMENTS + _SMALL_SEGMENTS]
    parts.append(jnp.zeros((w.shape[0], PROJ_TN - 2 * SSD_HEADS - 2 * GLA_LOWRANK), w.dtype))
    return jnp.concatenate(parts, axis=1).astype(BF16)


def _in_projection(x2, mod_l, norm_g, w_perm, rows_per_mod, mod_row0):
    T = x2.shape[0]
    tm = min(PROJ_TM, rows_per_mod)
    n_main = M_COLS // PROJ_TN
    per = rows_per_mod // tm

    def mod_map(col):
        return lambda i, j: (mod_row0 + i // per, 0, col)

    return pl.pallas_call(
        functools.partial(_inproj_kernel, n_main=n_main),
        grid=(T // tm, n_main + 1),
        in_specs=[
            pl.BlockSpec((tm, D_MODEL), lambda i, j: (i, 0)),
            pl.BlockSpec((1, D_MODEL), lambda i, j: (0, 0)),
            pl.BlockSpec((None, 1, D_MODEL), mod_map(0)),
            pl.BlockSpec((None, 1, D_MODEL), mod_map(1)),
            pl.BlockSpec((D_MODEL, PROJ_TN), lambda i, j: (0, j)),
        ],
        out_specs=[
            pl.BlockSpec((tm, PROJ_TN), lambda i, j: (i, jnp.minimum(j, n_main - 1))),
            pl.BlockSpec((tm, S_COLS), lambda i, j: (i, 0)),
        ],
        out_shape=[jax.ShapeDtypeStruct((T, M_COLS), BF16), jax.ShapeDtypeStruct((T, S_COLS), F32)],
        scratch_shapes=[pltpu.VMEM((tm, D_MODEL), BF16)],
        compiler_params=_cparams(("arbitrary", "arbitrary")),
        name="in_projection",
    )(x2, norm_g.reshape(1, D_MODEL), mod_l, mod_l, w_perm)


_CONV_WIN = 128
_CONV_LEAD = 64
_CONV_OFF = 48


def _ssd_kernel(*refs, L, use_s0, emit_state):
    it = iter(refs)
    x_ref, b_ref, c_ref, ps_ref = next(it), next(it), next(it), next(it)
    cwx_ref, cwb_ref, cwc_ref = next(it), next(it), next(it)
    cbx_ref, cbb_ref, cbc_ref = next(it), next(it), next(it)
    dtb_ref, alog_ref, dsk_ref = next(it), next(it), next(it)
    s0_ref = next(it) if use_s0 else None
    y_ref = next(it)
    so_ref = next(it) if emit_state else None
    xpad, bpad, cpad, xc, bc, cc, dts, st = (next(it) for _ in range(8))

    Q = SSD_Q
    nc = L // Q
    g = pl.program_id(1)
    pad = SSD_CONV_W // 2

    for src, dst in ((x_ref, xpad), (b_ref, bpad), (c_ref, cpad)):
        w = dst.shape[1]
        dst[0:_CONV_LEAD, :] = jnp.zeros((_CONV_LEAD, w), BF16)
        dst[_CONV_LEAD + L:_CONV_LEAD + L + _CONV_LEAD, :] = jnp.zeros((_CONV_LEAD, w), BF16)

    def copy_body(c, carry):
        r0 = pl.multiple_of(c * Q, Q)
        for src, dst in ((x_ref, xpad), (b_ref, bpad), (c_ref, cpad)):
            dst[pl.ds(r0 + _CONV_LEAD, Q), :] = src[pl.ds(r0, Q), :]
        return carry

    lax.fori_loop(0, nc, copy_body, 0)

    ri = lax.broadcasted_iota(jnp.int32, (SSD_CONV_W * Q, _CONV_WIN), 0)
    ci = lax.broadcasted_iota(jnp.int32, (SSD_CONV_W * Q, _CONV_WIN), 1)
    shift_sel = (ci == (ri % Q) + (ri // Q) + (_CONV_LEAD - _CONV_OFF - pad)).astype(BF16)

    def conv_chunk(pad_ref, w_ref, bias_ref, r0):
        win = pad_ref[pl.ds(r0 + _CONV_OFF, _CONV_WIN), :]
        sh = _dot(shift_sel, win)
        acc = bias_ref[...] + w_ref[0:1, :] * sh[0:Q]
        for k in range(1, SSD_CONV_W):
            acc = acc + w_ref[k:k + 1, :] * sh[k * Q:(k + 1) * Q]
        return _silu(acc)

    def prep_body(c, carry):
        r0 = pl.multiple_of(c * Q, Q)
        xv = conv_chunk(xpad, cwx_ref, cbx_ref, r0)
        xc[pl.ds(r0, Q), :] = xv
        bc[pl.ds(r0, Q), :] = conv_chunk(bpad, cwb_ref, cbb_ref, r0)
        cc[pl.ds(r0, Q), :] = conv_chunk(cpad, cwc_ref, cbc_ref, r0)
        dts[pl.ds(r0, Q), :] = _softplus(ps_ref[pl.ds(r0, Q), :] + dtb_ref[...])
        y_ref[pl.ds(r0, Q), :] = dsk_ref[...] * xv
        return carry

    lax.fori_loop(0, nc, prep_body, 0)

    GW = SSD_GW
    hpg = SSD_HEADS // SSD_GROUPS
    er = lax.broadcasted_iota(jnp.int32, (S_COLS, GW), 0)
    ec = lax.broadcasted_iota(jnp.int32, (S_COLS, GW), 1)
    qi = lax.broadcasted_iota(jnp.int32, (Q, GW), 0)
    qj = lax.broadcasted_iota(jnp.int32, (Q, GW), 1) % Q
    diag_sel = (qi == qj).astype(F32)
    ones_q = jnp.ones((Q, Q), BF16)
    ti = lax.broadcasted_iota(jnp.int32, (Q, Q), 0)
    tj = lax.broadcasted_iota(jnp.int32, (Q, Q), 1)
    lane = lax.broadcasted_iota(jnp.int32, (Q, 128), 1)
    lo_half = lane < SSD_HEAD_DIM
    a_all = -jnp.exp(alog_ref[...])

    per_dir = []
    for d in range(2):
        base = S_DT + d * SSD_HEADS + g * hpg
        esel = (er == base + ec // SSD_HEAD_DIM).astype(BF16)
        a_exp = _sel_dot_r(jnp.broadcast_to(a_all, (8, S_COLS)), esel)[0:1]
        if d == 0:
            tri = (tj <= ti).astype(BF16)
            mask = qj <= qi
        else:
            tri = (tj >= ti).astype(BF16)
            mask = qj >= qi
        per_dir.append((esel, a_exp, tri, mask))

    for d in range(2):
        if use_s0:
            st[d] = s0_ref[d].T
        else:
            st[d] = jnp.zeros((SSD_STATE, GW), F32)

    def chunk_dir(c, d):
        esel, a_exp, tri, mask = per_dir[d]
        r0 = pl.multiple_of(c * Q, Q)
        dt_exp = _sel_dot_r(dts[pl.ds(r0, Q), :], esel)
        cum = _sel_dot_l(tri, dt_exp * a_exp)
        rowb = _sel_dot_l(ones_q, cum * diag_sel)
        lmat = jnp.exp(jnp.where(mask, cum - rowb, -jnp.inf))
        bq = bc[pl.ds(r0, Q), :].astype(BF16)
        cq = cc[pl.ds(r0, Q), :].astype(BF16)
        cb = _dot_nt(cq, jnp.concatenate([bq] * (GW // Q), axis=0))
        amat = (cb * lmat).astype(BF16)
        xq = xc[pl.ds(r0, Q), :] * dt_exp
        parts = []
        for p in range(GW // 128):
            xp = xq[:, p * 128:(p + 1) * 128]
            xbd = jnp.concatenate([jnp.where(lo_half, xp, 0.0), jnp.where(lo_half, 0.0, xp)], axis=0)
            parts.append(_dot(amat[:, p * 128:(p + 1) * 128], xbd.astype(BF16)))
        y_intra = jnp.concatenate(parts, axis=1)
        s_t = st[d]
        y_inter = _dot(cq, s_t.astype(BF16)) * jnp.exp(cum)
        cum_last = cum[Q - 1:Q] if d == 0 else cum[0:1]
        xdec = (xq * jnp.exp(cum_last - cum)).astype(BF16)
        st[d] = s_t * jnp.exp(cum_last) + _dot_tn(bq, xdec)
        y_ref[pl.ds(r0, Q), :] += y_intra + y_inter

    def scan_body(i, carry):
        chunk_dir(i, 0)
        chunk_dir(nc - 1 - i, 1)
        return carry

    lax.fori_loop(0, nc, scan_body, 0, unroll=SSD_UNROLL)

    if emit_state:
        for d in range(2):
            so_ref[d] = st[d].T


def _ssd_branch(pm, ps, conv_w, conv_b, dt_bias, a_log, d_skip, s0, B, L, emit_state):
    use_s0 = s0 is not None
    G, GW = SSD_GROUPS, SSD_GW
    hpg = SSD_HEADS // G
    dtb = jnp.zeros((1, S_COLS), F32).at[0, S_DT:S_DT + 2 * SSD_HEADS].set(dt_bias.reshape(-1))
    alog = jnp.zeros((1, S_COLS), F32).at[0, S_DT:S_DT + 2 * SSD_HEADS].set(a_log.reshape(-1))
    dsk = jnp.repeat(d_skip, SSD_HEAD_DIM).reshape(1, SSD_INNER)
    cb2 = conv_b.reshape(1, SSD_CONV_DIM)
    nb_x = M_X // GW
    in_specs = [
        pl.BlockSpec((L, GW), lambda b, g: (b, nb_x + g)),
        pl.BlockSpec((L, SSD_STATE), lambda b, g: (b, M_B // SSD_STATE + g)),
        pl.BlockSpec((L, SSD_STATE), lambda b, g: (b, M_C // SSD_STATE + g)),
        pl.BlockSpec((L, S_COLS), lambda b, g: (b, 0)),
        pl.BlockSpec((SSD_CONV_W, GW), lambda b, g: (0, g)),
        pl.BlockSpec((SSD_CONV_W, SSD_STATE), lambda b, g: (0, SSD_INNER // SSD_STATE + g)),
        pl.BlockSpec((SSD_CONV_W, SSD_STATE), lambda b, g: (0, (SSD_INNER + SSD_GN) // SSD_STATE + g)),
        pl.BlockSpec((1, GW), lambda b, g: (0, g)),
        pl.BlockSpec((1, SSD_STATE), lambda b, g: (0, SSD_INNER // SSD_STATE + g)),
        pl.BlockSpec((1, SSD_STATE), lambda b, g: (0, (SSD_INNER + SSD_GN) // SSD_STATE + g)),
        pl.BlockSpec((1, S_COLS), lambda b, g: (0, 0)),
        pl.BlockSpec((1, S_COLS), lambda b, g: (0, 0)),
        pl.BlockSpec((1, GW), lambda b, g: (0, g)),
    ]
    args = [pm, pm, pm, ps, conv_w, conv_w, conv_w, cb2, cb2, cb2, dtb, alog, dsk]
    state_spec = pl.BlockSpec((None, 2, None, GW, SSD_STATE), lambda b, g: (b, 0, g, 0, 0))
    if use_s0:
        in_specs.append(state_spec)
        args.append(s0.reshape(B, 2, G, GW, SSD_STATE))
    out_specs = [pl.BlockSpec((L, GW), lambda b, g: (b, g))]
    out_shape = [jax.ShapeDtypeStruct((B * L, SSD_INNER), F32)]
    if emit_state:
        out_specs.append(state_spec)
        out_shape.append(jax.ShapeDtypeStruct((B, 2, G, GW, SSD_STATE), F32))
    plen = L + 2 * _CONV_LEAD
    outs = pl.pallas_call(
        functools.partial(_ssd_kernel, L=L, use_s0=use_s0, emit_state=emit_state),
        grid=(B, G),
        in_specs=in_specs,
        out_specs=out_specs,
        out_shape=out_shape,
        scratch_shapes=[
            pltpu.VMEM((plen, GW), BF16), pltpu.VMEM((plen, SSD_STATE), BF16), pltpu.VMEM((plen, SSD_STATE), BF16),
            pltpu.VMEM((L, GW), F32), pltpu.VMEM((L, SSD_STATE), F32), pltpu.VMEM((L, SSD_STATE), F32),
            pltpu.VMEM((L, S_COLS), F32), pltpu.VMEM((2, SSD_STATE, GW), F32),
        ],
        compiler_params=_cparams(("arbitrary", "arbitrary")),
        name="ssd_scan",
    )(*args)
    y = outs[0]
    state = outs[1].reshape(B, 2, SSD_HEADS, SSD_HEAD_DIM, SSD_STATE) if emit_state else None
    return y, state


def _ctx_attn_kernel(q_ref, k_ref, v_ref, o_ref):
    q = q_ref[...]
    L = q.shape[0]
    lane = lax.broadcasted_iota(jnp.int32, q.shape, 1)
    lo = lane < NA_HEAD_DIM
    zero = jnp.zeros_like(q)
    q2 = jnp.concatenate([jnp.where(lo, q, zero), jnp.where(lo, zero, q)], axis=0)
    s = _dot_nt(q2, k_ref[...]) * NA_SCALE
    m = jnp.max(s, axis=-1, keepdims=True)
    p = jnp.exp(s - m)
    den = jnp.sum(p, axis=-1, keepdims=True)
    o2 = _dot(p.astype(BF16), v_ref[...]) / den
    o_ref[...] = jnp.where(lo, o2[:L], o2[L:]).astype(o_ref.dtype)


def _context_attention(pm, B, L):
    nq, nk, nv = M_Q // 128, M_K // 128, M_V // 128
    return pl.pallas_call(
        _ctx_attn_kernel,
        grid=(B, NA_HEADS // 2),
        in_specs=[
            pl.BlockSpec((L, 128), lambda b, h: (b, nq + h)),
            pl.BlockSpec((L, 128), lambda b, h: (b, nk + h)),
            pl.BlockSpec((L, 128), lambda b, h: (b, nv + h)),
        ],
        out_specs=pl.BlockSpec((L, 128), lambda b, h: (b, h)),
        out_shape=jax.ShapeDtypeStruct((B * L, NA_INNER), BF16),
        compiler_params=_cparams(("arbitrary", "arbitrary")),
        name="context_attention",
    )(pm, pm, pm)


def _na_bias_table(rpb):
    qc = np.arange(GRID_W)
    cstart = np.clip(qc - NA_WIN_COLS // 2, 0, GRID_W - NA_WIN_COLS)
    kc = np.arange(GRID_W)
    valid = (kc[None, :] >= cstart[:, None]) & (kc[None, :] < cstart[:, None] + NA_WIN_COLS)
    dx = np.clip(kc[None, :] - qc[:, None] + NA_WIN_COLS - 1, 0, 2 * NA_WIN_COLS - 2)
    H, n_dr, n_dx = rpb.shape
    pick = (dx[None, :, :] == np.arange(n_dx)[:, None, None]).astype(np.float32).reshape(n_dx, GRID_W * GRID_W)
    cols = jnp.dot(rpb.reshape(H * n_dr, n_dx), jnp.asarray(pick), precision=lax.Precision.HIGHEST)
    cols = cols.reshape(H, n_dr, GRID_W, GRID_W)
    t = jnp.stack([cols[:, d0:d0 + NA_WIN_ROWS] for d0 in range(NA_WIN_ROWS)], axis=1)
    t = t.transpose(0, 1, 3, 2, 4)
    t = jnp.where(jnp.asarray(valid)[None, None, :, None, :], t, NA_NEG)
    t = t.reshape(H // 2, 2, NA_WIN_ROWS, GRID_W, NA_WIN_ROWS * GRID_W).astype(F32)
    return t.transpose(0, 2, 1, 3, 4).reshape(H // 2, NA_WIN_ROWS, 2 * GRID_W, NA_WIN_ROWS * GRID_W)


def _na_kernel(q_ref, k_ref, v_ref, ck_ref, cv_ref, bias_ref, o_ref, ckb, cvb, *, rows):
    ckb[...] = ck_ref[...].astype(BF16)
    cvb[...] = cv_ref[...].astype(BF16)
    wr = NA_WIN_ROWS
    nloc = wr * GRID_W
    lo = lax.broadcasted_iota(jnp.int32, (GRID_W, 128), 1) < NA_HEAD_DIM

    def row_body(r, carry):
        rs = jnp.clip(r - wr // 2, 0, rows - wr)
        d0 = rs - r + wr - 1
        q = q_ref[pl.ds(pl.multiple_of(r * GRID_W, GRID_W), GRID_W), :]
        k0 = pl.multiple_of(rs * GRID_W, GRID_W)
        kw = k_ref[pl.ds(k0, nloc), :]
        vw = v_ref[pl.ds(k0, nloc), :]
        zero = jnp.zeros_like(q)
        q2 = jnp.concatenate([jnp.where(lo, q, zero), jnp.where(lo, zero, q)], axis=0)
        s_loc = _dot_nt(q2, kw) * NA_SCALE + bias_ref[d0]
        s_ctx = _dot_nt(q2, ckb[...]) * NA_SCALE
        m = jnp.maximum(jnp.max(s_loc, axis=-1, keepdims=True), jnp.max(s_ctx, axis=-1, keepdims=True))
        p_loc = jnp.exp(s_loc - m)
        p_ctx = jnp.exp(s_ctx - m)
        den = jnp.sum(p_loc, axis=-1, keepdims=True) + jnp.sum(p_ctx, axis=-1, keepdims=True)
        o2 = (_dot(p_loc.astype(BF16), vw) + _dot(p_ctx.astype(BF16), cvb[...])) / den
        o = jnp.where(lo, o2[:GRID_W], o2[GRID_W:])
        o_ref[pl.ds(pl.multiple_of(r * GRID_W, GRID_W), GRID_W), :] = o.astype(o_ref.dtype)
        return carry

    lax.fori_loop(0, rows, row_body, 0, unroll=NA_UNROLL)


def _neighbourhood_attention(pm, ck, cv, bias_tab, B, S):
    rows = S // GRID_W
    assert rows >= NA_WIN_ROWS
    Lc = ck.shape[1]
    nq, nk, nv = M_Q // 128, M_K // 128, M_V // 128
    return pl.pallas_call(
        functools.partial(_na_kernel, rows=rows),
        grid=(NA_HEADS // 2, B),
        in_specs=[
            pl.BlockSpec((S, 128), lambda h, b: (b, nq + h)),
            pl.BlockSpec((S, 128), lambda h, b: (b, nk + h)),
            pl.BlockSpec((S, 128), lambda h, b: (b, nv + h)),
            pl.BlockSpec((None, Lc, 128), lambda h, b: (b, 0, h)),
            pl.BlockSpec((None, Lc, 128), lambda h, b: (b, 0, h)),
            pl.BlockSpec((None, NA_WIN_ROWS, 2 * GRID_W, NA_WIN_ROWS * GRID_W), lambda h, b: (h, 0, 0, 0)),
        ],
        out_specs=pl.BlockSpec((S, 128), lambda h, b: (b, h)),
        out_shape=jax.ShapeDtypeStruct((B * S, NA_INNER), BF16),
        scratch_shapes=[pltpu.VMEM((Lc, 128), BF16), pltpu.VMEM((Lc, 128), BF16)],
        compiler_params=_cparams(("arbitrary", "arbitrary")),
        name="neighbourhood_attention",
    )(pm, pm, pm, ck, cv, bias_tab)


def _gla_kernel(*refs, L, use_s0, emit_state):
    it = iter(refs)
    q_ref, k_ref, v_ref, r_ref, ps_ref, wg_ref, bg_ref, ng_ref = (next(it) for _ in range(8))
    s0_ref = next(it) if use_s0 else None
    o_ref = next(it)
    so_ref = next(it) if emit_state else None
    acc, st = next(it), next(it)

    Q, H, DK, DV = GLA_Q, GLA_HEADS, GLA_DK_HEAD, GLA_DV_HEAD
    HQ = H * Q
    nc = L // Q

    def zero_body(c, carry):
        acc[c] = jnp.zeros((HQ, DV), F32)
        return carry

    lax.fori_loop(0, nc, zero_body, 0)

    for d in range(2):
        for h in range(H):
            if use_s0:
                st[d, :, h * DK:(h + 1) * DK] = s0_ref[d, h].T
            else:
                st[d, :, h * DK:(h + 1) * DK] = jnp.zeros((DV, DK), F32)

    ti = lax.broadcasted_iota(jnp.int32, (Q, Q), 0)
    tj = lax.broadcasted_iota(jnp.int32, (Q, Q), 1)
    tris = ((tj <= ti).astype(BF16), (tj >= ti).astype(BF16))
    row_k = lax.broadcasted_iota(jnp.int32, (HQ, H * DK), 0)
    lane_k = lax.broadcasted_iota(jnp.int32, (HQ, H * DK), 1)
    own_k = (row_k // Q) == (lane_k // DK)
    row_a = lax.broadcasted_iota(jnp.int32, (HQ, HQ), 0)
    lane_a = lax.broadcasted_iota(jnp.int32, (HQ, HQ), 1)
    own_a = (row_a // Q) == (lane_a // Q)
    ri = lax.broadcasted_iota(jnp.int32, (HQ, Q), 0) % Q
    rj = lax.broadcasted_iota(jnp.int32, (HQ, Q), 1)
    causal = ((rj <= ri), (rj >= ri))
    rep_r = lax.broadcasted_iota(jnp.int32, (Q, HQ), 0)
    rep_c = lax.broadcasted_iota(jnp.int32, (Q, HQ), 1) % Q
    rep = (rep_r == rep_c).astype(BF16)

    def chunk_dir(c, d):
        r0 = pl.multiple_of(c * Q, Q)
        lr = ps_ref[pl.ds(r0, Q), :].astype(BF16)
        gpre = (_dot(lr, wg_ref[:, d * GLA_DK:(d + 1) * GLA_DK].astype(BF16))
                + bg_ref[:, d * GLA_DK:(d + 1) * GLA_DK])
        g = (jnp.minimum(gpre, 0.0) - jnp.log1p(jnp.exp(-jnp.abs(gpre)))) / GLA_TAU
        bcum = _sel_dot_l(tris[d], g)
        b_end = bcum[Q - 1:Q] if d == 0 else bcum[0:1]
        qc = q_ref[pl.ds(r0, Q), :].astype(F32) * GLA_QK_SCALE
        kc = k_ref[pl.ds(r0, Q), :].astype(F32)
        vc = v_ref[pl.ds(r0, Q), :]
        qt = (qc * jnp.exp(bcum)).astype(BF16)
        kt = (kc * jnp.exp(-bcum)).astype(BF16)
        kend = (kc * jnp.exp(b_end - bcum)).astype(BF16)
        zk = jnp.zeros((HQ, H * DK), BF16)
        q4 = jnp.where(own_k, jnp.concatenate([qt] * H, axis=0), zk)
        k4 = jnp.where(own_k, jnp.concatenate([kend] * H, axis=0), zk)
        v4 = jnp.concatenate([vc[:, h * DV:(h + 1) * DV] for h in range(H)], axis=0)
        att = jnp.where(causal[d], _dot_nt(q4, kt), 0.0)
        a4 = jnp.where(own_a, _dot(att.astype(BF16), rep), 0.0).astype(BF16)
        s_t = st[d]
        o4 = _dot(a4, v4) + _dot_nt(q4, s_t.astype(BF16))
        st[d] = s_t * jnp.exp(b_end) + _dot_tn(v4, k4)
        acc[c] += o4

    def scan_body(i, carry):
        chunk_dir(i, 0)
        chunk_dir(nc - 1 - i, 1)
        return carry

    lax.fori_loop(0, nc, scan_body, 0, unroll=GLA_UNROLL)

    def fin_body(c, carry):
        r0 = pl.multiple_of(c * Q, Q)
        o = acc[c]
        o = o * lax.rsqrt(jnp.mean(o * o, axis=-1, keepdims=True) + EPS) * ng_ref[...]
        for h in range(H):
            gate = _silu(r_ref[pl.ds(r0, Q), h * DV:(h + 1) * DV].astype(F32))
            o_ref[pl.ds(r0, Q), h * DV:(h + 1) * DV] = (o[h * Q:(h + 1) * Q] * gate).astype(o_ref.dtype)
        return carry

    lax.fori_loop(0, nc, fin_body, 0)

    if emit_state:
        for d in range(2):
            for h in range(H):
                so_ref[d, h] = st[d, :, h * DK:(h + 1) * DK].T


def _gla_branch(pm, ps, w_gate, b_gate, norm_g, s0, B, L, emit_state):
    use_s0 = s0 is not None
    H, DK, DV = GLA_HEADS, GLA_DK_HEAD, GLA_DV_HEAD
    wg = jnp.zeros((S_COLS, 2 * GLA_DK), F32)
    for d in range(2):
        wg = wg.at[S_LR + d * GLA_LOWRANK:S_LR + (d + 1) * GLA_LOWRANK, d * GLA_DK:(d + 1) * GLA_DK].set(w_gate[d])
    bg = b_gate.reshape(1, 2 * GLA_DK)
    in_specs = [
        pl.BlockSpec((L, GLA_DK), lambda b: (b, M_GQ // GLA_DK)),
        pl.BlockSpec((L, GLA_DK), lambda b: (b, M_GK // GLA_DK)),
        pl.BlockSpec((L, GLA_DV), lambda b: (b, M_GV // GLA_DV)),
        pl.BlockSpec((L, GLA_DV), lambda b: (b, M_R // GLA_DV)),
        pl.BlockSpec((L, S_COLS), lambda b: (b, 0)),
        pl.BlockSpec((S_COLS, 2 * GLA_DK), lambda b: (0, 0)),
        pl.BlockSpec((1, 2 * GLA_DK), lambda b: (0, 0)),
        pl.BlockSpec((1, DV), lambda b: (0, 0)),
    ]
    args = [pm, pm, pm, pm, ps, wg, bg, norm_g.reshape(1, DV)]
    state_spec = pl.BlockSpec((None, 2, H, DK, DV), lambda b: (b, 0, 0, 0, 0))
    if use_s0:
        in_specs.append(state_spec)
        args.append(s0)
    out_specs = [pl.BlockSpec((L, GLA_DV), lambda b: (b, 0))]
    out_shape = [jax.ShapeDtypeStruct((B * L, GLA_DV), BF16)]
    if emit_state:
        out_specs.append(state_spec)
        out_shape.append(jax.ShapeDtypeStruct((B, 2, H, DK, DV), F32))
    outs = pl.pallas_call(
        functools.partial(_gla_kernel, L=L, use_s0=use_s0, emit_state=emit_state),
        grid=(B,),
        in_specs=in_specs,
        out_specs=out_specs,
        out_shape=out_shape,
        scratch_shapes=[pltpu.VMEM((L // GLA_Q, H * GLA_Q, DV), F32), pltpu.VMEM((2, DV, H * DK), F32)],
        compiler_params=_cparams(("arbitrary",)),
        name="gla_scan",
    )(*args)
    return outs[0], (outs[1] if emit_state else None)


def _merge_kernel(*refs, aliased):
    (x_ref, yssd_ref, z_ref, ona_ref, ogla_ref, g0_ref, g1_ref, g2_ref,
     gate1_ref, shift2_ref, scale2_ref, sng_ref, n2g_ref, wb_ref, wo_ref, rw_ref, rb_ref) = refs[:17]
    xo_ref, h2_ref, lg_ref = refs[-3:]
    y = yssd_ref[...] * _silu(z_ref[...].astype(F32))
    y = y * lax.rsqrt(jnp.mean(y * y, axis=-1, keepdims=True) + EPS) * sng_ref[...]
    m = jax.nn.sigmoid(g0_ref[...].astype(F32)) * _dot(y.astype(BF16), wb_ref[0])
    m = m + jax.nn.sigmoid(g1_ref[...].astype(F32)) * _dot(ona_ref[...], wb_ref[1])
    m = m + jax.nn.sigmoid(g2_ref[...].astype(F32)) * _dot(ogla_ref[...], wb_ref[2])
    x = x_ref[...] + gate1_ref[...] * _dot(m.astype(BF16), wo_ref[...])
    xo_ref[...] = x
    h2 = x * lax.rsqrt(jnp.mean(x * x, axis=-1, keepdims=True) + EPS) * n2g_ref[...]
    h2 = h2 * (1.0 + scale2_ref[...]) + shift2_ref[...]
    h2_ref[...] = h2
    lg_ref[...] = jnp.dot(h2, rw_ref[...], preferred_element_type=F32,
                          precision=lax.Precision.HIGHEST) + rb_ref[...]


def _merge(x2, y_ssd, pm, o_na, o_gla, mod_l, ssd_norm_g, norm2_g, wb_bf, wo_bf, rw_pad, rb_pad,
           rows_per_mod, mod_row0, t_all, row0, shared):
    T = x2.shape[0]
    tm = min(256, rows_per_mod)
    per = rows_per_mod // tm
    D = D_MODEL
    ng = M_GATES // D
    blk0 = row0 // tm

    def mod_map(col):
        return lambda i: (mod_row0 + i // per, 0, col)

    row = lambda i: (i, 0)
    row_off = lambda i: (blk0 + i, 0)
    const2 = lambda i: (0, 0)
    in_specs = [
        pl.BlockSpec((tm, D), row),
        pl.BlockSpec((tm, D), row),
        pl.BlockSpec((tm, D), lambda i: (i, M_Z // D)),
        pl.BlockSpec((tm, D), row),
        pl.BlockSpec((tm, D), row),
        pl.BlockSpec((tm, D), lambda i: (i, ng)),
        pl.BlockSpec((tm, D), lambda i: (i, ng + 1)),
        pl.BlockSpec((tm, D), lambda i: (i, ng + 2)),
        pl.BlockSpec((None, 1, D), mod_map(2)),
        pl.BlockSpec((None, 1, D), mod_map(3)),
        pl.BlockSpec((None, 1, D), mod_map(4)),
        pl.BlockSpec((1, D), const2),
        pl.BlockSpec((1, D), const2),
        pl.BlockSpec((N_BRANCH, D, D), lambda i: (0, 0, 0)),
        pl.BlockSpec((D, D), const2),
        pl.BlockSpec((D, 128), const2),
        pl.BlockSpec((1, 128), const2),
    ]
    args = [x2, y_ssd, pm, o_na, o_gla, pm, pm, pm, mod_l, mod_l, mod_l,
            ssd_norm_g.reshape(1, D), norm2_g.reshape(1, D), wb_bf, wo_bf, rw_pad, rb_pad]
    aliases = {}
    if shared is not None:
        in_specs += [pl.BlockSpec(memory_space=pl.ANY), pl.BlockSpec(memory_space=pl.ANY)]
        aliases = {len(args): 1, len(args) + 1: 2}
        args += list(shared)
    return pl.pallas_call(
        functools.partial(_merge_kernel, aliased=shared is not None),
        grid=(T // tm,),
        in_specs=in_specs,
        out_specs=[pl.BlockSpec((tm, D), row), pl.BlockSpec((tm, D), row_off), pl.BlockSpec((tm, 128), row_off)],
        out_shape=[jax.ShapeDtypeStruct((T, D), F32), jax.ShapeDtypeStruct((t_all, D), F32),
                   jax.ShapeDtypeStruct((t_all, 128), F32)],
        input_output_aliases=aliases,
        compiler_params=_cparams(("arbitrary",)),
        name="branch_merge",
    )(*args)


def _moe_kernel(be_ref, nused_ref, dst_ref, dstn_ref, h2_hbm,
                wgu_ref, bgu_ref, wdn_ref, bdn_ref, y_hbm, xbuf, obuf, wgu_bf, wdn_bf, gsem, ssem):
    b = pl.program_id(0)
    nused = nused_ref[0]
    slot = b % 2
    n_tok = h2_hbm.shape[0]

    def token_of(row):
        if n_tok & (n_tok - 1) == 0:
            return jnp.bitwise_and(row, n_tok - 1)
        return lax.rem(row, n_tok)

    def start_gather(idx_ref, s):
        def body(i, carry):
            pltpu.make_async_copy(h2_hbm.at[pl.ds(token_of(idx_ref[0, i]), 1), :], xbuf.at[s, pl.ds(i, 1), :],
                                  gsem.at[s]).start()
            return carry
        lax.fori_loop(0, MOE_ROWS, body, 0, unroll=MOE_DMA_UNROLL)

    def wait_gather(s):
        pltpu.make_async_copy(h2_hbm.at[pl.ds(0, MOE_ROWS), :], xbuf.at[s], gsem.at[s]).wait()

    def wait_scatter():
        pltpu.make_async_copy(obuf, y_hbm.at[pl.ds(0, MOE_ROWS), :], ssem.at[0]).wait()

    @pl.when(b == 0)
    def _():
        start_gather(dst_ref, 0)

    @pl.when(b < nused)
    def _():
        wait_gather(slot)

        prev = be_ref[jnp.maximum(b - 1, 0)]

        @pl.when(jnp.logical_or(b == 0, be_ref[b] != prev))
        def _():
            rc = 128

            def cast_body(i, carry):
                r0 = pl.multiple_of(i * rc, rc)
                wgu_bf[pl.ds(r0, rc), :] = wgu_ref[pl.ds(r0, rc), :].astype(BF16)
                wdn_bf[pl.ds(r0, rc), :] = wdn_ref[pl.ds(r0, rc), :].astype(BF16)
                return carry

            lax.fori_loop(0, D_MODEL // rc, cast_body, 0)

        for i in range(MOE_ROWS):
            pltpu.make_async_copy(h2_hbm.at[pl.ds(token_of(dstn_ref[0, i]), 1), :],
                                  xbuf.at[1 - slot, pl.ds(i, 1), :], gsem.at[1 - slot]).start()

        x = xbuf[slot].astype(BF16)
        acc = jnp.zeros((MOE_ROWS, D_MODEL), F32) + bdn_ref[...]
        for f in range(D_FF // MOE_FF_TILE):
            c0 = f * MOE_FF_TILE
            glu = _dot(x, wgu_bf[:, c0:c0 + MOE_FF_TILE]) + bgu_ref[:, c0:c0 + MOE_FF_TILE]
            lin = _dot(x, wgu_bf[:, D_FF + c0:D_FF + c0 + MOE_FF_TILE]) + bgu_ref[:, D_FF + c0:D_FF + c0 + MOE_FF_TILE]
            glu = jnp.minimum(glu, SWIGLU_LIMIT)
            lin = jnp.clip(lin, -SWIGLU_LIMIT, SWIGLU_LIMIT)
            act = glu * jax.nn.sigmoid(SWIGLU_ALPHA * glu) * (lin + 1.0)
            acc = acc + _dot(act.astype(BF16), wdn_bf[c0:c0 + MOE_FF_TILE, :])

        @pl.when(b > 0)
        def _():
            wait_scatter()

        obuf[...] = acc

        def sc_body(i, carry):
            pltpu.make_async_copy(obuf.at[pl.ds(i, 1), :], y_hbm.at[pl.ds(dst_ref[0, i], 1), :], ssem.at[0]).start()
            return carry

        lax.fori_loop(0, MOE_ROWS, sc_body, 0, unroll=MOE_DMA_UNROLL)

        @pl.when(b == nused - 1)
        def _():
            wait_scatter()
            wait_gather(1 - slot)


def _moe_experts(h2_all, slot_dst, block_e, n_used, w_gu, b_gu, w_dn, b_dn, layer):
    n_blocks = slot_dst.shape[0]
    T, D = h2_all.shape
    smem_blk = lambda f: pl.BlockSpec((None, 1, MOE_ROWS), f, memory_space=pltpu.SMEM)
    grid_spec = pltpu.PrefetchScalarGridSpec(
        num_scalar_prefetch=2,
        grid=(n_blocks,),
        in_specs=[
            smem_blk(lambda b, be, nu: (b, 0, 0)),
            smem_blk(lambda b, be, nu: (jnp.minimum(b + 1, n_blocks - 1), 0, 0)),
            pl.BlockSpec(memory_space=pl.ANY),
            pl.BlockSpec((None, None, D, 2 * D_FF), lambda b, be, nu: (layer, be[b], 0, 0)),
            pl.BlockSpec((None, None, 1, 2 * D_FF), lambda b, be, nu: (layer, be[b], 0, 0)),
            pl.BlockSpec((None, None, D_FF, D), lambda b, be, nu: (layer, be[b], 0, 0)),
            pl.BlockSpec((None, None, 1, D), lambda b, be, nu: (layer, be[b], 0, 0)),
        ],
        out_specs=pl.BlockSpec(memory_space=pl.ANY),
        scratch_shapes=[pltpu.VMEM((2, MOE_ROWS, D), F32), pltpu.VMEM((MOE_ROWS, D), F32),
                        pltpu.VMEM((D, 2 * D_FF), BF16), pltpu.VMEM((D_FF, D), BF16),
                        pltpu.SemaphoreType.DMA((2,)), pltpu.SemaphoreType.DMA((1,))],
    )
    return pl.pallas_call(
        _moe_kernel,
        grid_spec=grid_spec,
        out_shape=jax.ShapeDtypeStruct((TOP_K * T + MOE_ROWS, D), F32),
        compiler_params=_cparams(("arbitrary",)),
        name="moe_experts",
    )(block_e, n_used, slot_dst, slot_dst, h2_all, w_gu,
      b_gu.reshape(DEPTH, N_EXPERTS, 1, 2 * D_FF), w_dn, b_dn.reshape(DEPTH, N_EXPERTS, 1, D))


def _moe_route(logits):
    T = logits.shape[0]
    TK = T * TOP_K
    top_logit, top_idx = lax.top_k(logits, TOP_K)
    top_w = jax.nn.softmax(top_logit, axis=-1)
    flat_e = top_idx.reshape(TK).astype(jnp.int32)
    onehot = (flat_e[:, None] == jnp.arange(N_EXPERTS, dtype=jnp.int32)[None, :]).astype(jnp.int32)
    csum = jnp.cumsum(onehot, axis=0)
    rank = jnp.sum(onehot * csum, axis=1) - 1
    counts = csum[-1]
    padded = (counts + MOE_ROWS - 1) // MOE_ROWS * MOE_ROWS
    pad_end = jnp.cumsum(padded)
    pad_start = pad_end - padded
    dest = jnp.sum(onehot * pad_start[None, :], axis=1) + rank
    n_blocks = -(-(TK + N_EXPERTS * (MOE_ROWS - 1)) // MOE_ROWS)
    n_slots = n_blocks * MOE_ROWS
    flat = jnp.arange(TK, dtype=jnp.int32)
    spare = TK + jnp.arange(n_slots, dtype=jnp.int32) % MOE_ROWS
    slot_dst = spare.at[dest].set((flat % TOP_K) * T + flat // TOP_K)
    blk_start = jnp.arange(n_blocks, dtype=jnp.int32) * MOE_ROWS
    block_e = jnp.sum((blk_start[:, None] >= pad_end[None, :]).astype(jnp.int32), axis=1)
    block_e = jnp.minimum(block_e, N_EXPERTS - 1).astype(jnp.int32)
    n_used = (pad_end[-1] // MOE_ROWS).astype(jnp.int32).reshape(1)
    top_w_pad = jnp.pad(top_w, ((0, 0), (0, 128 - TOP_K)))
    return top_w_pad, slot_dst.reshape(n_blocks, 1, MOE_ROWS), block_e, n_used


def _combine_kernel(x_ref, y0_ref, y1_ref, y2_ref, y3_ref, w_ref, gate_ref, o_ref):
    w = w_ref[...]
    y = w[:, 0:1] * y0_ref[...]
    for k, y_ref in enumerate((y1_ref, y2_ref, y3_ref), start=1):
        y = y + w[:, k:k + 1] * y_ref[...]
    o_ref[...] = x_ref[...] + gate_ref[...] * y


def _moe_combine(x2, y_rows, top_w_pad, mod_l, rows_per_mod, mod_row0, row0, t_all):
    assert TOP_K == 4
    T, D = x2.shape
    tm = min(512, rows_per_mod)
    per = rows_per_mod // tm
    blk0 = row0 // tm
    nblk = t_all // tm

    def y_map(k):
        return lambda i: (k * nblk + blk0 + i, 0)

    return pl.pallas_call(
        _combine_kernel,
        grid=(T // tm,),
        in_specs=[pl.BlockSpec((tm, D), lambda i: (i, 0))]
        + [pl.BlockSpec((tm, D), y_map(k)) for k in range(TOP_K)]
        + [pl.BlockSpec((tm, 128), lambda i: (blk0 + i, 0)),
           pl.BlockSpec((None, 1, D), lambda i: (mod_row0 + i // per, 0, 5))],
        out_specs=pl.BlockSpec((tm, D), lambda i: (i, 0)),
        out_shape=jax.ShapeDtypeStruct((T, D), F32),
        compiler_params=_cparams(("arbitrary",)),
        name="moe_combine",
    )(x2, y_rows, y_rows, y_rows, y_rows, top_w_pad, mod_l)


def _final_norm_kernel(x_ref, g_ref, o_ref):
    x = x_ref[...]
    o_ref[...] = x * lax.rsqrt(jnp.mean(x * x, axis=-1, keepdims=True) + EPS) * g_ref[...]


def _final_norm(x2, g):
    T = x2.shape[0]
    tm = 512
    return pl.pallas_call(
        _final_norm_kernel,
        grid=(T // tm,),
        in_specs=[pl.BlockSpec((tm, D_MODEL), lambda i: (i, 0)), pl.BlockSpec((1, D_MODEL), lambda i: (0, 0))],
        out_specs=pl.BlockSpec((tm, D_MODEL), lambda i: (i, 0)),
        out_shape=jax.ShapeDtypeStruct((T, D_MODEL), F32),
        compiler_params=_cparams(("arbitrary",)),
        name="final_norm",
    )(x2, g.reshape(1, D_MODEL))


def _mixer_half(x2, B, L, mod_l, mod_row0, lw, ctx, t_all, row0, shared):
    latent = ctx is not None
    rows_per_mod = L if latent else B * L
    pm, ps = _in_projection(x2, mod_l, lw["norm1_g"], lw["w_in"], rows_per_mod, mod_row0)
    if latent:
        ck, cv, s_ssd0, s_gla0 = ctx
    else:
        s_ssd0 = s_gla0 = None
    y_ssd, s_ssd = _ssd_branch(pm, ps, lw["ssd_conv_w"], lw["ssd_conv_b"], lw["ssd_dt_bias"], lw["ssd_a_log"],
                               lw["ssd_d"], s_ssd0, B, L, emit_state=not latent)
    if latent:
        o_na = _neighbourhood_attention(pm, ck, cv, lw["na_bias"], B, L)
    else:
        o_na = _context_attention(pm, B, L)
    o_gla, s_gla = _gla_branch(pm, ps, lw["gla_w_gate"], lw["gla_b_gate"], lw["gla_norm_g"], s_gla0, B, L,
                               emit_state=not latent)
    x_new, h2, logits = _merge(x2, y_ssd, pm, o_na, o_gla, mod_l, lw["ssd_norm_g"], lw["norm2_g"],
                               lw["w_branch"], lw["w_out"], lw["router_w"], lw["router_b"], rows_per_mod, mod_row0,
                               t_all, row0, shared)
    return x_new, h2, logits, pm, s_ssd, s_gla


def kernel(x_prompt, x_sample, cache_na_k, cache_na_v, state_ssd, state_gla, c, c_ctx, w_ada, b_ada, norm1_g, norm2_g, w_in, ssd_conv_w, ssd_conv_b, ssd_dt_bias, ssd_a_log, ssd_d, ssd_norm_g, na_rpb, gla_w_gate, gla_b_gate, gla_norm_g, w_branch, w_out, router_w, router_b, moe_w_gu, moe_b_gu, moe_w_dn, moe_b_dn, final_norm_g):
    Bp, Lp, D = x_prompt.shape
    Bs, Ls, _ = x_sample.shape
    Tp, Ts = Bp * Lp, Bs * Ls
    Lc = cache_na_k.shape[2]

    cvecs = jnp.concatenate([c_ctx[None], c, jnp.zeros((8 - 1 - Bs, D), F32)], axis=0)
    mod = _modulation(cvecs, w_ada, b_ada).reshape(DEPTH, 8, 1, 6 * D)

    xp = x_prompt.reshape(Tp, D)
    xs = x_sample.reshape(Ts, D)
    ks_, vs_, sss_, sgs_ = [], [], [], []
    for l in range(DEPTH):
        lw = {
            "norm1_g": norm1_g[l], "norm2_g": norm2_g[l], "w_in": _relayout_w_in(w_in[l]),
            "ssd_conv_w": ssd_conv_w[l], "ssd_conv_b": ssd_conv_b[l], "ssd_dt_bias": ssd_dt_bias[l],
            "ssd_a_log": ssd_a_log[l], "ssd_d": ssd_d[l], "ssd_norm_g": ssd_norm_g[l],
            "na_bias": _na_bias_table(na_rpb[l]),
            "gla_w_gate": gla_w_gate[l], "gla_b_gate": gla_b_gate[l], "gla_norm_g": gla_norm_g[l],
            "w_branch": w_branch[l].astype(BF16), "w_out": w_out[l].astype(BF16),
            "router_w": jnp.pad(router_w[l], ((0, 0), (0, 128 - N_EXPERTS))),
            "router_b": jnp.pad(router_b[l], (0, 128 - N_EXPERTS)).reshape(1, 128),
        }
        mod_l = mod[l]
        t_all = Tp + Ts
        xp, h2a, lga, pmp, s_ssd, s_gla = _mixer_half(xp, Bp, Lp, mod_l, 0, lw, None, t_all, 0, None)
        ctx = (cache_na_k[:, l].reshape(Bs, Lc, NA_INNER), cache_na_v[:, l].reshape(Bs, Lc, NA_INNER),
               state_ssd[:, l], state_gla[:, l])
        xs, h2a, lga, _, _, _ = _mixer_half(xs, Bs, Ls, mod_l, 1, lw, ctx, t_all, Tp, (h2a, lga))
        ks_.append(pmp[:, M_K:M_K + NA_INNER].astype(F32).reshape(Bp, Lp, NA_HEADS, NA_HEAD_DIM))
        vs_.append(pmp[:, M_V:M_V + NA_INNER].astype(F32).reshape(Bp, Lp, NA_HEADS, NA_HEAD_DIM))
        sss_.append(s_ssd)
        sgs_.append(s_gla)

        top_w, slot_dst, block_e, n_used = _moe_route(lga[:, :N_EXPERTS])
        y_rows = _moe_experts(h2a, slot_dst, block_e, n_used, moe_w_gu, moe_b_gu, moe_w_dn, moe_b_dn, l)
        xp = _moe_combine(xp, y_rows, top_w, mod_l, Tp, 0, 0, t_all)
        xs = _moe_combine(xs, y_rows, top_w, mod_l, Ls, 1, Tp, t_all)

    y_prompt = _final_norm(xp, final_norm_g).reshape(Bp, Lp, D)
    y_sample = _final_norm(xs, final_norm_g).reshape(Bs, Ls, D)
    return (y_prompt, y_sample, jnp.stack(ks_, axis=1), jnp.stack(vs_, axis=1),
            jnp.stack(sss_, axis=1), jnp.stack(sgs_, axis=1))
```

```python
import functools

import numpy as np
import jax
import jax.numpy as jnp
from jax import lax
from jax.experimental import pallas as pl
from jax.experimental.pallas import tpu as pltpu

F32 = jnp.float32
BF16 = jnp.bfloat16

D_MODEL = 1024
DEPTH = 4
GRID_W = 64
EPS = 1e-6

SSD_HEADS = 16
SSD_HEAD_DIM = 64
SSD_INNER = SSD_HEADS * SSD_HEAD_DIM
SSD_GROUPS = 2
SSD_STATE = 128
SSD_GN = SSD_GROUPS * SSD_STATE
SSD_CONV_W = 5
SSD_CONV_DIM = SSD_INNER + 2 * SSD_GN
SSD_Q = 64
SSD_GW = SSD_INNER // SSD_GROUPS

NA_HEADS = 16
NA_HEAD_DIM = 64
NA_INNER = NA_HEADS * NA_HEAD_DIM
NA_WIN_ROWS = 8
NA_WIN_COLS = 16
NA_SCALE = NA_HEAD_DIM ** -0.5
NA_NEG = -1e30

GLA_HEADS = 4
GLA_DK = D_MODEL // 2
GLA_DV = D_MODEL
GLA_DK_HEAD = GLA_DK // GLA_HEADS
GLA_DV_HEAD = GLA_DV // GLA_HEADS
GLA_LOWRANK = 16
GLA_TAU = 16.0
GLA_Q = 32
GLA_QK_SCALE = GLA_DK_HEAD ** -0.5

N_BRANCH = 3
N_EXPERTS = 32
TOP_K = 4
D_FF = D_MODEL
SWIGLU_ALPHA = 1.702
SWIGLU_LIMIT = 7.0
MOE_ROWS = 512
MOE_FF_TILE = 512
MOE_DMA_UNROLL = 8

SSD_UNROLL = 2
NA_UNROLL = 4
GLA_SUPER = 4

OFF_SSD_XBC = SSD_INNER
OFF_SSD_DT = OFF_SSD_XBC + SSD_CONV_DIM
OFF_NA_QKV = OFF_SSD_DT + 2 * SSD_HEADS
OFF_GLA_Q = OFF_NA_QKV + 3 * NA_INNER
OFF_GLA_LR = OFF_GLA_Q + 2 * GLA_DK + 2 * GLA_DV
OFF_GATES = OFF_GLA_LR + 2 * GLA_LOWRANK
IN_COLS = OFF_GATES + N_BRANCH * D_MODEL

M_Z = 0
M_X = M_Z + SSD_INNER
M_Q = M_X + SSD_INNER
M_K = M_Q + NA_INNER
M_V = M_K + NA_INNER
M_GV = M_V + NA_INNER
M_R = M_GV + GLA_DV
M_GATES = M_R + GLA_DV
M_GQ = M_GATES + N_BRANCH * D_MODEL
M_GK = M_GQ + GLA_DK
M_B = M_GK + GLA_DK
M_C = M_B + SSD_GN
M_COLS = M_C + SSD_GN
_MAIN_SEGMENTS = (
    (0, 2 * SSD_INNER),
    (OFF_NA_QKV, 3 * NA_INNER),
    (OFF_GLA_Q + 2 * GLA_DK, 2 * GLA_DV),
    (OFF_GATES, N_BRANCH * D_MODEL),
    (OFF_GLA_Q, 2 * GLA_DK),
    (OFF_SSD_XBC + SSD_INNER, 2 * SSD_GN),
)
_SMALL_SEGMENTS = ((OFF_SSD_DT, 2 * SSD_HEADS), (OFF_GLA_LR, 2 * GLA_LOWRANK))
S_DT = 0
S_LR = 2 * SSD_HEADS
S_COLS = 128
PROJ_TN = 512
PROJ_COLS = M_COLS + PROJ_TN
PROJ_TM = 2048

VMEM_LIMIT = 56 * 1024 * 1024


def _cparams(sem):
    return pltpu.CompilerParams(dimension_semantics=sem, vmem_limit_bytes=VMEM_LIMIT)


def _silu(x):
    return x * jax.nn.sigmoid(x)


def _softplus(x):
    return jnp.maximum(x, 0.0) + jnp.log1p(jnp.exp(-jnp.abs(x)))


def _split3(x):
    hi = x.astype(BF16)
    r1 = x - hi.astype(F32)
    mid = r1.astype(BF16)
    lo = (r1 - mid.astype(F32)).astype(BF16)
    return hi, mid, lo


def _dot(a, b):
    return jnp.dot(a, b, preferred_element_type=F32)


def _dot_nt(a, b):
    return lax.dot_general(a, b, (((1,), (1,)), ((), ())), preferred_element_type=F32)


def _dot_tn(a, b):
    return lax.dot_general(a, b, (((0,), (0,)), ((), ())), preferred_element_type=F32)


def _sel_dot_l(sel_bf, x):
    hi, mid, lo = _split3(x)
    return (_dot(sel_bf, lo) + _dot(sel_bf, mid)) + _dot(sel_bf, hi)


def _sel_dot_r(x, sel_bf):
    hi, mid, lo = _split3(x)
    return (_dot(lo, sel_bf) + _dot(mid, sel_bf)) + _dot(hi, sel_bf)


def _mod_kernel(c_ref, w_ref, b_ref, o_ref):
    c = c_ref[...]
    o_ref[...] = jnp.dot(_silu(c), w_ref[...], preferred_element_type=F32,
                         precision=lax.Precision.HIGHEST) + b_ref[...]


def _modulation(cvecs, w_ada, b_ada):
    nrow = cvecs.shape[0]
    return pl.pallas_call(
        _mod_kernel,
        grid=(DEPTH, 6),
        in_specs=[
            pl.BlockSpec((nrow, D_MODEL), lambda l, j: (0, 0)),
            pl.BlockSpec((None, D_MODEL, D_MODEL), lambda l, j: (l, 0, j)),
            pl.BlockSpec((None, 1, D_MODEL), lambda l, j: (l, 0, j)),
        ],
        out_specs=pl.BlockSpec((None, nrow, D_MODEL), lambda l, j: (l, 0, j)),
        out_shape=jax.ShapeDtypeStruct((DEPTH, nrow, 6 * D_MODEL), F32),
        compiler_params=_cparams(("arbitrary", "arbitrary")),
        name="adaln_mod",
    )(cvecs, w_ada, b_ada.reshape(DEPTH, 1, 6 * D_MODEL))


def _inproj_kernel(x_ref, g_ref, shift_ref, scale_ref, w_ref, om_ref, os_ref, h_ref, *, n_main):
    j = pl.program_id(1)

    @pl.when(j == 0)
    def _():
        x = x_ref[...]
        h = x * lax.rsqrt(jnp.mean(x * x, axis=-1, keepdims=True) + EPS) * g_ref[...]
        h = h * (1.0 + scale_ref[...]) + shift_ref[...]
        h_ref[...] = h.astype(BF16)

    acc = _dot(h_ref[...], w_ref[...])

    @pl.when(j < n_main)
    def _():
        om_ref[...] = acc.astype(BF16)

    @pl.when(j == n_main)
    def _():
        os_ref[...] = acc[:, :S_COLS]


def _relayout_w_in(w):
    parts = [w[:, s:s + n] for s, n in _MAIN_SEGMENTS + _SMALL_SEGMENTS]
    parts.append(jnp.zeros((w.shape[0], PROJ_TN - 2 * SSD_HEADS - 2 * GLA_LOWRANK), w.dtype))
    return jnp.concatenate(parts, axis=1).astype(BF16)


def _in_projection(x2, mod_l, norm_g, w_perm, rows_per_mod, mod_row0):
    T = x2.shape[0]
    tm = min(PROJ_TM, rows_per_mod)
    n_main = M_COLS // PROJ_TN
    per = rows_per_mod // tm

    def mod_map(col):
        return lambda i, j: (mod_row0 + i // per, 0, col)

    return pl.pallas_call(
        functools.partial(_inproj_kernel, n_main=n_main),
        grid=(T // tm, n_main + 1),
        in_specs=[
            pl.BlockSpec((tm, D_MODEL), lambda i, j: (i, 0)),
            pl.BlockSpec((1, D_MODEL), lambda i, j: (0, 0)),
            pl.BlockSpec((None, 1, D_MODEL), mod_map(0)),
            pl.BlockSpec((None, 1, D_MODEL), mod_map(1)),
            pl.BlockSpec((D_MODEL, PROJ_TN), lambda i, j: (0, j)),
        ],
        out_specs=[
            pl.BlockSpec((tm, PROJ_TN), lambda i, j: (i, jnp.minimum(j, n_main - 1))),
            pl.BlockSpec((tm, S_COLS), lambda i, j: (i, 0)),
        ],
        out_shape=[jax.ShapeDtypeStruct((T, M_COLS), BF16), jax.ShapeDtypeStruct((T, S_COLS), F32)],
        scratch_shapes=[pltpu.VMEM((tm, D_MODEL), BF16)],
        compiler_params=_cparams(("arbitrary", "arbitrary")),
        name="in_projection",
    )(x2, norm_g.reshape(1, D_MODEL), mod_l, mod_l, w_perm)


_CONV_WIN = 128
_CONV_LEAD = 64
_CONV_OFF = 48


def _ssd_kernel(*refs, L, use_s0, emit_state):
    it = iter(refs)
    x_ref, b_ref, c_ref, ps_ref = next(it), next(it), next(it), next(it)
    cwx_ref, cwb_ref, cwc_ref = next(it), next(it), next(it)
    cbx_ref, cbb_ref, cbc_ref = next(it), next(it), next(it)
    dtb_ref, alog_ref, dsk_ref = next(it), next(it), next(it)
    s0_ref = next(it) if use_s0 else None
    y_ref = next(it)
    so_ref = next(it) if emit_state else None
    xpad, bpad, cpad, xc, bc, cc, dts, st = (next(it) for _ in range(8))

    Q = SSD_Q
    nc = L // Q
    g = pl.program_id(1)
    pad = SSD_CONV_W // 2

    for src, dst in ((x_ref, xpad), (b_ref, bpad), (c_ref, cpad)):
        w = dst.shape[1]
        dst[0:_CONV_LEAD, :] = jnp.zeros((_CONV_LEAD, w), BF16)
        dst[_CONV_LEAD + L:_CONV_LEAD + L + _CONV_LEAD, :] = jnp.zeros((_CONV_LEAD, w), BF16)

    def copy_body(c, carry):
        r0 = pl.multiple_of(c * Q, Q)
        for src, dst in ((x_ref, xpad), (b_ref, bpad), (c_ref, cpad)):
            dst[pl.ds(r0 + _CONV_LEAD, Q), :] = src[pl.ds(r0, Q), :]
        return carry

    lax.fori_loop(0, nc, copy_body, 0)

    ri = lax.broadcasted_iota(jnp.int32, (SSD_CONV_W * Q, _CONV_WIN), 0)
    ci = lax.broadcasted_iota(jnp.int32, (SSD_CONV_W * Q, _CONV_WIN), 1)
    shift_sel = (ci == (ri % Q) + (ri // Q) + (_CONV_LEAD - _CONV_OFF - pad)).astype(BF16)

    def conv_chunk(pad_ref, w_ref, bias_ref, r0):
        win = pad_ref[pl.ds(r0 + _CONV_OFF, _CONV_WIN), :]
        sh = _dot(shift_sel, win)
        acc = bias_ref[...] + w_ref[0:1, :] * sh[0:Q]
        for k in range(1, SSD_CONV_W):
            acc = acc + w_ref[k:k + 1, :] * sh[k * Q:(k + 1) * Q]
        return _silu(acc)

    def prep_body(c, carry):
        r0 = pl.multiple_of(c * Q, Q)
        xv = conv_chunk(xpad, cwx_ref, cbx_ref, r0)
        xc[pl.ds(r0, Q), :] = xv
        bc[pl.ds(r0, Q), :] = conv_chunk(bpad, cwb_ref, cbb_ref, r0)
        cc[pl.ds(r0, Q), :] = conv_chunk(cpad, cwc_ref, cbc_ref, r0)
        dts[pl.ds(r0, Q), :] = _softplus(ps_ref[pl.ds(r0, Q), :] + dtb_ref[...])
        y_ref[pl.ds(r0, Q), :] = dsk_ref[...] * xv
        return carry

    lax.fori_loop(0, nc, prep_body, 0)

    GW = SSD_GW
    hpg = SSD_HEADS // SSD_GROUPS
    er = lax.broadcasted_iota(jnp.int32, (S_COLS, GW), 0)
    ec = lax.broadcasted_iota(jnp.int32, (S_COLS, GW), 1)
    qi = lax.broadcasted_iota(jnp.int32, (Q, GW), 0)
    qj = lax.broadcasted_iota(jnp.int32, (Q, GW), 1) % Q
    diag_sel = (qi == qj).astype(F32)
    ones_q = jnp.ones((Q, Q), BF16)
    ti = lax.broadcasted_iota(jnp.int32, (Q, Q), 0)
    tj = lax.broadcasted_iota(jnp.int32, (Q, Q), 1)
    lane = lax.broadcasted_iota(jnp.int32, (Q, 128), 1)
    lo_half = lane < SSD_HEAD_DIM
    a_all = -jnp.exp(alog_ref[...])

    per_dir = []
    for d in range(2):
        base = S_DT + d * SSD_HEADS + g * hpg
        esel = (er == base + ec // SSD_HEAD_DIM).astype(BF16)
        a_exp = _sel_dot_r(jnp.broadcast_to(a_all, (8, S_COLS)), esel)[0:1]
        if d == 0:
            tri = (tj <= ti).astype(BF16)
            mask = qj <= qi
        else:
            tri = (tj >= ti).astype(BF16)
            mask = qj >= qi
        per_dir.append((esel, a_exp, tri, mask))

    for d in range(2):
        if use_s0:
            st[d] = s0_ref[d].T
        else:
            st[d] = jnp.zeros((SSD_STATE, GW), F32)

    def chunk_dir(c, d):
        esel, a_exp, tri, mask = per_dir[d]
        r0 = pl.multiple_of(c * Q, Q)
        dt_exp = _sel_dot_r(dts[pl.ds(r0, Q), :], esel)
        cum = _sel_dot_l(tri, dt_exp * a_exp)
        rowb = _sel_dot_l(ones_q, cum * diag_sel)
        lmat = jnp.exp(jnp.where(mask, cum - rowb, -jnp.inf))
        bq = bc[pl.ds(r0, Q), :].astype(BF16)
        cq = cc[pl.ds(r0, Q), :].astype(BF16)
        cb = _dot_nt(cq, jnp.concatenate([bq] * (GW // Q), axis=0))
        amat = (cb * lmat).astype(BF16)
        xq = xc[pl.ds(r0, Q), :] * dt_exp
        parts = []
        for p in range(GW // 128):
            xp = xq[:, p * 128:(p + 1) * 128]
            xbd = jnp.concatenate([jnp.where(lo_half, xp, 0.0), jnp.where(lo_half, 0.0, xp)], axis=0)
            parts.append(_dot(amat[:, p * 128:(p + 1) * 128], xbd.astype(BF16)))
        y_intra = jnp.concatenate(parts, axis=1)
        s_t = st[d]
        y_inter = _dot(cq, s_t.astype(BF16)) * jnp.exp(cum)
        cum_last = cum[Q - 1:Q] if d == 0 else cum[0:1]
        xdec = (xq * jnp.exp(cum_last - cum)).astype(BF16)
        st[d] = s_t * jnp.exp(cum_last) + _dot_tn(bq, xdec)
        y_ref[pl.ds(r0, Q), :] += y_intra + y_inter

    def scan_body(i, carry):
        chunk_dir(i, 0)
        chunk_dir(nc - 1 - i, 1)
        return carry

    lax.fori_loop(0, nc, scan_body, 0, unroll=SSD_UNROLL)

    if emit_state:
        for d in range(2):
            so_ref[d] = st[d].T


def _ssd_branch(pm, ps, conv_w, conv_b, dt_bias, a_log, d_skip, s0, B, L, emit_state):
    use_s0 = s0 is not None
    G, GW = SSD_GROUPS, SSD_GW
    hpg = SSD_HEADS // G
    dtb = jnp.zeros((1, S_COLS), F32).at[0, S_DT:S_DT + 2 * SSD_HEADS].set(dt_bias.reshape(-1))
    alog = jnp.zeros((1, S_COLS), F32).at[0, S_DT:S_DT + 2 * SSD_HEADS].set(a_log.reshape(-1))
    dsk = jnp.repeat(d_skip, SSD_HEAD_DIM).reshape(1, SSD_INNER)
    cb2 = conv_b.reshape(1, SSD_CONV_DIM)
    nb_x = M_X // GW
    in_specs = [
        pl.BlockSpec((L, GW), lambda b, g: (b, nb_x + g)),
        pl.BlockSpec((L, SSD_STATE), lambda b, g: (b, M_B // SSD_STATE + g)),
        pl.BlockSpec((L, SSD_STATE), lambda b, g: (b, M_C // SSD_STATE + g)),
        pl.BlockSpec((L, S_COLS), lambda b, g: (b, 0)),
        pl.BlockSpec((SSD_CONV_W, GW), lambda b, g: (0, g)),
        pl.BlockSpec((SSD_CONV_W, SSD_STATE), lambda b, g: (0, SSD_INNER // SSD_STATE + g)),
        pl.BlockSpec((SSD_CONV_W, SSD_STATE), lambda b, g: (0, (SSD_INNER + SSD_GN) // SSD_STATE + g)),
        pl.BlockSpec((1, GW), lambda b, g: (0, g)),
        pl.BlockSpec((1, SSD_STATE), lambda b, g: (0, SSD_INNER // SSD_STATE + g)),
        pl.BlockSpec((1, SSD_STATE), lambda b, g: (0, (SSD_INNER + SSD_GN) // SSD_STATE + g)),
        pl.BlockSpec((1, S_COLS), lambda b, g: (0, 0)),
        pl.BlockSpec((1, S_COLS), lambda b, g: (0, 0)),
        pl.BlockSpec((1, GW), lambda b, g: (0, g)),
    ]
    args = [pm, pm, pm, ps, conv_w, conv_w, conv_w, cb2, cb2, cb2, dtb, alog, dsk]
    state_spec = pl.BlockSpec((None, 2, None, GW, SSD_STATE), lambda b, g: (b, 0, g, 0, 0))
    if use_s0:
        in_specs.append(state_spec)
        args.append(s0.reshape(B, 2, G, GW, SSD_STATE))
    out_specs = [pl.BlockSpec((L, GW), lambda b, g: (b, g))]
    out_shape = [jax.ShapeDtypeStruct((B * L, SSD_INNER), F32)]
    if emit_state:
        out_specs.append(state_spec)
        out_shape.append(jax.ShapeDtypeStruct((B, 2, G, GW, SSD_STATE), F32))
    plen = L + 2 * _CONV_LEAD
    outs = pl.pallas_call(
        functools.partial(_ssd_kernel, L=L, use_s0=use_s0, emit_state=emit_state),
        grid=(B, G),
        in_specs=in_specs,
        out_specs=out_specs,
        out_shape=out_shape,
        scratch_shapes=[
            pltpu.VMEM((plen, GW), BF16), pltpu.VMEM((plen, SSD_STATE), BF16), pltpu.VMEM((plen, SSD_STATE), BF16),
            pltpu.VMEM((L, GW), F32), pltpu.VMEM((L, SSD_STATE), F32), pltpu.VMEM((L, SSD_STATE), F32),
            pltpu.VMEM((L, S_COLS), F32), pltpu.VMEM((2, SSD_STATE, GW), F32),
        ],
        compiler_params=_cparams(("arbitrary", "arbitrary")),
        name="ssd_scan",
    )(*args)
    y = outs[0]
    state = outs[1].reshape(B, 2, SSD_HEADS, SSD_HEAD_DIM, SSD_STATE) if emit_state else None
    return y, state


def _ctx_attn_kernel(q_ref, k_ref, v_ref, o_ref):
    q = q_ref[...]
    L = q.shape[0]
    lane = lax.broadcasted_iota(jnp.int32, q.shape, 1)
    lo = lane < NA_HEAD_DIM
    zero = jnp.zeros_like(q)
    q2 = jnp.concatenate([jnp.where(lo, q, zero), jnp.where(lo, zero, q)], axis=0)
    s = _dot_nt(q2, k_ref[...]) * NA_SCALE
    m = jnp.max(s, axis=-1, keepdims=True)
    p = jnp.exp(s - m)
    den = jnp.sum(p, axis=-1, keepdims=True)
    o2 = _dot(p.astype(BF16), v_ref[...]) / den
    o_ref[...] = jnp.where(lo, o2[:L], o2[L:]).astype(o_ref.dtype)


def _context_attention(pm, B, L):
    nq, nk, nv = M_Q // 128, M_K // 128, M_V // 128
    return pl.pallas_call(
        _ctx_attn_kernel,
        grid=(B, NA_HEADS // 2),
        in_specs=[
            pl.BlockSpec((L, 128), lambda b, h: (b, nq + h)),
            pl.BlockSpec((L, 128), lambda b, h: (b, nk + h)),
            pl.BlockSpec((L, 128), lambda b, h: (b, nv + h)),
        ],
        out_specs=pl.BlockSpec((L, 128), lambda b, h: (b, h)),
        out_shape=jax.ShapeDtypeStruct((B * L, NA_INNER), BF16),
        compiler_params=_cparams(("arbitrary", "arbitrary")),
        name="context_attention",
    )(pm, pm, pm)


def _na_bias_table(rpb):
    qc = np.arange(GRID_W)
    cstart = np.clip(qc - NA_WIN_COLS // 2, 0, GRID_W - NA_WIN_COLS)
    kc = np.arange(GRID_W)
    valid = (kc[None, :] >= cstart[:, None]) & (kc[None, :] < cstart[:, None] + NA_WIN_COLS)
    dx = np.clip(kc[None, :] - qc[:, None] + NA_WIN_COLS - 1, 0, 2 * NA_WIN_COLS - 2)
    H, n_dr, n_dx = rpb.shape
    pick = (dx[None, :, :] == np.arange(n_dx)[:, None, None]).astype(np.float32).reshape(n_dx, GRID_W * GRID_W)
    cols = jnp.dot(rpb.reshape(H * n_dr, n_dx), jnp.asarray(pick), precision=lax.Precision.HIGHEST)
    cols = cols.reshape(H, n_dr, GRID_W, GRID_W)
    t = jnp.stack([cols[:, d0:d0 + NA_WIN_ROWS] for d0 in range(NA_WIN_ROWS)], axis=1)
    t = t.transpose(0, 1, 3, 2, 4)
    t = jnp.where(jnp.asarray(valid)[None, None, :, None, :], t, NA_NEG)
    t = t.reshape(H // 2, 2, NA_WIN_ROWS, GRID_W, NA_WIN_ROWS * GRID_W).astype(F32)
    return t.transpose(0, 2, 1, 3, 4).reshape(H // 2, NA_WIN_ROWS, 2 * GRID_W, NA_WIN_ROWS * GRID_W)


def _na_kernel(q_ref, k_ref, v_ref, ck_ref, cv_ref, bias_ref, o_ref, ckb, cvb, *, rows):
    ckb[...] = ck_ref[...].astype(BF16)
    cvb[...] = cv_ref[...].astype(BF16)
    wr = NA_WIN_ROWS
    nloc = wr * GRID_W
    lo = lax.broadcasted_iota(jnp.int32, (GRID_W, 128), 1) < NA_HEAD_DIM

    def row_body(r, carry):
        rs = jnp.clip(r - wr // 2, 0, rows - wr)
        d0 = rs - r + wr - 1
        q = q_ref[pl.ds(pl.multiple_of(r * GRID_W, GRID_W), GRID_W), :]
        k0 = pl.multiple_of(rs * GRID_W, GRID_W)
        kw = k_ref[pl.ds(k0, nloc), :]
        vw = v_ref[pl.ds(k0, nloc), :]
        zero = jnp.zeros_like(q)
        q2 = jnp.concatenate([jnp.where(lo, q, zero), jnp.where(lo, zero, q)], axis=0)
        s_loc = _dot_nt(q2, kw) * NA_SCALE + bias_ref[d0]
        s_ctx = _dot_nt(q2, ckb[...]) * NA_SCALE
        m = jnp.maximum(jnp.max(s_loc, axis=-1, keepdims=True), jnp.max(s_ctx, axis=-1, keepdims=True))
        p_loc = jnp.exp(s_loc - m)
        p_ctx = jnp.exp(s_ctx - m)
        den = jnp.sum(p_loc, axis=-1, keepdims=True) + jnp.sum(p_ctx, axis=-1, keepdims=True)
        o2 = (_dot(p_loc.astype(BF16), vw) + _dot(p_ctx.astype(BF16), cvb[...])) / den
        o = jnp.where(lo, o2[:GRID_W], o2[GRID_W:])
        o_ref[pl.ds(pl.multiple_of(r * GRID_W, GRID_W), GRID_W), :] = o.astype(o_ref.dtype)
        return carry

    lax.fori_loop(0, rows, row_body, 0, unroll=NA_UNROLL)


def _neighbourhood_attention(pm, ck, cv, bias_tab, B, S):
    rows = S // GRID_W
    assert rows >= NA_WIN_ROWS
    Lc = ck.shape[1]
    nq, nk, nv = M_Q // 128, M_K // 128, M_V // 128
    return pl.pallas_call(
        functools.partial(_na_kernel, rows=rows),
        grid=(NA_HEADS // 2, B),
        in_specs=[
            pl.BlockSpec((S, 128), lambda h, b: (b, nq + h)),
            pl.BlockSpec((S, 128), lambda h, b: (b, nk + h)),
            pl.BlockSpec((S, 128), lambda h, b: (b, nv + h)),
            pl.BlockSpec((None, Lc, 128), lambda h, b: (b, 0, h)),
            pl.BlockSpec((None, Lc, 128), lambda h, b: (b, 0, h)),
            pl.BlockSpec((None, NA_WIN_ROWS, 2 * GRID_W, NA_WIN_ROWS * GRID_W), lambda h, b: (h, 0, 0, 0)),
        ],
        out_specs=pl.BlockSpec((S, 128), lambda h, b: (b, h)),
        out_shape=jax.ShapeDtypeStruct((B * S, NA_INNER), BF16),
        scratch_shapes=[pltpu.VMEM((Lc, 128), BF16), pltpu.VMEM((Lc, 128), BF16)],
        compiler_params=_cparams(("arbitrary", "arbitrary")),
        name="neighbourhood_attention",
    )(pm, pm, pm, ck, cv, bias_tab)


def _gla_kernel(*refs, L, use_s0, emit_state):
    it = iter(refs)
    q_ref, k_ref, v_ref, r_ref, ps_ref, wg_ref, bg_ref, ng_ref = (next(it) for _ in range(8))
    s0_ref = next(it) if use_s0 else None
    o_ref = next(it)
    so_ref = next(it) if emit_state else None
    acc, st = next(it), next(it)

    Q, H, DK, DV = GLA_Q, GLA_HEADS, GLA_DK_HEAD, GLA_DV_HEAD
    HQ = H * Q
    nc = L // Q

    def zero_body(c, carry):
        acc[c] = jnp.zeros((HQ, DV), F32)
        return carry

    lax.fori_loop(0, nc, zero_body, 0)

    for d in range(2):
        for h in range(H):
            if use_s0:
                st[d, :, h * DK:(h + 1) * DK] = s0_ref[d, h].T
            else:
                st[d, :, h * DK:(h + 1) * DK] = jnp.zeros((DV, DK), F32)

    SC = GLA_SUPER
    G = SC * Q
    R = SC * HQ
    nsc = L // G
    gi = lax.broadcasted_iota(jnp.int32, (G, G), 0)
    gj = lax.broadcasted_iota(jnp.int32, (G, G), 1)
    same_chunk = (gi // Q) == (gj // Q)
    tris = (jnp.logical_and(same_chunk, gj <= gi).astype(BF16), jnp.logical_and(same_chunk, gj >= gi).astype(BF16))
    ends = ((gj == (gi // Q) * Q + (Q - 1)).astype(BF16), (gj == (gi // Q) * Q).astype(BF16))
    row_k = lax.broadcasted_iota(jnp.int32, (R, H * DK), 0)
    lane_k = lax.broadcasted_iota(jnp.int32, (R, H * DK), 1)
    own_k = ((row_k // Q) % H) == (lane_k // DK)
    row_s = lax.broadcasted_iota(jnp.int32, (R, G), 0)
    col_s = lax.broadcasted_iota(jnp.int32, (R, G), 1)
    in_chunk = (row_s // HQ) == (col_s // Q)
    causal = (jnp.logical_and(in_chunk, (col_s % Q) <= (row_s % Q)),
              jnp.logical_and(in_chunk, (col_s % Q) >= (row_s % Q)))
    rep_r = lax.broadcasted_iota(jnp.int32, (G, R), 0)
    rep_c = lax.broadcasted_iota(jnp.int32, (G, R), 1)
    rep = jnp.logical_and(rep_r // Q == rep_c // HQ, rep_r % Q == rep_c % Q).astype(BF16)
    row_a = lax.broadcasted_iota(jnp.int32, (R, R), 0)
    lane_a = lax.broadcasted_iota(jnp.int32, (R, R), 1)
    own_a = (row_a // Q) == (lane_a // Q)

    def stack_heads(x):
        return jnp.concatenate([x[c * Q:(c + 1) * Q] for c in range(SC) for _ in range(H)], axis=0)

    def super_dir(u, d):
        r0 = pl.multiple_of(u * G, G)
        lr = ps_ref[pl.ds(r0, G), :].astype(BF16)
        gpre = (_dot(lr, wg_ref[:, d * GLA_DK:(d + 1) * GLA_DK].astype(BF16))
                + bg_ref[:, d * GLA_DK:(d + 1) * GLA_DK])
        g = (jnp.minimum(gpre, 0.0) - jnp.log1p(jnp.exp(-jnp.abs(gpre)))) / GLA_TAU
        bcum = _sel_dot_l(tris[d], g)
        bend = _sel_dot_l(ends[d], bcum)
        qc = q_ref[pl.ds(r0, G), :].astype(F32) * GLA_QK_SCALE
        kc = k_ref[pl.ds(r0, G), :].astype(F32)
        vc = v_ref[pl.ds(r0, G), :]
        qt = (qc * jnp.exp(bcum)).astype(BF16)
        kt = (kc * jnp.exp(-bcum)).astype(BF16)
        kend = (kc * jnp.exp(bend - bcum)).astype(BF16)
        zk = jnp.zeros((R, H * DK), BF16)
        q16 = jnp.where(own_k, stack_heads(qt), zk)
        k16 = jnp.where(own_k, stack_heads(kend), zk)
        v16 = jnp.concatenate([vc[c * Q:(c + 1) * Q, h * DV:(h + 1) * DV] for c in range(SC) for h in range(H)],
                              axis=0)
        att = jnp.where(causal[d], _dot_nt(q16, kt), 0.0)
        a16 = jnp.where(own_a, _dot(att.astype(BF16), rep), 0.0).astype(BF16)
        o_intra = _dot(a16, v16)
        decay = jnp.exp(bend)
        order = range(SC) if d == 0 else range(SC - 1, -1, -1)
        for c in order:
            q4 = q16[c * HQ:(c + 1) * HQ]
            s_t = st[d]
            o4 = o_intra[c * HQ:(c + 1) * HQ] + _dot_nt(q4, s_t.astype(BF16))
            st[d] = s_t * decay[c * Q:c * Q + 1] + _dot_tn(v16[c * HQ:(c + 1) * HQ], k16[c * HQ:(c + 1) * HQ])
            acc[u * SC + c] += o4

    def scan_body(i, carry):
        super_dir(i, 0)
        super_dir(nsc - 1 - i, 1)
        return carry

    lax.fori_loop(0, nsc, scan_body, 0)

    def fin_body(c, carry):
        r0 = pl.multiple_of(c * Q, Q)
        o = acc[c]
        o = o * lax.rsqrt(jnp.mean(o * o, axis=-1, keepdims=True) + EPS) * ng_ref[...]
        for h in range(H):
            gate = _silu(r_ref[pl.ds(r0, Q), h * DV:(h + 1) * DV].astype(F32))
            o_ref[pl.ds(r0, Q), h * DV:(h + 1) * DV] = (o[h * Q:(h + 1) * Q] * gate).astype(o_ref.dtype)
        return carry

    lax.fori_loop(0, nc, fin_body, 0)

    if emit_state:
        for d in range(2):
            for h in range(H):
                so_ref[d, h] = st[d, :, h * DK:(h + 1) * DK].T


def _gla_branch(pm, ps, w_gate, b_gate, norm_g, s0, B, L, emit_state):
    use_s0 = s0 is not None
    H, DK, DV = GLA_HEADS, GLA_DK_HEAD, GLA_DV_HEAD
    wg = jnp.zeros((S_COLS, 2 * GLA_DK), F32)
    for d in range(2):
        wg = wg.at[S_LR + d * GLA_LOWRANK:S_LR + (d + 1) * GLA_LOWRANK, d * GLA_DK:(d + 1) * GLA_DK].set(w_gate[d])
    bg = b_gate.reshape(1, 2 * GLA_DK)
    in_specs = [
        pl.BlockSpec((L, GLA_DK), lambda b: (b, M_GQ // GLA_DK)),
        pl.BlockSpec((L, GLA_DK), lambda b: (b, M_GK // GLA_DK)),
        pl.BlockSpec((L, GLA_DV), lambda b: (b, M_GV // GLA_DV)),
        pl.BlockSpec((L, GLA_DV), lambda b: (b, M_R // GLA_DV)),
        pl.BlockSpec((L, S_COLS), lambda b: (b, 0)),
        pl.BlockSpec((S_COLS, 2 * GLA_DK), lambda b: (0, 0)),
        pl.BlockSpec((1, 2 * GLA_DK), lambda b: (0, 0)),
        pl.BlockSpec((1, DV), lambda b: (0, 0)),
    ]
    args = [pm, pm, pm, pm, ps, wg, bg, norm_g.reshape(1, DV)]
    state_spec = pl.BlockSpec((None, 2, H, DK, DV), lambda b: (b, 0, 0, 0, 0))
    if use_s0:
        in_specs.append(state_spec)
        args.append(s0)
    out_specs = [pl.BlockSpec((L, GLA_DV), lambda b: (b, 0))]
    out_shape = [jax.ShapeDtypeStruct((B * L, GLA_DV), BF16)]
    if emit_state:
        out_specs.append(state_spec)
        out_shape.append(jax.ShapeDtypeStruct((B, 2, H, DK, DV), F32))
    outs = pl.pallas_call(
        functools.partial(_gla_kernel, L=L, use_s0=use_s0, emit_state=emit_state),
        grid=(B,),
        in_specs=in_specs,
        out_specs=out_specs,
        out_shape=out_shape,
        scratch_shapes=[pltpu.VMEM((L // GLA_Q, H * GLA_Q, DV), F32), pltpu.VMEM((2, DV, H * DK), F32)],
        compiler_params=_cparams(("arbitrary",)),
        name="gla_scan",
    )(*args)
    return outs[0], (outs[1] if emit_state else None)


def _merge_kernel(*refs, aliased):
    (x_ref, yssd_ref, z_ref, ona_ref, ogla_ref, g0_ref, g1_ref, g2_ref,
     gate1_ref, shift2_ref, scale2_ref, sng_ref, n2g_ref, wb_ref, wo_ref, rw_ref, rb_ref) = refs[:17]
    xo_ref, h2_ref, lg_ref = refs[-3:]
    y = yssd_ref[...] * _silu(z_ref[...].astype(F32))
    y = y * lax.rsqrt(jnp.mean(y * y, axis=-1, keepdims=True) + EPS) * sng_ref[...]
    m = jax.nn.sigmoid(g0_ref[...].astype(F32)) * _dot(y.astype(BF16), wb_ref[0])
    m = m + jax.nn.sigmoid(g1_ref[...].astype(F32)) * _dot(ona_ref[...], wb_ref[1])
    m = m + jax.nn.sigmoid(g2_ref[...].astype(F32)) * _dot(ogla_ref[...], wb_ref[2])
    x = x_ref[...] + gate1_ref[...] * _dot(m.astype(BF16), wo_ref[...])
    xo_ref[...] = x
    h2 = x * lax.rsqrt(jnp.mean(x * x, axis=-1, keepdims=True) + EPS) * n2g_ref[...]
    h2 = h2 * (1.0 + scale2_ref[...]) + shift2_ref[...]
    h2_ref[...] = h2
    lg_ref[...] = jnp.dot(h2, rw_ref[...], preferred_element_type=F32,
                          precision=lax.Precision.HIGHEST) + rb_ref[...]


def _merge(x2, y_ssd, pm, o_na, o_gla, mod_l, ssd_norm_g, norm2_g, wb_bf, wo_bf, rw_pad, rb_pad,
           rows_per_mod, mod_row0, t_all, row0, shared):
    T = x2.shape[0]
    tm = min(256, rows_per_mod)
    per = rows_per_mod // tm
    D = D_MODEL
    ng = M_GATES // D
    blk0 = row0 // tm

    def mod_map(col):
        return lambda i: (mod_row0 + i // per, 0, col)

    row = lambda i: (i, 0)
    row_off = lambda i: (blk0 + i, 0)
    const2 = lambda i: (0, 0)
    in_specs = [
        pl.BlockSpec((tm, D), row),
        pl.BlockSpec((tm, D), row),
        pl.BlockSpec((tm, D), lambda i: (i, M_Z // D)),
        pl.BlockSpec((tm, D), row),
        pl.BlockSpec((tm, D), row),
        pl.BlockSpec((tm, D), lambda i: (i, ng)),
        pl.BlockSpec((tm, D), lambda i: (i, ng + 1)),
        pl.BlockSpec((tm, D), lambda i: (i, ng + 2)),
        pl.BlockSpec((None, 1, D), mod_map(2)),
        pl.BlockSpec((None, 1, D), mod_map(3)),
        pl.BlockSpec((None, 1, D), mod_map(4)),
        pl.BlockSpec((1, D), const2),
        pl.BlockSpec((1, D), const2),
        pl.BlockSpec((N_BRANCH, D, D), lambda i: (0, 0, 0)),
        pl.BlockSpec((D, D), const2),
        pl.BlockSpec((D, 128), const2),
        pl.BlockSpec((1, 128), const2),
    ]
    args = [x2, y_ssd, pm, o_na, o_gla, pm, pm, pm, mod_l, mod_l, mod_l,
            ssd_norm_g.reshape(1, D), norm2_g.reshape(1, D), wb_bf, wo_bf, rw_pad, rb_pad]
    aliases = {}
    if shared is not None:
        in_specs += [pl.BlockSpec(memory_space=pl.ANY), pl.BlockSpec(memory_space=pl.ANY)]
        aliases = {len(args): 1, len(args) + 1: 2}
        args += list(shared)
    return pl.pallas_call(
        functools.partial(_merge_kernel, aliased=shared is not None),
        grid=(T // tm,),
        in_specs=in_specs,
        out_specs=[pl.BlockSpec((tm, D), row), pl.BlockSpec((tm, D), row_off), pl.BlockSpec((tm, 128), row_off)],
        out_shape=[jax.ShapeDtypeStruct((T, D), F32), jax.ShapeDtypeStruct((t_all, D), F32),
                   jax.ShapeDtypeStruct((t_all, 128), F32)],
        input_output_aliases=aliases,
        compiler_params=_cparams(("arbitrary",)),
        name="branch_merge",
    )(*args)


def _moe_kernel(be_ref, nused_ref, dst_ref, dstn_ref, h2_hbm,
                wgu_ref, bgu_ref, wdn_ref, bdn_ref, y_hbm, xbuf, obuf, wgu_bf, wdn_bf, gsem, ssem):
    b = pl.program_id(0)
    nused = nused_ref[0]
    slot = b % 2
    n_tok = h2_hbm.shape[0]

    def token_of(row):
        if n_tok & (n_tok - 1) == 0:
            return jnp.bitwise_and(row, n_tok - 1)
        return lax.rem(row, n_tok)

    def start_gather(idx_ref, s):
        def body(i, carry):
            pltpu.make_async_copy(h2_hbm.at[pl.ds(token_of(idx_ref[0, i]), 1), :], xbuf.at[s, pl.ds(i, 1), :],
                                  gsem.at[s]).start()
            return carry
        lax.fori_loop(0, MOE_ROWS, body, 0, unroll=MOE_DMA_UNROLL)

    def wait_gather(s):
        pltpu.make_async_copy(h2_hbm.at[pl.ds(0, MOE_ROWS), :], xbuf.at[s], gsem.at[s]).wait()

    def wait_scatter():
        pltpu.make_async_copy(obuf, y_hbm.at[pl.ds(0, MOE_ROWS), :], ssem.at[0]).wait()

    @pl.when(b == 0)
    def _():
        start_gather(dst_ref, 0)

    @pl.when(b < nused)
    def _():
        wait_gather(slot)

        prev = be_ref[jnp.maximum(b - 1, 0)]

        @pl.when(jnp.logical_or(b == 0, be_ref[b] != prev))
        def _():
            rc = 128

            def cast_body(i, carry):
                r0 = pl.multiple_of(i * rc, rc)
                wgu_bf[pl.ds(r0, rc), :] = wgu_ref[pl.ds(r0, rc), :].astype(BF16)
                wdn_bf[pl.ds(r0, rc), :] = wdn_ref[pl.ds(r0, rc), :].astype(BF16)
                return carry

            lax.fori_loop(0, D_MODEL // rc, cast_body, 0)

        for i in range(MOE_ROWS):
            pltpu.make_async_copy(h2_hbm.at[pl.ds(token_of(dstn_ref[0, i]), 1), :],
                                  xbuf.at[1 - slot, pl.ds(i, 1), :], gsem.at[1 - slot]).start()

        x = xbuf[slot].astype(BF16)
        acc = jnp.zeros((MOE_ROWS, D_MODEL), F32) + bdn_ref[...]
        for f in range(D_FF // MOE_FF_TILE):
            c0 = f * MOE_FF_TILE
            glu = _dot(x, wgu_bf[:, c0:c0 + MOE_FF_TILE]) + bgu_ref[:, c0:c0 + MOE_FF_TILE]
            lin = _dot(x, wgu_bf[:, D_FF + c0:D_FF + c0 + MOE_FF_TILE]) + bgu_ref[:, D_FF + c0:D_FF + c0 + MOE_FF_TILE]
            glu = jnp.minimum(glu, SWIGLU_LIMIT)
            lin = jnp.clip(lin, -SWIGLU_LIMIT, SWIGLU_LIMIT)
            act = glu * jax.nn.sigmoid(SWIGLU_ALPHA * glu) * (lin + 1.0)
            acc = acc + _dot(act.astype(BF16), wdn_bf[c0:c0 + MOE_FF_TILE, :])

        @pl.when(b > 0)
        def _():
            wait_scatter()

        obuf[...] = acc

        def sc_body(i, carry):
            pltpu.make_async_copy(obuf.at[pl.ds(i, 1), :], y_hbm.at[pl.ds(dst_ref[0, i], 1), :], ssem.at[0]).start()
            return carry

        lax.fori_loop(0, MOE_ROWS, sc_body, 0, unroll=MOE_DMA_UNROLL)

        @pl.when(b == nused - 1)
        def _():
            wait_scatter()
            wait_gather(1 - slot)


def _moe_experts(h2_all, slot_dst, block_e, n_used, w_gu, b_gu, w_dn, b_dn, layer):
    n_blocks = slot_dst.shape[0]
    T, D = h2_all.shape
    smem_blk = lambda f: pl.BlockSpec((None, 1, MOE_ROWS), f, memory_space=pltpu.SMEM)
    grid_spec = pltpu.PrefetchScalarGridSpec(
        num_scalar_prefetch=2,
        grid=(n_blocks,),
        in_specs=[
            smem_blk(lambda b, be, nu: (b, 0, 0)),
            smem_blk(lambda b, be, nu: (jnp.minimum(b + 1, n_blocks - 1), 0, 0)),
            pl.BlockSpec(memory_space=pl.ANY),
            pl.BlockSpec((None, None, D, 2 * D_FF), lambda b, be, nu: (layer, be[b], 0, 0)),
            pl.BlockSpec((None, None, 1, 2 * D_FF), lambda b, be, nu: (layer, be[b], 0, 0)),
            pl.BlockSpec((None, None, D_FF, D), lambda b, be, nu: (layer, be[b], 0, 0)),
            pl.BlockSpec((None, None, 1, D), lambda b, be, nu: (layer, be[b], 0, 0)),
        ],
        out_specs=pl.BlockSpec(memory_space=pl.ANY),
        scratch_shapes=[pltpu.VMEM((2, MOE_ROWS, D), F32), pltpu.VMEM((MOE_ROWS, D), F32),
                        pltpu.VMEM((D, 2 * D_FF), BF16), pltpu.VMEM((D_FF, D), BF16),
                        pltpu.SemaphoreType.DMA((2,)), pltpu.SemaphoreType.DMA((1,))],
    )
    return pl.pallas_call(
        _moe_kernel,
        grid_spec=grid_spec,
        out_shape=jax.ShapeDtypeStruct((TOP_K * T + MOE_ROWS, D), F32),
        compiler_params=_cparams(("arbitrary",)),
        name="moe_experts",
    )(block_e, n_used, slot_dst, slot_dst, h2_all, w_gu,
      b_gu.reshape(DEPTH, N_EXPERTS, 1, 2 * D_FF), w_dn, b_dn.reshape(DEPTH, N_EXPERTS, 1, D))


def _moe_route(logits):
    T = logits.shape[0]
    TK = T * TOP_K
    top_logit, top_idx = lax.top_k(logits, TOP_K)
    top_w = jax.nn.softmax(top_logit, axis=-1)
    flat_e = top_idx.reshape(TK).astype(jnp.int32)
    onehot = (flat_e[:, None] == jnp.arange(N_EXPERTS, dtype=jnp.int32)[None, :]).astype(jnp.int32)
    csum = jnp.cumsum(onehot, axis=0)
    rank = jnp.sum(onehot * csum, axis=1) - 1
    counts = csum[-1]
    padded = (counts + MOE_ROWS - 1) // MOE_ROWS * MOE_ROWS
    pad_end = jnp.cumsum(padded)
    pad_start = pad_end - padded
    dest = jnp.sum(onehot * pad_start[None, :], axis=1) + rank
    n_blocks = -(-(TK + N_EXPERTS * (MOE_ROWS - 1)) // MOE_ROWS)
    n_slots = n_blocks * MOE_ROWS
    flat = jnp.arange(TK, dtype=jnp.int32)
    spare = TK + jnp.arange(n_slots, dtype=jnp.int32) % MOE_ROWS
    slot_dst = spare.at[dest].set((flat % TOP_K) * T + flat // TOP_K)
    blk_start = jnp.arange(n_blocks, dtype=jnp.int32) * MOE_ROWS
    block_e = jnp.sum((blk_start[:, None] >= pad_end[None, :]).astype(jnp.int32), axis=1)
    block_e = jnp.minimum(block_e, N_EXPERTS - 1).astype(jnp.int32)
    n_used = (pad_end[-1] // MOE_ROWS).astype(jnp.int32).reshape(1)
    top_w_pad = jnp.pad(top_w, ((0, 0), (0, 128 - TOP_K)))
    return top_w_pad, slot_dst.reshape(n_blocks, 1, MOE_ROWS), block_e, n_used


def _combine_kernel(x_ref, y0_ref, y1_ref, y2_ref, y3_ref, w_ref, gate_ref, o_ref):
    w = w_ref[...]
    y = w[:, 0:1] * y0_ref[...]
    for k, y_ref in enumerate((y1_ref, y2_ref, y3_ref), start=1):
        y = y + w[:, k:k + 1] * y_ref[...]
    o_ref[...] = x_ref[...] + gate_ref[...] * y


def _moe_combine(x2, y_rows, top_w_pad, mod_l, rows_per_mod, mod_row0, row0, t_all):
    assert TOP_K == 4
    T, D = x2.shape
    tm = min(512, rows_per_mod)
    per = rows_per_mod // tm
    blk0 = row0 // tm
    nblk = t_all // tm

    def y_map(k):
        return lambda i: (k * nblk + blk0 + i, 0)

    return pl.pallas_call(
        _combine_kernel,
        grid=(T // tm,),
        in_specs=[pl.BlockSpec((tm, D), lambda i: (i, 0))]
        + [pl.BlockSpec((tm, D), y_map(k)) for k in range(TOP_K)]
        + [pl.BlockSpec((tm, 128), lambda i: (blk0 + i, 0)),
           pl.BlockSpec((None, 1, D), lambda i: (mod_row0 + i // per, 0, 5))],
        out_specs=pl.BlockSpec((tm, D), lambda i: (i, 0)),
        out_shape=jax.ShapeDtypeStruct((T, D), F32),
        compiler_params=_cparams(("arbitrary",)),
        name="moe_combine",
    )(x2, y_rows, y_rows, y_rows, y_rows, top_w_pad, mod_l)


def _final_norm_kernel(x_ref, g_ref, o_ref):
    x = x_ref[...]
    o_ref[...] = x * lax.rsqrt(jnp.mean(x * x, axis=-1, keepdims=True) + EPS) * g_ref[...]


def _final_norm(x2, g):
    T = x2.shape[0]
    tm = 512
    return pl.pallas_call(
        _final_norm_kernel,
        grid=(T // tm,),
        in_specs=[pl.BlockSpec((tm, D_MODEL), lambda i: (i, 0)), pl.BlockSpec((1, D_MODEL), lambda i: (0, 0))],
        out_specs=pl.BlockSpec((tm, D_MODEL), lambda i: (i, 0)),
        out_shape=jax.ShapeDtypeStruct((T, D_MODEL), F32),
        compiler_params=_cparams(("arbitrary",)),
        name="final_norm",
    )(x2, g.reshape(1, D_MODEL))


def _mixer_half(x2, B, L, mod_l, mod_row0, lw, ctx, t_all, row0, shared):
    latent = ctx is not None
    rows_per_mod = L if latent else B * L
    pm, ps = _in_projection(x2, mod_l, lw["norm1_g"], lw["w_in"], rows_per_mod, mod_row0)
    if latent:
        ck, cv, s_ssd0, s_gla0 = ctx
    else:
        s_ssd0 = s_gla0 = None
    y_ssd, s_ssd = _ssd_branch(pm, ps, lw["ssd_conv_w"], lw["ssd_conv_b"], lw["ssd_dt_bias"], lw["ssd_a_log"],
                               lw["ssd_d"], s_ssd0, B, L, emit_state=not latent)
    if latent:
        o_na = _neighbourhood_attention(pm, ck, cv, lw["na_bias"], B, L)
    else:
        o_na = _context_attention(pm, B, L)
    o_gla, s_gla = _gla_branch(pm, ps, lw["gla_w_gate"], lw["gla_b_gate"], lw["gla_norm_g"], s_gla0, B, L,
                               emit_state=not latent)
    x_new, h2, logits = _merge(x2, y_ssd, pm, o_na, o_gla, mod_l, lw["ssd_norm_g"], lw["norm2_g"],
                               lw["w_branch"], lw["w_out"], lw["router_w"], lw["router_b"], rows_per_mod, mod_row0,
                               t_all, row0, shared)
    return x_new, h2, logits, pm, s_ssd, s_gla


def kernel(x_prompt, x_sample, cache_na_k, cache_na_v, state_ssd, state_gla, c, c_ctx, w_ada, b_ada, norm1_g, norm2_g, w_in, ssd_conv_w, ssd_conv_b, ssd_dt_bias, ssd_a_log, ssd_d, ssd_norm_g, na_rpb, gla_w_gate, gla_b_gate, gla_norm_g, w_branch, w_out, router_w, router_b, moe_w_gu, moe_b_gu, moe_w_dn, moe_b_dn, final_norm_g):
    Bp, Lp, D = x_prompt.shape
    Bs, Ls, _ = x_sample.shape
    Tp, Ts = Bp * Lp, Bs * Ls
    Lc = cache_na_k.shape[2]

    cvecs = jnp.concatenate([c_ctx[None], c, jnp.zeros((8 - 1 - Bs, D), F32)], axis=0)
    mod = _modulation(cvecs, w_ada, b_ada).reshape(DEPTH, 8, 1, 6 * D)

    xp = x_prompt.reshape(Tp, D)
    xs = x_sample.reshape(Ts, D)
    ks_, vs_, sss_, sgs_ = [], [], [], []
    for l in range(DEPTH):
        lw = {
            "norm1_g": norm1_g[l], "norm2_g": norm2_g[l], "w_in": _relayout_w_in(w_in[l]),
            "ssd_conv_w": ssd_conv_w[l], "ssd_conv_b": ssd_conv_b[l], "ssd_dt_bias": ssd_dt_bias[l],
            "ssd_a_log": ssd_a_log[l], "ssd_d": ssd_d[l], "ssd_norm_g": ssd_norm_g[l],
            "na_bias": _na_bias_table(na_rpb[l]),
            "gla_w_gate": gla_w_gate[l], "gla_b_gate": gla_b_gate[l], "gla_norm_g": gla_norm_g[l],
            "w_branch": w_branch[l].astype(BF16), "w_out": w_out[l].astype(BF16),
            "router_w": jnp.pad(router_w[l], ((0, 0), (0, 128 - N_EXPERTS))),
            "router_b": jnp.pad(router_b[l], (0, 128 - N_EXPERTS)).reshape(1, 128),
        }
        mod_l = mod[l]
        t_all = Tp + Ts
        xp, h2a, lga, pmp, s_ssd, s_gla = _mixer_half(xp, Bp, Lp, mod_l, 0, lw, None, t_all, 0, None)
        ctx = (cache_na_k[:, l].reshape(Bs, Lc, NA_INNER), cache_na_v[:, l].reshape(Bs, Lc, NA_INNER),
               state_ssd[:, l], state_gla[:, l])
        xs, h2a, lga, _, _, _ = _mixer_half(xs, Bs, Ls, mod_l, 1, lw, ctx, t_all, Tp, (h2a, lga))
        ks_.append(pmp[:, M_K:M_K + NA_INNER].astype(F32).reshape(Bp, Lp, NA_HEADS, NA_HEAD_DIM))
        vs_.append(pmp[:, M_V:M_V + NA_INNER].astype(F32).reshape(Bp, Lp, NA_HEADS, NA_HEAD_DIM))
        sss_.append(s_ssd)
        sgs_.append(s_gla)

        top_w, slot_dst, block_e, n_used = _moe_route(lga[:, :N_EXPERTS])
        y_rows = _moe_experts(h2a, slot_dst, block_e, n_used, moe_w_gu, moe_b_gu, moe_w_dn, moe_b_dn, l)
        xp = _moe_combine(xp, y_rows, top_w, mod_l, Tp, 0, 0, t_all)
        xs = _moe_combine(xs, y_rows, top_w, mod_l, Ls, 1, Tp, t_all)

    y_prompt = _final_norm(xp, final_norm_g).reshape(Bp, Lp, D)
    y_sample = _final_norm(xs, final_norm_g).reshape(Bs, Ls, D)
    return (y_prompt, y_sample, jnp.stack(ks_, axis=1), jnp.stack(vs_, axis=1),
            jnp.stack(sss_, axis=1), jnp.stack(sgs_, axis=1))
```

```python
import functools

import numpy as np
import jax
import jax.numpy as jnp
from jax import lax
from jax.experimental import pallas as pl
from jax.experimental.pallas import tpu as pltpu

F32 = jnp.float32
BF16 = jnp.bfloat16

D_MODEL = 1024
DEPTH = 4
GRID_W = 64
EPS = 1e-6

SSD_HEADS = 16
SSD_HEAD_DIM = 64
SSD_INNER = SSD_HEADS * SSD_HEAD_DIM
SSD_GROUPS = 2
SSD_STATE = 128
SSD_GN = SSD_GROUPS * SSD_STATE
SSD_CONV_W = 5
SSD_CONV_DIM = SSD_INNER + 2 * SSD_GN
SSD_Q = 64
SSD_GW = SSD_INNER // SSD_GROUPS

NA_HEADS = 16
NA_HEAD_DIM = 64
NA_INNER = NA_HEADS * NA_HEAD_DIM
NA_WIN_ROWS = 8
NA_WIN_COLS = 16
NA_SCALE = NA_HEAD_DIM ** -0.5
NA_NEG = -1e30

GLA_HEADS = 4
GLA_DK = D_MODEL // 2
GLA_DV = D_MODEL
GLA_DK_HEAD = GLA_DK // GLA_HEADS
GLA_DV_HEAD = GLA_DV // GLA_HEADS
GLA_LOWRANK = 16
GLA_TAU = 16.0
GLA_Q = 32
GLA_QK_SCALE = GLA_DK_HEAD ** -0.5

N_BRANCH = 3
N_EXPERTS = 32
TOP_K = 4
D_FF = D_MODEL
SWIGLU_ALPHA = 1.702
SWIGLU_LIMIT = 7.0
MOE_ROWS = 512
MOE_FF_TILE = 512
MOE_DMA_UNROLL = 8

SSD_UNROLL = 2
SSD_DECAY_ROWS = 256
NA_UNROLL = 4
GLA_SUPER = 4

OFF_SSD_XBC = SSD_INNER
OFF_SSD_DT = OFF_SSD_XBC + SSD_CONV_DIM
OFF_NA_QKV = OFF_SSD_DT + 2 * SSD_HEADS
OFF_GLA_Q = OFF_NA_QKV + 3 * NA_INNER
OFF_GLA_LR = OFF_GLA_Q + 2 * GLA_DK + 2 * GLA_DV
OFF_GATES = OFF_GLA_LR + 2 * GLA_LOWRANK
IN_COLS = OFF_GATES + N_BRANCH * D_MODEL

M_Z = 0
M_X = M_Z + SSD_INNER
M_Q = M_X + SSD_INNER
M_K = M_Q + NA_INNER
M_V = M_K + NA_INNER
M_GV = M_V + NA_INNER
M_R = M_GV + GLA_DV
M_GATES = M_R + GLA_DV
M_GQ = M_GATES + N_BRANCH * D_MODEL
M_GK = M_GQ + GLA_DK
M_B = M_GK + GLA_DK
M_C = M_B + SSD_GN
M_COLS = M_C + SSD_GN
_MAIN_SEGMENTS = (
    (0, 2 * SSD_INNER),
    (OFF_NA_QKV, 3 * NA_INNER),
    (OFF_GLA_Q + 2 * GLA_DK, 2 * GLA_DV),
    (OFF_GATES, N_BRANCH * D_MODEL),
    (OFF_GLA_Q, 2 * GLA_DK),
    (OFF_SSD_XBC + SSD_INNER, 2 * SSD_GN),
)
_SMALL_SEGMENTS = ((OFF_SSD_DT, 2 * SSD_HEADS), (OFF_GLA_LR, 2 * GLA_LOWRANK))
S_DT = 0
S_LR = 2 * SSD_HEADS
S_COLS = 128
PROJ_TN = 512
PROJ_COLS = M_COLS + PROJ_TN
PROJ_TM = 2048

VMEM_LIMIT = 56 * 1024 * 1024


def _cparams(sem):
    return pltpu.CompilerParams(dimension_semantics=sem, vmem_limit_bytes=VMEM_LIMIT)


def _silu(x):
    return x * jax.nn.sigmoid(x)


def _softplus(x):
    return jnp.maximum(x, 0.0) + jnp.log1p(jnp.exp(-jnp.abs(x)))


def _split3(x):
    hi = x.astype(BF16)
    r1 = x - hi.astype(F32)
    mid = r1.astype(BF16)
    lo = (r1 - mid.astype(F32)).astype(BF16)
    return hi, mid, lo


def _dot(a, b):
    return jnp.dot(a, b, preferred_element_type=F32)


def _dot_nt(a, b):
    return lax.dot_general(a, b, (((1,), (1,)), ((), ())), preferred_element_type=F32)


def _dot_tn(a, b):
    return lax.dot_general(a, b, (((0,), (0,)), ((), ())), preferred_element_type=F32)


def _sel_dot_l(sel_bf, x):
    hi, mid, lo = _split3(x)
    return (_dot(sel_bf, lo) + _dot(sel_bf, mid)) + _dot(sel_bf, hi)


def _sel_dot_r(x, sel_bf):
    hi, mid, lo = _split3(x)
    return (_dot(lo, sel_bf) + _dot(mid, sel_bf)) + _dot(hi, sel_bf)


def _mod_kernel(c_ref, w_ref, b_ref, o_ref):
    c = c_ref[...]
    o_ref[...] = jnp.dot(_silu(c), w_ref[...], preferred_element_type=F32,
                         precision=lax.Precision.HIGHEST) + b_ref[...]


def _modulation(cvecs, w_ada, b_ada):
    nrow = cvecs.shape[0]
    return pl.pallas_call(
        _mod_kernel,
        grid=(DEPTH, 6),
        in_specs=[
            pl.BlockSpec((nrow, D_MODEL), lambda l, j: (0, 0)),
            pl.BlockSpec((None, D_MODEL, D_MODEL), lambda l, j: (l, 0, j)),
            pl.BlockSpec((None, 1, D_MODEL), lambda l, j: (l, 0, j)),
        ],
        out_specs=pl.BlockSpec((None, nrow, D_MODEL), lambda l, j: (l, 0, j)),
        out_shape=jax.ShapeDtypeStruct((DEPTH, nrow, 6 * D_MODEL), F32),
        compiler_params=_cparams(("arbitrary", "arbitrary")),
        name="adaln_mod",
    )(cvecs, w_ada, b_ada.reshape(DEPTH, 1, 6 * D_MODEL))


def _inproj_kernel(x_ref, g_ref, shift_ref, scale_ref, w_ref, om_ref, os_ref, h_ref, *, n_main):
    j = pl.program_id(1)

    @pl.when(j == 0)
    def _():
        x = x_ref[...]
        h = x * lax.rsqrt(jnp.mean(x * x, axis=-1, keepdims=True) + EPS) * g_ref[...]
        h = h * (1.0 + scale_ref[...]) + shift_ref[...]
        h_ref[...] = h.astype(BF16)

    acc = _dot(h_ref[...], w_ref[...])

    @pl.when(j < n_main)
    def _():
        om_ref[...] = acc.astype(BF16)

    @pl.when(j == n_main)
    def _():
        os_ref[...] = acc[:, :S_COLS]


def _relayout_w_in(w):
    parts = [w[:, s:s + n] for s, n in _MAIN_SEGMENTS + _SMALL_SEGMENTS]
    parts.append(jnp.zeros((w.shape[0], PROJ_TN - 2 * SSD_HEADS - 2 * GLA_LOWRANK), w.dtype))
    return jnp.concatenate(parts, axis=1).astype(BF16)


def _in_projection(x2, mod_l, norm_g, w_perm, rows_per_mod, mod_row0):
    T = x2.shape[0]
    tm = min(PROJ_TM, rows_per_mod)
    n_main = M_COLS // PROJ_TN
    per = rows_per_mod // tm

    def mod_map(col):
        return lambda i, j: (mod_row0 + i // per, 0, col)

    return pl.pallas_call(
        functools.partial(_inproj_kernel, n_main=n_main),
        grid=(T // tm, n_main + 1),
        in_specs=[
            pl.BlockSpec((tm, D_MODEL), lambda i, j: (i, 0)),
            pl.BlockSpec((1, D_MODEL), lambda i, j: (0, 0)),
            pl.BlockSpec((None, 1, D_MODEL), mod_map(0)),
            pl.BlockSpec((None, 1, D_MODEL), mod_map(1)),
            pl.BlockSpec((D_MODEL, PROJ_TN), lambda i, j: (0, j)),
        ],
        out_specs=[
            pl.BlockSpec((tm, PROJ_TN), lambda i, j: (i, jnp.minimum(j, n_main - 1))),
            pl.BlockSpec((tm, S_COLS), lambda i, j: (i, 0)),
        ],
        out_shape=[jax.ShapeDtypeStruct((T, M_COLS), BF16), jax.ShapeDtypeStruct((T, S_COLS), F32)],
        scratch_shapes=[pltpu.VMEM((tm, D_MODEL), BF16)],
        compiler_params=_cparams(("arbitrary", "arbitrary")),
        name="in_projection",
    )(x2, norm_g.reshape(1, D_MODEL), mod_l, mod_l, w_perm)


_CONV_WIN = 128
_CONV_LEAD = 64
_CONV_OFF = 48


def _ssd_kernel(*refs, L, use_s0, emit_state):
    it = iter(refs)
    x_ref, b_ref, c_ref, ps_ref = next(it), next(it), next(it), next(it)
    cwx_ref, cwb_ref, cwc_ref = next(it), next(it), next(it)
    cbx_ref, cbb_ref, cbc_ref = next(it), next(it), next(it)
    dtb_ref, alog_ref, dsk_ref = next(it), next(it), next(it)
    s0_ref = next(it) if use_s0 else None
    y_ref = next(it)
    so_ref = next(it) if emit_state else None
    xpad, bpad, cpad, xc, bc, cc, dts, st, dtx, cumx = (next(it) for _ in range(10))

    Q = SSD_Q
    nc = L // Q
    g = pl.program_id(1)
    pad = SSD_CONV_W // 2

    for src, dst in ((x_ref, xpad), (b_ref, bpad), (c_ref, cpad)):
        w = dst.shape[1]
        dst[0:_CONV_LEAD, :] = jnp.zeros((_CONV_LEAD, w), BF16)
        dst[_CONV_LEAD + L:_CONV_LEAD + L + _CONV_LEAD, :] = jnp.zeros((_CONV_LEAD, w), BF16)

    def copy_body(c, carry):
        r0 = pl.multiple_of(c * Q, Q)
        for src, dst in ((x_ref, xpad), (b_ref, bpad), (c_ref, cpad)):
            dst[pl.ds(r0 + _CONV_LEAD, Q), :] = src[pl.ds(r0, Q), :]
        return carry

    lax.fori_loop(0, nc, copy_body, 0)

    ri = lax.broadcasted_iota(jnp.int32, (SSD_CONV_W * Q, _CONV_WIN), 0)
    ci = lax.broadcasted_iota(jnp.int32, (SSD_CONV_W * Q, _CONV_WIN), 1)
    shift_sel = (ci == (ri % Q) + (ri // Q) + (_CONV_LEAD - _CONV_OFF - pad)).astype(BF16)

    def conv_chunk(pad_ref, w_ref, bias_ref, r0):
        win = pad_ref[pl.ds(r0 + _CONV_OFF, _CONV_WIN), :]
        sh = _dot(shift_sel, win)
        acc = bias_ref[...] + w_ref[0:1, :] * sh[0:Q]
        for k in range(1, SSD_CONV_W):
            acc = acc + w_ref[k:k + 1, :] * sh[k * Q:(k + 1) * Q]
        return _silu(acc)

    def prep_body(c, carry):
        r0 = pl.multiple_of(c * Q, Q)
        xv = conv_chunk(xpad, cwx_ref, cbx_ref, r0)
        xc[pl.ds(r0, Q), :] = xv
        bc[pl.ds(r0, Q), :] = conv_chunk(bpad, cwb_ref, cbb_ref, r0)
        cc[pl.ds(r0, Q), :] = conv_chunk(cpad, cwc_ref, cbc_ref, r0)
        dts[pl.ds(r0, Q), :] = _softplus(ps_ref[pl.ds(r0, Q), :] + dtb_ref[...])
        y_ref[pl.ds(r0, Q), :] = dsk_ref[...] * xv
        return carry

    lax.fori_loop(0, nc, prep_body, 0)

    GW = SSD_GW
    hpg = SSD_HEADS // SSD_GROUPS
    er = lax.broadcasted_iota(jnp.int32, (S_COLS, GW), 0)
    ec = lax.broadcasted_iota(jnp.int32, (S_COLS, GW), 1)
    qi = lax.broadcasted_iota(jnp.int32, (Q, GW), 0)
    qj = lax.broadcasted_iota(jnp.int32, (Q, GW), 1) % Q
    diag_sel = (qi == qj).astype(F32)
    ones_q = jnp.ones((Q, Q), BF16)
    ti = lax.broadcasted_iota(jnp.int32, (Q, Q), 0)
    tj = lax.broadcasted_iota(jnp.int32, (Q, Q), 1)
    lane = lax.broadcasted_iota(jnp.int32, (Q, 128), 1)
    lo_half = lane < SSD_HEAD_DIM
    a_all = -jnp.exp(alog_ref[...])

    per_dir = []
    for d in range(2):
        base = S_DT + d * SSD_HEADS + g * hpg
        esel = (er == base + ec // SSD_HEAD_DIM).astype(BF16)
        a_exp = _sel_dot_r(jnp.broadcast_to(a_all, (8, S_COLS)), esel)[0:1]
        if d == 0:
            tri = (tj <= ti).astype(BF16)
            mask = qj <= qi
        else:
            tri = (tj >= ti).astype(BF16)
            mask = qj >= qi
        per_dir.append((esel, a_exp, tri, mask))

    for d in range(2):
        if use_s0:
            st[d] = s0_ref[d].T
        else:
            st[d] = jnp.zeros((SSD_STATE, GW), F32)

    SR = SSD_DECAY_ROWS
    bi = lax.broadcasted_iota(jnp.int32, (SR, SR), 0)
    bj = lax.broadcasted_iota(jnp.int32, (SR, SR), 1)
    same_chunk = (bi // Q) == (bj // Q)
    big_tri = (jnp.logical_and(same_chunk, bj <= bi).astype(BF16), jnp.logical_and(same_chunk, bj >= bi).astype(BF16))

    def decay_body(c, carry):
        r0 = pl.multiple_of(c * SR, SR)
        dt_c = dts[pl.ds(r0, SR), :]
        for d in range(2):
            esel, a_exp, _, _ = per_dir[d]
            dt_exp = _sel_dot_r(dt_c, esel)
            dtx[d, pl.ds(r0, SR), :] = dt_exp
            cumx[d, pl.ds(r0, SR), :] = _sel_dot_l(big_tri[d], dt_exp * a_exp)
        return carry

    lax.fori_loop(0, L // SR, decay_body, 0)

    def chunk_dir(c, d):
        _, _, _, mask = per_dir[d]
        r0 = pl.multiple_of(c * Q, Q)
        dt_exp = dtx[d, pl.ds(r0, Q), :]
        cum = cumx[d, pl.ds(r0, Q), :]
        rowb = _sel_dot_l(ones_q, cum * diag_sel)
        lmat = jnp.exp(jnp.where(mask, cum - rowb, -jnp.inf))
        bq = bc[pl.ds(r0, Q), :].astype(BF16)
        cq = cc[pl.ds(r0, Q), :].astype(BF16)
        cb = _dot_nt(cq, jnp.concatenate([bq] * (GW // Q), axis=0))
        amat = (cb * lmat).astype(BF16)
        xq = xc[pl.ds(r0, Q), :] * dt_exp
        parts = []
        for p in range(GW // 128):
            xp = xq[:, p * 128:(p + 1) * 128]
            xbd = jnp.concatenate([jnp.where(lo_half, xp, 0.0), jnp.where(lo_half, 0.0, xp)], axis=0)
            parts.append(_dot(amat[:, p * 128:(p + 1) * 128], xbd.astype(BF16)))
        y_intra = jnp.concatenate(parts, axis=1)
        s_t = st[d]
        y_inter = _dot(cq, s_t.astype(BF16)) * jnp.exp(cum)
        cum_last = cum[Q - 1:Q] if d == 0 else cum[0:1]
        xdec = (xq * jnp.exp(cum_last - cum)).astype(BF16)
        st[d] = s_t * jnp.exp(cum_last) + _dot_tn(bq, xdec)
        y_ref[pl.ds(r0, Q), :] += y_intra + y_inter

    def scan_body(i, carry):
        chunk_dir(i, 0)
        chunk_dir(nc - 1 - i, 1)
        return carry

    lax.fori_loop(0, nc, scan_body, 0, unroll=SSD_UNROLL)

    if emit_state:
        for d in range(2):
            so_ref[d] = st[d].T


def _ssd_branch(pm, ps, conv_w, conv_b, dt_bias, a_log, d_skip, s0, B, L, emit_state):
    use_s0 = s0 is not None
    G, GW = SSD_GROUPS, SSD_GW
    hpg = SSD_HEADS // G
    dtb = jnp.zeros((1, S_COLS), F32).at[0, S_DT:S_DT + 2 * SSD_HEADS].set(dt_bias.reshape(-1))
    alog = jnp.zeros((1, S_COLS), F32).at[0, S_DT:S_DT + 2 * SSD_HEADS].set(a_log.reshape(-1))
    dsk = jnp.repeat(d_skip, SSD_HEAD_DIM).reshape(1, SSD_INNER)
    cb2 = conv_b.reshape(1, SSD_CONV_DIM)
    nb_x = M_X // GW
    in_specs = [
        pl.BlockSpec((L, GW), lambda b, g: (b, nb_x + g)),
        pl.BlockSpec((L, SSD_STATE), lambda b, g: (b, M_B // SSD_STATE + g)),
        pl.BlockSpec((L, SSD_STATE), lambda b, g: (b, M_C // SSD_STATE + g)),
        pl.BlockSpec((L, S_COLS), lambda b, g: (b, 0)),
        pl.BlockSpec((SSD_CONV_W, GW), lambda b, g: (0, g)),
        pl.BlockSpec((SSD_CONV_W, SSD_STATE), lambda b, g: (0, SSD_INNER // SSD_STATE + g)),
        pl.BlockSpec((SSD_CONV_W, SSD_STATE), lambda b, g: (0, (SSD_INNER + SSD_GN) // SSD_STATE + g)),
        pl.BlockSpec((1, GW), lambda b, g: (0, g)),
        pl.BlockSpec((1, SSD_STATE), lambda b, g: (0, SSD_INNER // SSD_STATE + g)),
        pl.BlockSpec((1, SSD_STATE), lambda b, g: (0, (SSD_INNER + SSD_GN) // SSD_STATE + g)),
        pl.BlockSpec((1, S_COLS), lambda b, g: (0, 0)),
        pl.BlockSpec((1, S_COLS), lambda b, g: (0, 0)),
        pl.BlockSpec((1, GW), lambda b, g: (0, g)),
    ]
    args = [pm, pm, pm, ps, conv_w, conv_w, conv_w, cb2, cb2, cb2, dtb, alog, dsk]
    state_spec = pl.BlockSpec((None, 2, None, GW, SSD_STATE), lambda b, g: (b, 0, g, 0, 0))
    if use_s0:
        in_specs.append(state_spec)
        args.append(s0.reshape(B, 2, G, GW, SSD_STATE))
    out_specs = [pl.BlockSpec((L, GW), lambda b, g: (b, g))]
    out_shape = [jax.ShapeDtypeStruct((B * L, SSD_INNER), F32)]
    if emit_state:
        out_specs.append(state_spec)
        out_shape.append(jax.ShapeDtypeStruct((B, 2, G, GW, SSD_STATE), F32))
    plen = L + 2 * _CONV_LEAD
    outs = pl.pallas_call(
        functools.partial(_ssd_kernel, L=L, use_s0=use_s0, emit_state=emit_state),
        grid=(B, G),
        in_specs=in_specs,
        out_specs=out_specs,
        out_shape=out_shape,
        scratch_shapes=[
            pltpu.VMEM((plen, GW), BF16), pltpu.VMEM((plen, SSD_STATE), BF16), pltpu.VMEM((plen, SSD_STATE), BF16),
            pltpu.VMEM((L, GW), F32), pltpu.VMEM((L, SSD_STATE), F32), pltpu.VMEM((L, SSD_STATE), F32),
            pltpu.VMEM((L, S_COLS), F32), pltpu.VMEM((2, SSD_STATE, GW), F32),
            pltpu.VMEM((2, L, GW), F32), pltpu.VMEM((2, L, GW), F32),
        ],
        compiler_params=_cparams(("arbitrary", "arbitrary")),
        name="ssd_scan",
    )(*args)
    y = outs[0]
    state = outs[1].reshape(B, 2, SSD_HEADS, SSD_HEAD_DIM, SSD_STATE) if emit_state else None
    return y, state


def _ctx_attn_kernel(q_ref, k_ref, v_ref, o_ref):
    q = q_ref[...]
    L = q.shape[0]
    lane = lax.broadcasted_iota(jnp.int32, q.shape, 1)
    lo = lane < NA_HEAD_DIM
    zero = jnp.zeros_like(q)
    q2 = jnp.concatenate([jnp.where(lo, q, zero), jnp.where(lo, zero, q)], axis=0)
    s = _dot_nt(q2, k_ref[...]) * NA_SCALE
    m = jnp.max(s, axis=-1, keepdims=True)
    p = jnp.exp(s - m)
    den = jnp.sum(p, axis=-1, keepdims=True)
    o2 = _dot(p.astype(BF16), v_ref[...]) / den
    o_ref[...] = jnp.where(lo, o2[:L], o2[L:]).astype(o_ref.dtype)


def _context_attention(pm, B, L):
    nq, nk, nv = M_Q // 128, M_K // 128, M_V // 128
    return pl.pallas_call(
        _ctx_attn_kernel,
        grid=(B, NA_HEADS // 2),
        in_specs=[
            pl.BlockSpec((L, 128), lambda b, h: (b, nq + h)),
            pl.BlockSpec((L, 128), lambda b, h: (b, nk + h)),
            pl.BlockSpec((L, 128), lambda b, h: (b, nv + h)),
        ],
        out_specs=pl.BlockSpec((L, 128), lambda b, h: (b, h)),
        out_shape=jax.ShapeDtypeStruct((B * L, NA_INNER), BF16),
        compiler_params=_cparams(("arbitrary", "arbitrary")),
        name="context_attention",
    )(pm, pm, pm)


def _na_bias_table(rpb):
    qc = np.arange(GRID_W)
    cstart = np.clip(qc - NA_WIN_COLS // 2, 0, GRID_W - NA_WIN_COLS)
    kc = np.arange(GRID_W)
    valid = (kc[None, :] >= cstart[:, None]) & (kc[None, :] < cstart[:, None] + NA_WIN_COLS)
    dx = np.clip(kc[None, :] - qc[:, None] + NA_WIN_COLS - 1, 0, 2 * NA_WIN_COLS - 2)
    H, n_dr, n_dx = rpb.shape
    pick = (dx[None, :, :] == np.arange(n_dx)[:, None, None]).astype(np.float32).reshape(n_dx, GRID_W * GRID_W)
    cols = jnp.dot(rpb.reshape(H * n_dr, n_dx), jnp.asarray(pick), precision=lax.Precision.HIGHEST)
    cols = cols.reshape(H, n_dr, GRID_W, GRID_W)
    t = jnp.stack([cols[:, d0:d0 + NA_WIN_ROWS] for d0 in range(NA_WIN_ROWS)], axis=1)
    t = t.transpose(0, 1, 3, 2, 4)
    t = jnp.where(jnp.asarray(valid)[None, None, :, None, :], t, NA_NEG)
    t = t.reshape(H // 2, 2, NA_WIN_ROWS, GRID_W, NA_WIN_ROWS * GRID_W).astype(F32)
    return t.transpose(0, 2, 1, 3, 4).reshape(H // 2, NA_WIN_ROWS, 2 * GRID_W, NA_WIN_ROWS * GRID_W)


def _na_kernel(q_ref, k_ref, v_ref, ck_ref, cv_ref, bias_ref, o_ref, ckb, cvb, *, rows):
    ckb[...] = ck_ref[...].astype(BF16)
    cvb[...] = cv_ref[...].astype(BF16)
    wr = NA_WIN_ROWS
    nloc = wr * GRID_W
    lo = lax.broadcasted_iota(jnp.int32, (GRID_W, 128), 1) < NA_HEAD_DIM

    def row_body(r, carry):
        rs = jnp.clip(r - wr // 2, 0, rows - wr)
        d0 = rs - r + wr - 1
        q = q_ref[pl.ds(pl.multiple_of(r * GRID_W, GRID_W), GRID_W), :]
        k0 = pl.multiple_of(rs * GRID_W, GRID_W)
        kw = k_ref[pl.ds(k0, nloc), :]
        vw = v_ref[pl.ds(k0, nloc), :]
        zero = jnp.zeros_like(q)
        q2 = jnp.concatenate([jnp.where(lo, q, zero), jnp.where(lo, zero, q)], axis=0)
        s_loc = _dot_nt(q2, kw) * NA_SCALE + bias_ref[d0]
        s_ctx = _dot_nt(q2, ckb[...]) * NA_SCALE
        m = jnp.maximum(jnp.max(s_loc, axis=-1, keepdims=True), jnp.max(s_ctx, axis=-1, keepdims=True))
        p_loc = jnp.exp(s_loc - m)
        p_ctx = jnp.exp(s_ctx - m)
        den = jnp.sum(p_loc, axis=-1, keepdims=True) + jnp.sum(p_ctx, axis=-1, keepdims=True)
        o2 = (_dot(p_loc.astype(BF16), vw) + _dot(p_ctx.astype(BF16), cvb[...])) / den
        o = jnp.where(lo, o2[:GRID_W], o2[GRID_W:])
        o_ref[pl.ds(pl.multiple_of(r * GRID_W, GRID_W), GRID_W), :] = o.astype(o_ref.dtype)
        return carry

    lax.fori_loop(0, rows, row_body, 0, unroll=NA_UNROLL)


def _neighbourhood_attention(pm, ck, cv, bias_tab, B, S):
    rows = S // GRID_W
    assert rows >= NA_WIN_ROWS
    Lc = ck.shape[1]
    nq, nk, nv = M_Q // 128, M_K // 128, M_V // 128
    return pl.pallas_call(
        functools.partial(_na_kernel, rows=rows),
        grid=(NA_HEADS // 2, B),
        in_specs=[
            pl.BlockSpec((S, 128), lambda h, b: (b, nq + h)),
            pl.BlockSpec((S, 128), lambda h, b: (b, nk + h)),
            pl.BlockSpec((S, 128), lambda h, b: (b, nv + h)),
            pl.BlockSpec((None, Lc, 128), lambda h, b: (b, 0, h)),
            pl.BlockSpec((None, Lc, 128), lambda h, b: (b, 0, h)),
            pl.BlockSpec((None, NA_WIN_ROWS, 2 * GRID_W, NA_WIN_ROWS * GRID_W), lambda h, b: (h, 0, 0, 0)),
        ],
        out_specs=pl.BlockSpec((S, 128), lambda h, b: (b, h)),
        out_shape=jax.ShapeDtypeStruct((B * S, NA_INNER), BF16),
        scratch_shapes=[pltpu.VMEM((Lc, 128), BF16), pltpu.VMEM((Lc, 128), BF16)],
        compiler_params=_cparams(("arbitrary", "arbitrary")),
        name="neighbourhood_attention",
    )(pm, pm, pm, ck, cv, bias_tab)


def _gla_kernel(*refs, L, use_s0, emit_state):
    it = iter(refs)
    q_ref, k_ref, v_ref, r_ref, ps_ref, wg_ref, bg_ref, ng_ref = (next(it) for _ in range(8))
    s0_ref = next(it) if use_s0 else None
    o_ref = next(it)
    so_ref = next(it) if emit_state else None
    acc, st = next(it), next(it)

    Q, H, DK, DV = GLA_Q, GLA_HEADS, GLA_DK_HEAD, GLA_DV_HEAD
    HQ = H * Q
    nc = L // Q

    def zero_body(c, carry):
        acc[c] = jnp.zeros((HQ, DV), F32)
        return carry

    lax.fori_loop(0, nc, zero_body, 0)

    for d in range(2):
        for h in range(H):
            if use_s0:
                st[d, :, h * DK:(h + 1) * DK] = s0_ref[d, h].T
            else:
                st[d, :, h * DK:(h + 1) * DK] = jnp.zeros((DV, DK), F32)

    SC = GLA_SUPER
    G = SC * Q
    R = SC * HQ
    nsc = L // G
    gi = lax.broadcasted_iota(jnp.int32, (G, G), 0)
    gj = lax.broadcasted_iota(jnp.int32, (G, G), 1)
    same_chunk = (gi // Q) == (gj // Q)
    tris = (jnp.logical_and(same_chunk, gj <= gi).astype(BF16), jnp.logical_and(same_chunk, gj >= gi).astype(BF16))
    ends = ((gj == (gi // Q) * Q + (Q - 1)).astype(BF16), (gj == (gi // Q) * Q).astype(BF16))
    row_k = lax.broadcasted_iota(jnp.int32, (R, H * DK), 0)
    lane_k = lax.broadcasted_iota(jnp.int32, (R, H * DK), 1)
    own_k = ((row_k // Q) % H) == (lane_k // DK)
    row_s = lax.broadcasted_iota(jnp.int32, (R, G), 0)
    col_s = lax.broadcasted_iota(jnp.int32, (R, G), 1)
    in_chunk = (row_s // HQ) == (col_s // Q)
    causal = (jnp.logical_and(in_chunk, (col_s % Q) <= (row_s % Q)),
              jnp.logical_and(in_chunk, (col_s % Q) >= (row_s % Q)))
    rep_r = lax.broadcasted_iota(jnp.int32, (G, R), 0)
    rep_c = lax.broadcasted_iota(jnp.int32, (G, R), 1)
    rep = jnp.logical_and(rep_r // Q == rep_c // HQ, rep_r % Q == rep_c % Q).astype(BF16)
    row_a = lax.broadcasted_iota(jnp.int32, (R, R), 0)
    lane_a = lax.broadcasted_iota(jnp.int32, (R, R), 1)
    own_a = (row_a // Q) == (lane_a // Q)

    def stack_heads(x):
        return jnp.concatenate([x[c * Q:(c + 1) * Q] for c in range(SC) for _ in range(H)], axis=0)

    def super_dir(u, d):
        r0 = pl.multiple_of(u * G, G)
        lr = ps_ref[pl.ds(r0, G), :].astype(BF16)
        gpre = (_dot(lr, wg_ref[:, d * GLA_DK:(d + 1) * GLA_DK].astype(BF16))
                + bg_ref[:, d * GLA_DK:(d + 1) * GLA_DK])
        g = (jnp.minimum(gpre, 0.0) - jnp.log1p(jnp.exp(-jnp.abs(gpre)))) / GLA_TAU
        bcum = _sel_dot_l(tris[d], g)
        bend = _sel_dot_l(ends[d], bcum)
        qc = q_ref[pl.ds(r0, G), :].astype(F32) * GLA_QK_SCALE
        kc = k_ref[pl.ds(r0, G), :].astype(F32)
        vc = v_ref[pl.ds(r0, G), :]
        qt = (qc * jnp.exp(bcum)).astype(BF16)
        kt = (kc * jnp.exp(-bcum)).astype(BF16)
        kend = (kc * jnp.exp(bend - bcum)).astype(BF16)
        zk = jnp.zeros((R, H * DK), BF16)
        q16 = jnp.where(own_k, stack_heads(qt), zk)
        k16 = jnp.where(own_k, stack_heads(kend), zk)
        v16 = jnp.concatenate([vc[c * Q:(c + 1) * Q, h * DV:(h + 1) * DV] for c in range(SC) for h in range(H)],
                              axis=0)
        att = jnp.where(causal[d], _dot_nt(q16, kt), 0.0)
        a16 = jnp.where(own_a, _dot(att.astype(BF16), rep), 0.0).astype(BF16)
        o_intra = _dot(a16, v16)
        decay = jnp.exp(bend)
        order = range(SC) if d == 0 else range(SC - 1, -1, -1)
        for c in order:
            q4 = q16[c * HQ:(c + 1) * HQ]
            s_t = st[d]
            o4 = o_intra[c * HQ:(c + 1) * HQ] + _dot_nt(q4, s_t.astype(BF16))
            st[d] = s_t * decay[c * Q:c * Q + 1] + _dot_tn(v16[c * HQ:(c + 1) * HQ], k16[c * HQ:(c + 1) * HQ])
            acc[u * SC + c] += o4

    def scan_body(i, carry):
        super_dir(i, 0)
        super_dir(nsc - 1 - i, 1)
        return carry

    lax.fori_loop(0, nsc, scan_body, 0)

    def fin_body(c, carry):
        r0 = pl.multiple_of(c * Q, Q)
        o = acc[c]
        o = o * lax.rsqrt(jnp.mean(o * o, axis=-1, keepdims=True) + EPS) * ng_ref[...]
        for h in range(H):
            gate = _silu(r_ref[pl.ds(r0, Q), h * DV:(h + 1) * DV].astype(F32))
            o_ref[pl.ds(r0, Q), h * DV:(h + 1) * DV] = (o[h * Q:(h + 1) * Q] * gate).astype(o_ref.dtype)
        return carry

    lax.fori_loop(0, nc, fin_body, 0)

    if emit_state:
        for d in range(2):
            for h in range(H):
                so_ref[d, h] = st[d, :, h * DK:(h + 1) * DK].T


def _gla_branch(pm, ps, w_gate, b_gate, norm_g, s0, B, L, emit_state):
    use_s0 = s0 is not None
    H, DK, DV = GLA_HEADS, GLA_DK_HEAD, GLA_DV_HEAD
    wg = jnp.zeros((S_COLS, 2 * GLA_DK), F32)
    for d in range(2):
        wg = wg.at[S_LR + d * GLA_LOWRANK:S_LR + (d + 1) * GLA_LOWRANK, d * GLA_DK:(d + 1) * GLA_DK].set(w_gate[d])
    bg = b_gate.reshape(1, 2 * GLA_DK)
    in_specs = [
        pl.BlockSpec((L, GLA_DK), lambda b: (b, M_GQ // GLA_DK)),
        pl.BlockSpec((L, GLA_DK), lambda b: (b, M_GK // GLA_DK)),
        pl.BlockSpec((L, GLA_DV), lambda b: (b, M_GV // GLA_DV)),
        pl.BlockSpec((L, GLA_DV), lambda b: (b, M_R // GLA_DV)),
        pl.BlockSpec((L, S_COLS), lambda b: (b, 0)),
        pl.BlockSpec((S_COLS, 2 * GLA_DK), lambda b: (0, 0)),
        pl.BlockSpec((1, 2 * GLA_DK), lambda b: (0, 0)),
        pl.BlockSpec((1, DV), lambda b: (0, 0)),
    ]
    args = [pm, pm, pm, pm, ps, wg, bg, norm_g.reshape(1, DV)]
    state_spec = pl.BlockSpec((None, 2, H, DK, DV), lambda b: (b, 0, 0, 0, 0))
    if use_s0:
        in_specs.append(state_spec)
        args.append(s0)
    out_specs = [pl.BlockSpec((L, GLA_DV), lambda b: (b, 0))]
    out_shape = [jax.ShapeDtypeStruct((B * L, GLA_DV), BF16)]
    if emit_state:
        out_specs.append(state_spec)
        out_shape.append(jax.ShapeDtypeStruct((B, 2, H, DK, DV), F32))
    outs = pl.pallas_call(
        functools.partial(_gla_kernel, L=L, use_s0=use_s0, emit_state=emit_state),
        grid=(B,),
        in_specs=in_specs,
        out_specs=out_specs,
        out_shape=out_shape,
        scratch_shapes=[pltpu.VMEM((L // GLA_Q, H * GLA_Q, DV), F32), pltpu.VMEM((2, DV, H * DK), F32)],
        compiler_params=_cparams(("arbitrary",)),
        name="gla_scan",
    )(*args)
    return outs[0], (outs[1] if emit_state else None)


def _merge_kernel(*refs, aliased):
    (x_ref, yssd_ref, z_ref, ona_ref, ogla_ref, g0_ref, g1_ref, g2_ref,
     gate1_ref, shift2_ref, scale2_ref, sng_ref, n2g_ref, wb_ref, wo_ref, rw_ref, rb_ref) = refs[:17]
    xo_ref, h2_ref, lg_ref = refs[-3:]
    y = yssd_ref[...] * _silu(z_ref[...].astype(F32))
    y = y * lax.rsqrt(jnp.mean(y * y, axis=-1, keepdims=True) + EPS) * sng_ref[...]
    m = jax.nn.sigmoid(g0_ref[...].astype(F32)) * _dot(y.astype(BF16), wb_ref[0])
    m = m + jax.nn.sigmoid(g1_ref[...].astype(F32)) * _dot(ona_ref[...], wb_ref[1])
    m = m + jax.nn.sigmoid(g2_ref[...].astype(F32)) * _dot(ogla_ref[...], wb_ref[2])
    x = x_ref[...] + gate1_ref[...] * _dot(m.astype(BF16), wo_ref[...])
    xo_ref[...] = x
    h2 = x * lax.rsqrt(jnp.mean(x * x, axis=-1, keepdims=True) + EPS) * n2g_ref[...]
    h2 = h2 * (1.0 + scale2_ref[...]) + shift2_ref[...]
    h2_ref[...] = h2
    lg_ref[...] = jnp.dot(h2, rw_ref[...], preferred_element_type=F32,
                          precision=lax.Precision.HIGHEST) + rb_ref[...]


def _merge(x2, y_ssd, pm, o_na, o_gla, mod_l, ssd_norm_g, norm2_g, wb_bf, wo_bf, rw_pad, rb_pad,
           rows_per_mod, mod_row0, t_all, row0, shared):
    T = x2.shape[0]
    tm = min(256, rows_per_mod)
    per = rows_per_mod // tm
    D = D_MODEL
    ng = M_GATES // D
    blk0 = row0 // tm

    def mod_map(col):
        return lambda i: (mod_row0 + i // per, 0, col)

    row = lambda i: (i, 0)
    row_off = lambda i: (blk0 + i, 0)
    const2 = lambda i: (0, 0)
    in_specs = [
        pl.BlockSpec((tm, D), row),
        pl.BlockSpec((tm, D), row),
        pl.BlockSpec((tm, D), lambda i: (i, M_Z // D)),
        pl.BlockSpec((tm, D), row),
        pl.BlockSpec((tm, D), row),
        pl.BlockSpec((tm, D), lambda i: (i, ng)),
        pl.BlockSpec((tm, D), lambda i: (i, ng + 1)),
        pl.BlockSpec((tm, D), lambda i: (i, ng + 2)),
        pl.BlockSpec((None, 1, D), mod_map(2)),
        pl.BlockSpec((None, 1, D), mod_map(3)),
        pl.BlockSpec((None, 1, D), mod_map(4)),
        pl.BlockSpec((1, D), const2),
        pl.BlockSpec((1, D), const2),
        pl.BlockSpec((N_BRANCH, D, D), lambda i: (0, 0, 0)),
        pl.BlockSpec((D, D), const2),
        pl.BlockSpec((D, 128), const2),
        pl.BlockSpec((1, 128), const2),
    ]
    args = [x2, y_ssd, pm, o_na, o_gla, pm, pm, pm, mod_l, mod_l, mod_l,
            ssd_norm_g.reshape(1, D), norm2_g.reshape(1, D), wb_bf, wo_bf, rw_pad, rb_pad]
    aliases = {}
    if shared is not None:
        in_specs += [pl.BlockSpec(memory_space=pl.ANY), pl.BlockSpec(memory_space=pl.ANY)]
        aliases = {len(args): 1, len(args) + 1: 2}
        args += list(shared)
    return pl.pallas_call(
        functools.partial(_merge_kernel, aliased=shared is not None),
        grid=(T // tm,),
        in_specs=in_specs,
        out_specs=[pl.BlockSpec((tm, D), row), pl.BlockSpec((tm, D), row_off), pl.BlockSpec((tm, 128), row_off)],
        out_shape=[jax.ShapeDtypeStruct((T, D), F32), jax.ShapeDtypeStruct((t_all, D), F32),
                   jax.ShapeDtypeStruct((t_all, 128), F32)],
        input_output_aliases=aliases,
        compiler_params=_cparams(("arbitrary",)),
        name="branch_merge",
    )(*args)


def _moe_kernel(be_ref, nused_ref, dst_ref, dstn_ref, h2_hbm,
                wgu_ref, bgu_ref, wdn_ref, bdn_ref, y_hbm, xbuf, obuf, wgu_bf, wdn_bf, gsem, ssem):
    b = pl.program_id(0)
    nused = nused_ref[0]
    slot = b % 2
    n_tok = h2_hbm.shape[0]

    def token_of(row):
        if n_tok & (n_tok - 1) == 0:
            return jnp.bitwise_and(row, n_tok - 1)
        return lax.rem(row, n_tok)

    def start_gather(idx_ref, s):
        def body(i, carry):
            pltpu.make_async_copy(h2_hbm.at[pl.ds(token_of(idx_ref[0, i]), 1), :], xbuf.at[s, pl.ds(i, 1), :],
                                  gsem.at[s]).start()
            return carry
        lax.fori_loop(0, MOE_ROWS, body, 0, unroll=MOE_DMA_UNROLL)

    def wait_gather(s):
        pltpu.make_async_copy(h2_hbm.at[pl.ds(0, MOE_ROWS), :], xbuf.at[s], gsem.at[s]).wait()

    def wait_scatter():
        pltpu.make_async_copy(obuf, y_hbm.at[pl.ds(0, MOE_ROWS), :], ssem.at[0]).wait()

    @pl.when(b == 0)
    def _():
        start_gather(dst_ref, 0)

    @pl.when(b < nused)
    def _():
        wait_gather(slot)

        prev = be_ref[jnp.maximum(b - 1, 0)]

        @pl.when(jnp.logical_or(b == 0, be_ref[b] != prev))
        def _():
            rc = 128

            def cast_body(i, carry):
                r0 = pl.multiple_of(i * rc, rc)
                wgu_bf[pl.ds(r0, rc), :] = wgu_ref[pl.ds(r0, rc), :].astype(BF16)
                wdn_bf[pl.ds(r0, rc), :] = wdn_ref[pl.ds(r0, rc), :].astype(BF16)
                return carry

            lax.fori_loop(0, D_MODEL // rc, cast_body, 0)

        for i in range(MOE_ROWS):
            pltpu.make_async_copy(h2_hbm.at[pl.ds(token_of(dstn_ref[0, i]), 1), :],
                                  xbuf.at[1 - slot, pl.ds(i, 1), :], gsem.at[1 - slot]).start()

        x = xbuf[slot].astype(BF16)
        acc = jnp.zeros((MOE_ROWS, D_MODEL), F32) + bdn_ref[...]
        for f in range(D_FF // MOE_FF_TILE):
            c0 = f * MOE_FF_TILE
            glu = _dot(x, wgu_bf[:, c0:c0 + MOE_FF_TILE]) + bgu_ref[:, c0:c0 + MOE_FF_TILE]
            lin = _dot(x, wgu_bf[:, D_FF + c0:D_FF + c0 + MOE_FF_TILE]) + bgu_ref[:, D_FF + c0:D_FF + c0 + MOE_FF_TILE]
            glu = jnp.minimum(glu, SWIGLU_LIMIT)
            lin = jnp.clip(lin, -SWIGLU_LIMIT, SWIGLU_LIMIT)
            act = glu * jax.nn.sigmoid(SWIGLU_ALPHA * glu) * (lin + 1.0)
            acc = acc + _dot(act.astype(BF16), wdn_bf[c0:c0 + MOE_FF_TILE, :])

        @pl.when(b > 0)
        def _():
            wait_scatter()

        obuf[...] = acc

        def sc_body(i, carry):
            pltpu.make_async_copy(obuf.at[pl.ds(i, 1), :], y_hbm.at[pl.ds(dst_ref[0, i], 1), :], ssem.at[0]).start()
            return carry

        lax.fori_loop(0, MOE_ROWS, sc_body, 0, unroll=MOE_DMA_UNROLL)

        @pl.when(b == nused - 1)
        def _():
            wait_scatter()
            wait_gather(1 - slot)


def _moe_experts(h2_all, slot_dst, block_e, n_used, w_gu, b_gu, w_dn, b_dn, layer):
    n_blocks = slot_dst.shape[0]
    T, D = h2_all.shape
    smem_blk = lambda f: pl.BlockSpec((None, 1, MOE_ROWS), f, memory_space=pltpu.SMEM)
    grid_spec = pltpu.PrefetchScalarGridSpec(
        num_scalar_prefetch=2,
        grid=(n_blocks,),
        in_specs=[
            smem_blk(lambda b, be, nu: (b, 0, 0)),
            smem_blk(lambda b, be, nu: (jnp.minimum(b + 1, n_blocks - 1), 0, 0)),
            pl.BlockSpec(memory_space=pl.ANY),
            pl.BlockSpec((None, None, D, 2 * D_FF), lambda b, be, nu: (layer, be[b], 0, 0)),
            pl.BlockSpec((None, None, 1, 2 * D_FF), lambda b, be, nu: (layer, be[b], 0, 0)),
            pl.BlockSpec((None, None, D_FF, D), lambda b, be, nu: (layer, be[b], 0, 0)),
            pl.BlockSpec((None, None, 1, D), lambda b, be, nu: (layer, be[b], 0, 0)),
        ],
        out_specs=pl.BlockSpec(memory_space=pl.ANY),
        scratch_shapes=[pltpu.VMEM((2, MOE_ROWS, D), F32), pltpu.VMEM((MOE_ROWS, D), F32),
                        pltpu.VMEM((D, 2 * D_FF), BF16), pltpu.VMEM((D_FF, D), BF16),
                        pltpu.SemaphoreType.DMA((2,)), pltpu.SemaphoreType.DMA((1,))],
    )
    return pl.pallas_call(
        _moe_kernel,
        grid_spec=grid_spec,
        out_shape=jax.ShapeDtypeStruct((TOP_K * T + MOE_ROWS, D), F32),
        compiler_params=_cparams(("arbitrary",)),
        name="moe_experts",
    )(block_e, n_used, slot_dst, slot_dst, h2_all, w_gu,
      b_gu.reshape(DEPTH, N_EXPERTS, 1, 2 * D_FF), w_dn, b_dn.reshape(DEPTH, N_EXPERTS, 1, D))


def _moe_route(logits):
    T = logits.shape[0]
    TK = T * TOP_K
    top_logit, top_idx = lax.top_k(logits, TOP_K)
    top_w = jax.nn.softmax(top_logit, axis=-1)
    flat_e = top_idx.reshape(TK).astype(jnp.int32)
    onehot = (flat_e[:, None] == jnp.arange(N_EXPERTS, dtype=jnp.int32)[None, :]).astype(jnp.int32)
    csum = jnp.cumsum(onehot, axis=0)
    rank = jnp.sum(onehot * csum, axis=1) - 1
    counts = csum[-1]
    padded = (counts + MOE_ROWS - 1) // MOE_ROWS * MOE_ROWS
    pad_end = jnp.cumsum(padded)
    pad_start = pad_end - padded
    dest = jnp.sum(onehot * pad_start[None, :], axis=1) + rank
    n_blocks = -(-(TK + N_EXPERTS * (MOE_ROWS - 1)) // MOE_ROWS)
    n_slots = n_blocks * MOE_ROWS
    flat = jnp.arange(TK, dtype=jnp.int32)
    spare = TK + jnp.arange(n_slots, dtype=jnp.int32) % MOE_ROWS
    slot_dst = spare.at[dest].set((flat % TOP_K) * T + flat // TOP_K)
    blk_start = jnp.arange(n_blocks, dtype=jnp.int32) * MOE_ROWS
    block_e = jnp.sum((blk_start[:, None] >= pad_end[None, :]).astype(jnp.int32), axis=1)
    block_e = jnp.minimum(block_e, N_EXPERTS - 1).astype(jnp.int32)
    n_used = (pad_end[-1] // MOE_ROWS).astype(jnp.int32).reshape(1)
    top_w_pad = jnp.pad(top_w, ((0, 0), (0, 128 - TOP_K)))
    return top_w_pad, slot_dst.reshape(n_blocks, 1, MOE_ROWS), block_e, n_used


def _combine_kernel(x_ref, y0_ref, y1_ref, y2_ref, y3_ref, w_ref, gate_ref, o_ref):
    w = w_ref[...]
    y = w[:, 0:1] * y0_ref[...]
    for k, y_ref in enumerate((y1_ref, y2_ref, y3_ref), start=1):
        y = y + w[:, k:k + 1] * y_ref[...]
    o_ref[...] = x_ref[...] + gate_ref[...] * y


def _moe_combine(x2, y_rows, top_w_pad, mod_l, rows_per_mod, mod_row0, row0, t_all):
    assert TOP_K == 4
    T, D = x2.shape
    tm = min(512, rows_per_mod)
    per = rows_per_mod // tm
    blk0 = row0 // tm
    nblk = t_all // tm

    def y_map(k):
        return lambda i: (k * nblk + blk0 + i, 0)

    return pl.pallas_call(
        _combine_kernel,
        grid=(T // tm,),
        in_specs=[pl.BlockSpec((tm, D), lambda i: (i, 0))]
        + [pl.BlockSpec((tm, D), y_map(k)) for k in range(TOP_K)]
        + [pl.BlockSpec((tm, 128), lambda i: (blk0 + i, 0)),
           pl.BlockSpec((None, 1, D), lambda i: (mod_row0 + i // per, 0, 5))],
        out_specs=pl.BlockSpec((tm, D), lambda i: (i, 0)),
        out_shape=jax.ShapeDtypeStruct((T, D), F32),
        compiler_params=_cparams(("arbitrary",)),
        name="moe_combine",
    )(x2, y_rows, y_rows, y_rows, y_rows, top_w_pad, mod_l)


def _final_norm_kernel(x_ref, g_ref, o_ref):
    x = x_ref[...]
    o_ref[...] = x * lax.rsqrt(jnp.mean(x * x, axis=-1, keepdims=True) + EPS) * g_ref[...]


def _final_norm(x2, g):
    T = x2.shape[0]
    tm = 512
    return pl.pallas_call(
        _final_norm_kernel,
        grid=(T // tm,),
        in_specs=[pl.BlockSpec((tm, D_MODEL), lambda i: (i, 0)), pl.BlockSpec((1, D_MODEL), lambda i: (0, 0))],
        out_specs=pl.BlockSpec((tm, D_MODEL), lambda i: (i, 0)),
        out_shape=jax.ShapeDtypeStruct((T, D_MODEL), F32),
        compiler_params=_cparams(("arbitrary",)),
        name="final_norm",
    )(x2, g.reshape(1, D_MODEL))


def _mixer_half(x2, B, L, mod_l, mod_row0, lw, ctx, t_all, row0, shared):
    latent = ctx is not None
    rows_per_mod = L if latent else B * L
    pm, ps = _in_projection(x2, mod_l, lw["norm1_g"], lw["w_in"], rows_per_mod, mod_row0)
    if latent:
        ck, cv, s_ssd0, s_gla0 = ctx
    else:
        s_ssd0 = s_gla0 = None
    y_ssd, s_ssd = _ssd_branch(pm, ps, lw["ssd_conv_w"], lw["ssd_conv_b"], lw["ssd_dt_bias"], lw["ssd_a_log"],
                               lw["ssd_d"], s_ssd0, B, L, emit_state=not latent)
    if latent:
        o_na = _neighbourhood_attention(pm, ck, cv, lw["na_bias"], B, L)
    else:
        o_na = _context_attention(pm, B, L)
    o_gla, s_gla = _gla_branch(pm, ps, lw["gla_w_gate"], lw["gla_b_gate"], lw["gla_norm_g"], s_gla0, B, L,
                               emit_state=not latent)
    x_new, h2, logits = _merge(x2, y_ssd, pm, o_na, o_gla, mod_l, lw["ssd_norm_g"], lw["norm2_g"],
                               lw["w_branch"], lw["w_out"], lw["router_w"], lw["router_b"], rows_per_mod, mod_row0,
                               t_all, row0, shared)
    return x_new, h2, logits, pm, s_ssd, s_gla


def kernel(x_prompt, x_sample, cache_na_k, cache_na_v, state_ssd, state_gla, c, c_ctx, w_ada, b_ada, norm1_g, norm2_g, w_in, ssd_conv_w, ssd_conv_b, ssd_dt_bias, ssd_a_log, ssd_d, ssd_norm_g, na_rpb, gla_w_gate, gla_b_gate, gla_norm_g, w_branch, w_out, router_w, router_b, moe_w_gu, moe_b_gu, moe_w_dn, moe_b_dn, final_norm_g):
    Bp, Lp, D = x_prompt.shape
    Bs, Ls, _ = x_sample.shape
    Tp, Ts = Bp * Lp, Bs * Ls
    Lc = cache_na_k.shape[2]

    cvecs = jnp.concatenate([c_ctx[None], c, jnp.zeros((8 - 1 - Bs, D), F32)], axis=0)
    mod = _modulation(cvecs, w_ada, b_ada).reshape(DEPTH, 8, 1, 6 * D)

    xp = x_prompt.reshape(Tp, D)
    xs = x_sample.reshape(Ts, D)
    ks_, vs_, sss_, sgs_ = [], [], [], []
    for l in range(DEPTH):
        lw = {
            "norm1_g": norm1_g[l], "norm2_g": norm2_g[l], "w_in": _relayout_w_in(w_in[l]),
            "ssd_conv_w": ssd_conv_w[l], "ssd_conv_b": ssd_conv_b[l], "ssd_dt_bias": ssd_dt_bias[l],
            "ssd_a_log": ssd_a_log[l], "ssd_d": ssd_d[l], "ssd_norm_g": ssd_norm_g[l],
            "na_bias": _na_bias_table(na_rpb[l]),
            "gla_w_gate": gla_w_gate[l], "gla_b_gate": gla_b_gate[l], "gla_norm_g": gla_norm_g[l],
            "w_branch": w_branch[l].astype(BF16), "w_out": w_out[l].astype(BF16),
            "router_w": jnp.pad(router_w[l], ((0, 0), (0, 128 - N_EXPERTS))),
            "router_b": jnp.pad(router_b[l], (0, 128 - N_EXPERTS)).reshape(1, 128),
        }
        mod_l = mod[l]
        t_all = Tp + Ts
        xp, h2a, lga, pmp, s_ssd, s_gla = _mixer_half(xp, Bp, Lp, mod_l, 0, lw, None, t_all, 0, None)
        ctx = (cache_na_k[:, l].reshape(Bs, Lc, NA_INNER), cache_na_v[:, l].reshape(Bs, Lc, NA_INNER),
               state_ssd[:, l], state_gla[:, l])
        xs, h2a, lga, _, _, _ = _mixer_half(xs, Bs, Ls, mod_l, 1, lw, ctx, t_all, Tp, (h2a, lga))
        ks_.append(pmp[:, M_K:M_K + NA_INNER].astype(F32).reshape(Bp, Lp, NA_HEADS, NA_HEAD_DIM))
        vs_.append(pmp[:, M_V:M_V + NA_INNER].astype(F32).reshape(Bp, Lp, NA_HEADS, NA_HEAD_DIM))
        sss_.append(s_ssd)
        sgs_.append(s_gla)

        top_w, slot_dst, block_e, n_used = _moe_route(lga[:, :N_EXPERTS])
        y_rows = _moe_experts(h2a, slot_dst, block_e, n_used, moe_w_gu, moe_b_gu, moe_w_dn, moe_b_dn, l)
        xp = _moe_combine(xp, y_rows, top_w, mod_l, Tp, 0, 0, t_all)
        xs = _moe_combine(xs, y_rows, top_w, mod_l, Ls, 1, Tp, t_all)

    y_prompt = _final_norm(xp, final_norm_g).reshape(Bp, Lp, D)
    y_sample = _final_norm(xs, final_norm_g).reshape(Bs, Ls, D)
    return (y_prompt, y_sample, jnp.stack(ks_, axis=1), jnp.stack(vs_, axis=1),
            jnp.stack(sss_, axis=1), jnp.stack(sgs_, axis=1))
```

```python
import functools

import numpy as np
import jax
import jax.numpy as jnp
from jax import lax
from jax.experimental import pallas as pl
from jax.experimental.pallas import tpu as pltpu

F32 = jnp.float32
BF16 = jnp.bfloat16

D_MODEL = 1024
DEPTH = 4
GRID_W = 64
EPS = 1e-6

SSD_HEADS = 16
SSD_HEAD_DIM = 64
SSD_INNER = SSD_HEADS * SSD_HEAD_DIM
SSD_GROUPS = 2
SSD_STATE = 128
SSD_GN = SSD_GROUPS * SSD_STATE
SSD_CONV_W = 5
SSD_CONV_DIM = SSD_INNER + 2 * SSD_GN
SSD_Q = 64
SSD_GW = SSD_INNER // SSD_GROUPS

NA_HEADS = 16
NA_HEAD_DIM = 64
NA_INNER = NA_HEADS * NA_HEAD_DIM
NA_WIN_ROWS = 8
NA_WIN_COLS = 16
NA_SCALE = NA_HEAD_DIM ** -0.5
NA_NEG = -1e30

GLA_HEADS = 4
GLA_DK = D_MODEL // 2
GLA_DV = D_MODEL
GLA_DK_HEAD = GLA_DK // GLA_HEADS
GLA_DV_HEAD = GLA_DV // GLA_HEADS
GLA_LOWRANK = 16
GLA_TAU = 16.0
GLA_Q = 32
GLA_QK_SCALE = GLA_DK_HEAD ** -0.5

N_BRANCH = 3
N_EXPERTS = 32
TOP_K = 4
D_FF = D_MODEL
SWIGLU_ALPHA = 1.702
SWIGLU_LIMIT = 7.0
MOE_ROWS = 512
MOE_FF_TILE = 512
MOE_DMA_UNROLL = 8

SSD_UNROLL = 2
SSD_DECAY_ROWS = 256
NA_UNROLL = 4
GLA_SUPER = 4

OFF_SSD_XBC = SSD_INNER
OFF_SSD_DT = OFF_SSD_XBC + SSD_CONV_DIM
OFF_NA_QKV = OFF_SSD_DT + 2 * SSD_HEADS
OFF_GLA_Q = OFF_NA_QKV + 3 * NA_INNER
OFF_GLA_LR = OFF_GLA_Q + 2 * GLA_DK + 2 * GLA_DV
OFF_GATES = OFF_GLA_LR + 2 * GLA_LOWRANK
IN_COLS = OFF_GATES + N_BRANCH * D_MODEL

M_Z = 0
M_X = M_Z + SSD_INNER
M_Q = M_X + SSD_INNER
M_K = M_Q + NA_INNER
M_V = M_K + NA_INNER
M_GV = M_V + NA_INNER
M_R = M_GV + GLA_DV
M_GATES = M_R + GLA_DV
M_GQ = M_GATES + N_BRANCH * D_MODEL
M_GK = M_GQ + GLA_DK
M_B = M_GK + GLA_DK
M_C = M_B + SSD_GN
M_COLS = M_C + SSD_GN
_MAIN_SEGMENTS = (
    (0, 2 * SSD_INNER),
    (OFF_NA_QKV, 3 * NA_INNER),
    (OFF_GLA_Q + 2 * GLA_DK, 2 * GLA_DV),
    (OFF_GATES, N_BRANCH * D_MODEL),
    (OFF_GLA_Q, 2 * GLA_DK),
    (OFF_SSD_XBC + SSD_INNER, 2 * SSD_GN),
)
_SMALL_SEGMENTS = ((OFF_SSD_DT, 2 * SSD_HEADS), (OFF_GLA_LR, 2 * GLA_LOWRANK))
S_DT = 0
S_LR = 2 * SSD_HEADS
S_COLS = 128
PROJ_TN = 512
PROJ_COLS = M_COLS + PROJ_TN
PROJ_TM = 2048

VMEM_LIMIT = 56 * 1024 * 1024


def _cparams(sem):
    return pltpu.CompilerParams(dimension_semantics=sem, vmem_limit_bytes=VMEM_LIMIT)


def _silu(x):
    return x * jax.nn.sigmoid(x)


def _softplus(x):
    return jnp.maximum(x, 0.0) + jnp.log1p(jnp.exp(-jnp.abs(x)))


def _split3(x):
    hi = x.astype(BF16)
    r1 = x - hi.astype(F32)
    mid = r1.astype(BF16)
    lo = (r1 - mid.astype(F32)).astype(BF16)
    return hi, mid, lo


def _dot(a, b):
    return jnp.dot(a, b, preferred_element_type=F32)


def _dot_nt(a, b):
    return lax.dot_general(a, b, (((1,), (1,)), ((), ())), preferred_element_type=F32)


def _dot_tn(a, b):
    return lax.dot_general(a, b, (((0,), (0,)), ((), ())), preferred_element_type=F32)


def _sel_dot_l(sel_bf, x):
    hi, mid, lo = _split3(x)
    return (_dot(sel_bf, lo) + _dot(sel_bf, mid)) + _dot(sel_bf, hi)


def _sel_dot_r(x, sel_bf):
    hi, mid, lo = _split3(x)
    return (_dot(lo, sel_bf) + _dot(mid, sel_bf)) + _dot(hi, sel_bf)


def _mod_kernel(c_ref, w_ref, b_ref, o_ref):
    c = c_ref[...]
    o_ref[...] = jnp.dot(_silu(c), w_ref[...], preferred_element_type=F32,
                         precision=lax.Precision.HIGHEST) + b_ref[...]


def _modulation(cvecs, w_ada, b_ada):
    nrow = cvecs.shape[0]
    return pl.pallas_call(
        _mod_kernel,
        grid=(DEPTH, 6),
        in_specs=[
            pl.BlockSpec((nrow, D_MODEL), lambda l, j: (0, 0)),
            pl.BlockSpec((None, D_MODEL, D_MODEL), lambda l, j: (l, 0, j)),
            pl.BlockSpec((None, 1, D_MODEL), lambda l, j: (l, 0, j)),
        ],
        out_specs=pl.BlockSpec((None, nrow, D_MODEL), lambda l, j: (l, 0, j)),
        out_shape=jax.ShapeDtypeStruct((DEPTH, nrow, 6 * D_MODEL), F32),
        compiler_params=_cparams(("arbitrary", "arbitrary")),
        name="adaln_mod",
    )(cvecs, w_ada, b_ada.reshape(DEPTH, 1, 6 * D_MODEL))


def _inproj_kernel(x_ref, g_ref, shift_ref, scale_ref, w_ref, om_ref, os_ref, h_ref, *, n_main):
    j = pl.program_id(1)

    @pl.when(j == 0)
    def _():
        x = x_ref[...]
        h = x * lax.rsqrt(jnp.mean(x * x, axis=-1, keepdims=True) + EPS) * g_ref[...]
        h = h * (1.0 + scale_ref[...]) + shift_ref[...]
        h_ref[...] = h.astype(BF16)

    acc = _dot(h_ref[...], w_ref[...])

    @pl.when(j < n_main)
    def _():
        om_ref[...] = acc.astype(BF16)

    @pl.when(j == n_main)
    def _():
        os_ref[...] = acc[:, :S_COLS]


def _relayout_w_in(w):
    parts = [w[:, s:s + n] for s, n in _MAIN_SEGMENTS + _SMALL_SEGMENTS]
    parts.append(jnp.zeros((w.shape[0], PROJ_TN - 2 * SSD_HEADS - 2 * GLA_LOWRANK), w.dtype))
    return jnp.concatenate(parts, axis=1).astype(BF16)


def _in_projection(x2, mod_l, norm_g, w_perm, rows_per_mod, mod_row0):
    T = x2.shape[0]
    tm = min(PROJ_TM, rows_per_mod)
    n_main = M_COLS // PROJ_TN
    per = rows_per_mod // tm

    def mod_map(col):
        return lambda i, j: (mod_row0 + i // per, 0, col)

    return pl.pallas_call(
        functools.partial(_inproj_kernel, n_main=n_main),
        grid=(T // tm, n_main + 1),
        in_specs=[
            pl.BlockSpec((tm, D_MODEL), lambda i, j: (i, 0)),
            pl.BlockSpec((1, D_MODEL), lambda i, j: (0, 0)),
            pl.BlockSpec((None, 1, D_MODEL), mod_map(0)),
            pl.BlockSpec((None, 1, D_MODEL), mod_map(1)),
            pl.BlockSpec((D_MODEL, PROJ_TN), lambda i, j: (0, j)),
        ],
        out_specs=[
            pl.BlockSpec((tm, PROJ_TN), lambda i, j: (i, jnp.minimum(j, n_main - 1))),
            pl.BlockSpec((tm, S_COLS), lambda i, j: (i, 0)),
        ],
        out_shape=[jax.ShapeDtypeStruct((T, M_COLS), BF16), jax.ShapeDtypeStruct((T, S_COLS), F32)],
        scratch_shapes=[pltpu.VMEM((tm, D_MODEL), BF16)],
        compiler_params=_cparams(("arbitrary", "arbitrary")),
        name="in_projection",
    )(x2, norm_g.reshape(1, D_MODEL), mod_l, mod_l, w_perm)


_CONV_WIN = 128
_CONV_LEAD = 64
_CONV_OFF = 48


def _ssd_kernel(*refs, L, use_s0, emit_state):
    it = iter(refs)
    x_ref, b_ref, c_ref, ps_ref = next(it), next(it), next(it), next(it)
    cwx_ref, cwb_ref, cwc_ref = next(it), next(it), next(it)
    cbx_ref, cbb_ref, cbc_ref = next(it), next(it), next(it)
    dtb_ref, alog_ref, dsk_ref = next(it), next(it), next(it)
    s0_ref = next(it) if use_s0 else None
    y_ref = next(it)
    so_ref = next(it) if emit_state else None
    xpad, bpad, cpad, xc, bc, cc, dts, st, dtx, cumx = (next(it) for _ in range(10))

    Q = SSD_Q
    nc = L // Q
    g = pl.program_id(1)
    pad = SSD_CONV_W // 2

    for src, dst in ((x_ref, xpad), (b_ref, bpad), (c_ref, cpad)):
        w = dst.shape[1]
        dst[0:_CONV_LEAD, :] = jnp.zeros((_CONV_LEAD, w), BF16)
        dst[_CONV_LEAD + L:_CONV_LEAD + L + _CONV_LEAD, :] = jnp.zeros((_CONV_LEAD, w), BF16)

    def copy_body(c, carry):
        r0 = pl.multiple_of(c * Q, Q)
        for src, dst in ((x_ref, xpad), (b_ref, bpad), (c_ref, cpad)):
            dst[pl.ds(r0 + _CONV_LEAD, Q), :] = src[pl.ds(r0, Q), :]
        return carry

    lax.fori_loop(0, nc, copy_body, 0)

    ri = lax.broadcasted_iota(jnp.int32, (SSD_CONV_W * Q, _CONV_WIN), 0)
    ci = lax.broadcasted_iota(jnp.int32, (SSD_CONV_W * Q, _CONV_WIN), 1)
    shift_sel = (ci == (ri % Q) + (ri // Q) + (_CONV_LEAD - _CONV_OFF - pad)).astype(BF16)

    def conv_chunk(pad_ref, w_ref, bias_ref, r0):
        win = pad_ref[pl.ds(r0 + _CONV_OFF, _CONV_WIN), :]
        sh = _dot(shift_sel, win)
        acc = bias_ref[...] + w_ref[0:1, :] * sh[0:Q]
        for k in range(1, SSD_CONV_W):
            acc = acc + w_ref[k:k + 1, :] * sh[k * Q:(k + 1) * Q]
        return _silu(acc)

    def prep_body(c, carry):
        r0 = pl.multiple_of(c * Q, Q)
        xv = conv_chunk(xpad, cwx_ref, cbx_ref, r0)
        xc[pl.ds(r0, Q), :] = xv
        bc[pl.ds(r0, Q), :] = conv_chunk(bpad, cwb_ref, cbb_ref, r0)
        cc[pl.ds(r0, Q), :] = conv_chunk(cpad, cwc_ref, cbc_ref, r0)
        dts[pl.ds(r0, Q), :] = _softplus(ps_ref[pl.ds(r0, Q), :] + dtb_ref[...])
        y_ref[pl.ds(r0, Q), :] = dsk_ref[...] * xv
        return carry

    lax.fori_loop(0, nc, prep_body, 0)

    GW = SSD_GW
    hpg = SSD_HEADS // SSD_GROUPS
    er = lax.broadcasted_iota(jnp.int32, (S_COLS, GW), 0)
    ec = lax.broadcasted_iota(jnp.int32, (S_COLS, GW), 1)
    qi = lax.broadcasted_iota(jnp.int32, (Q, GW), 0)
    qj = lax.broadcasted_iota(jnp.int32, (Q, GW), 1) % Q
    diag_sel = (qi == qj).astype(F32)
    ones_q = jnp.ones((Q, Q), BF16)
    ti = lax.broadcasted_iota(jnp.int32, (Q, Q), 0)
    tj = lax.broadcasted_iota(jnp.int32, (Q, Q), 1)
    lane = lax.broadcasted_iota(jnp.int32, (Q, 128), 1)
    lo_half = lane < SSD_HEAD_DIM
    a_all = -jnp.exp(alog_ref[...])

    per_dir = []
    for d in range(2):
        base = S_DT + d * SSD_HEADS + g * hpg
        esel = (er == base + ec // SSD_HEAD_DIM).astype(BF16)
        a_exp = _sel_dot_r(jnp.broadcast_to(a_all, (8, S_COLS)), esel)[0:1]
        if d == 0:
            tri = (tj <= ti).astype(BF16)
            mask = qj <= qi
        else:
            tri = (tj >= ti).astype(BF16)
            mask = qj >= qi
        per_dir.append((esel, a_exp, tri, mask))

    for d in range(2):
        if use_s0:
            st[d] = s0_ref[d].T
        else:
            st[d] = jnp.zeros((SSD_STATE, GW), F32)

    SR = SSD_DECAY_ROWS
    bi = lax.broadcasted_iota(jnp.int32, (SR, SR), 0)
    bj = lax.broadcasted_iota(jnp.int32, (SR, SR), 1)
    same_chunk = (bi // Q) == (bj // Q)
    big_tri = (jnp.logical_and(same_chunk, bj <= bi).astype(BF16), jnp.logical_and(same_chunk, bj >= bi).astype(BF16))

    def decay_body(c, carry):
        r0 = pl.multiple_of(c * SR, SR)
        dt_c = dts[pl.ds(r0, SR), :]
        for d in range(2):
            esel, a_exp, _, _ = per_dir[d]
            dt_exp = _sel_dot_r(dt_c, esel)
            dtx[d, pl.ds(r0, SR), :] = dt_exp
            cumx[d, pl.ds(r0, SR), :] = _sel_dot_l(big_tri[d], dt_exp * a_exp)
        return carry

    lax.fori_loop(0, L // SR, decay_body, 0)

    def chunk_dir(c, d):
        _, _, _, mask = per_dir[d]
        r0 = pl.multiple_of(c * Q, Q)
        dt_exp = dtx[d, pl.ds(r0, Q), :]
        cum = cumx[d, pl.ds(r0, Q), :]
        rowb = _sel_dot_l(ones_q, cum * diag_sel)
        lmat = jnp.exp(jnp.where(mask, cum - rowb, -jnp.inf))
        bq = bc[pl.ds(r0, Q), :].astype(BF16)
        cq = cc[pl.ds(r0, Q), :].astype(BF16)
        cb = _dot_nt(cq, jnp.concatenate([bq] * (GW // Q), axis=0))
        amat = (cb * lmat).astype(BF16)
        xq = xc[pl.ds(r0, Q), :] * dt_exp
        parts = []
        for p in range(GW // 128):
            xp = xq[:, p * 128:(p + 1) * 128]
            xbd = jnp.concatenate([jnp.where(lo_half, xp, 0.0), jnp.where(lo_half, 0.0, xp)], axis=0)
            parts.append(_dot(amat[:, p * 128:(p + 1) * 128], xbd.astype(BF16)))
        y_intra = jnp.concatenate(parts, axis=1)
        s_t = st[d]
        y_inter = _dot(cq, s_t.astype(BF16)) * jnp.exp(cum)
        cum_last = cum[Q - 1:Q] if d == 0 else cum[0:1]
        xdec = (xq * jnp.exp(cum_last - cum)).astype(BF16)
        st[d] = s_t * jnp.exp(cum_last) + _dot_tn(bq, xdec)
        y_ref[pl.ds(r0, Q), :] += y_intra + y_inter

    def scan_body(i, carry):
        chunk_dir(i, 0)
        chunk_dir(nc - 1 - i, 1)
        return carry

    lax.fori_loop(0, nc, scan_body, 0, unroll=SSD_UNROLL)

    if emit_state:
        for d in range(2):
            so_ref[d] = st[d].T


def _ssd_branch(pm, ps, conv_w, conv_b, dt_bias, a_log, d_skip, s0, B, L, emit_state):
    use_s0 = s0 is not None
    G, GW = SSD_GROUPS, SSD_GW
    hpg = SSD_HEADS // G
    dtb = jnp.zeros((1, S_COLS), F32).at[0, S_DT:S_DT + 2 * SSD_HEADS].set(dt_bias.reshape(-1))
    alog = jnp.zeros((1, S_COLS), F32).at[0, S_DT:S_DT + 2 * SSD_HEADS].set(a_log.reshape(-1))
    dsk = jnp.repeat(d_skip, SSD_HEAD_DIM).reshape(1, SSD_INNER)
    cb2 = conv_b.reshape(1, SSD_CONV_DIM)
    nb_x = M_X // GW
    in_specs = [
        pl.BlockSpec((L, GW), lambda b, g: (b, nb_x + g)),
        pl.BlockSpec((L, SSD_STATE), lambda b, g: (b, M_B // SSD_STATE + g)),
        pl.BlockSpec((L, SSD_STATE), lambda b, g: (b, M_C // SSD_STATE + g)),
        pl.BlockSpec((L, S_COLS), lambda b, g: (b, 0)),
        pl.BlockSpec((SSD_CONV_W, GW), lambda b, g: (0, g)),
        pl.BlockSpec((SSD_CONV_W, SSD_STATE), lambda b, g: (0, SSD_INNER // SSD_STATE + g)),
        pl.BlockSpec((SSD_CONV_W, SSD_STATE), lambda b, g: (0, (SSD_INNER + SSD_GN) // SSD_STATE + g)),
        pl.BlockSpec((1, GW), lambda b, g: (0, g)),
        pl.BlockSpec((1, SSD_STATE), lambda b, g: (0, SSD_INNER // SSD_STATE + g)),
        pl.BlockSpec((1, SSD_STATE), lambda b, g: (0, (SSD_INNER + SSD_GN) // SSD_STATE + g)),
        pl.BlockSpec((1, S_COLS), lambda b, g: (0, 0)),
        pl.BlockSpec((1, S_COLS), lambda b, g: (0, 0)),
        pl.BlockSpec((1, GW), lambda b, g: (0, g)),
    ]
    args = [pm, pm, pm, ps, conv_w, conv_w, conv_w, cb2, cb2, cb2, dtb, alog, dsk]
    state_spec = pl.BlockSpec((None, 2, None, GW, SSD_STATE), lambda b, g: (b, 0, g, 0, 0))
    if use_s0:
        in_specs.append(state_spec)
        args.append(s0.reshape(B, 2, G, GW, SSD_STATE))
    out_specs = [pl.BlockSpec((L, GW), lambda b, g: (b, g))]
    out_shape = [jax.ShapeDtypeStruct((B * L, SSD_INNER), F32)]
    if emit_state:
        out_specs.append(state_spec)
        out_shape.append(jax.ShapeDtypeStruct((B, 2, G, GW, SSD_STATE), F32))
    plen = L + 2 * _CONV_LEAD
    outs = pl.pallas_call(
        functools.partial(_ssd_kernel, L=L, use_s0=use_s0, emit_state=emit_state),
        grid=(B, G),
        in_specs=in_specs,
        out_specs=out_specs,
        out_shape=out_shape,
        scratch_shapes=[
            pltpu.VMEM((plen, GW), BF16), pltpu.VMEM((plen, SSD_STATE), BF16), pltpu.VMEM((plen, SSD_STATE), BF16),
            pltpu.VMEM((L, GW), F32), pltpu.VMEM((L, SSD_STATE), F32), pltpu.VMEM((L, SSD_STATE), F32),
            pltpu.VMEM((L, S_COLS), F32), pltpu.VMEM((2, SSD_STATE, GW), F32),
            pltpu.VMEM((2, L, GW), F32), pltpu.VMEM((2, L, GW), F32),
        ],
        compiler_params=_cparams(("arbitrary", "arbitrary")),
        name="ssd_scan",
    )(*args)
    y = outs[0]
    state = outs[1].reshape(B, 2, SSD_HEADS, SSD_HEAD_DIM, SSD_STATE) if emit_state else None
    return y, state


def _ctx_attn_kernel(q_ref, k_ref, v_ref, o_ref):
    q = q_ref[...]
    L = q.shape[0]
    lane = lax.broadcasted_iota(jnp.int32, q.shape, 1)
    lo = lane < NA_HEAD_DIM
    zero = jnp.zeros_like(q)
    q2 = jnp.concatenate([jnp.where(lo, q, zero), jnp.where(lo, zero, q)], axis=0)
    s = _dot_nt(q2, k_ref[...]) * NA_SCALE
    m = jnp.max(s, axis=-1, keepdims=True)
    p = jnp.exp(s - m)
    den = jnp.sum(p, axis=-1, keepdims=True)
    o2 = _dot(p.astype(BF16), v_ref[...]) / den
    o_ref[...] = jnp.where(lo, o2[:L], o2[L:]).astype(o_ref.dtype)


def _context_attention(pm, B, L):
    nq, nk, nv = M_Q // 128, M_K // 128, M_V // 128
    return pl.pallas_call(
        _ctx_attn_kernel,
        grid=(B, NA_HEADS // 2),
        in_specs=[
            pl.BlockSpec((L, 128), lambda b, h: (b, nq + h)),
            pl.BlockSpec((L, 128), lambda b, h: (b, nk + h)),
            pl.BlockSpec((L, 128), lambda b, h: (b, nv + h)),
        ],
        out_specs=pl.BlockSpec((L, 128), lambda b, h: (b, h)),
        out_shape=jax.ShapeDtypeStruct((B * L, NA_INNER), BF16),
        compiler_params=_cparams(("arbitrary", "arbitrary")),
        name="context_attention",
    )(pm, pm, pm)


def _na_bias_table(rpb):
    qc = np.arange(GRID_W)
    cstart = np.clip(qc - NA_WIN_COLS // 2, 0, GRID_W - NA_WIN_COLS)
    kc = np.arange(GRID_W)
    valid = (kc[None, :] >= cstart[:, None]) & (kc[None, :] < cstart[:, None] + NA_WIN_COLS)
    dx = np.clip(kc[None, :] - qc[:, None] + NA_WIN_COLS - 1, 0, 2 * NA_WIN_COLS - 2)
    H, n_dr, n_dx = rpb.shape
    pick = (dx[None, :, :] == np.arange(n_dx)[:, None, None]).astype(np.float32).reshape(n_dx, GRID_W * GRID_W)
    cols = jnp.dot(rpb.reshape(H * n_dr, n_dx), jnp.asarray(pick), precision=lax.Precision.HIGHEST)
    cols = cols.reshape(H, n_dr, GRID_W, GRID_W)
    t = jnp.stack([cols[:, d0:d0 + NA_WIN_ROWS] for d0 in range(NA_WIN_ROWS)], axis=1)
    t = t.transpose(0, 1, 3, 2, 4)
    t = jnp.where(jnp.asarray(valid)[None, None, :, None, :], t, NA_NEG)
    t = t.reshape(H // 2, 2, NA_WIN_ROWS, GRID_W, NA_WIN_ROWS * GRID_W).astype(F32)
    return t.transpose(0, 2, 1, 3, 4).reshape(H // 2, NA_WIN_ROWS, 2 * GRID_W, NA_WIN_ROWS * GRID_W)


def _na_kernel(q_ref, k_ref, v_ref, ck_ref, cv_ref, bias_ref, o_ref, ckb, cvb, *, rows):
    ckb[...] = ck_ref[...].astype(BF16)
    cvb[...] = cv_ref[...].astype(BF16)
    wr = NA_WIN_ROWS
    nloc = wr * GRID_W
    lo = lax.broadcasted_iota(jnp.int32, (GRID_W, 128), 1) < NA_HEAD_DIM

    def row_body(r, carry):
        rs = jnp.clip(r - wr // 2, 0, rows - wr)
        d0 = rs - r + wr - 1
        q = q_ref[pl.ds(pl.multiple_of(r * GRID_W, GRID_W), GRID_W), :]
        k0 = pl.multiple_of(rs * GRID_W, GRID_W)
        kw = k_ref[pl.ds(k0, nloc), :]
        vw = v_ref[pl.ds(k0, nloc), :]
        zero = jnp.zeros_like(q)
        q2 = jnp.concatenate([jnp.where(lo, q, zero), jnp.where(lo, zero, q)], axis=0)
        s_loc = _dot_nt(q2, kw) * NA_SCALE + bias_ref[d0]
        s_ctx = _dot_nt(q2, ckb[...]) * NA_SCALE
        m = jnp.maximum(jnp.max(s_loc, axis=-1, keepdims=True), jnp.max(s_ctx, axis=-1, keepdims=True))
        p_loc = jnp.exp(s_loc - m)
        p_ctx = jnp.exp(s_ctx - m)
        den = jnp.sum(p_loc, axis=-1, keepdims=True) + jnp.sum(p_ctx, axis=-1, keepdims=True)
        o2 = (_dot(p_loc.astype(BF16), vw) + _dot(p_ctx.astype(BF16), cvb[...])) / den
        o = jnp.where(lo, o2[:GRID_W], o2[GRID_W:])
        o_ref[pl.ds(pl.multiple_of(r * GRID_W, GRID_W), GRID_W), :] = o.astype(o_ref.dtype)
        return carry

    lax.fori_loop(0, rows, row_body, 0, unroll=NA_UNROLL)


def _neighbourhood_attention(pm, ck, cv, bias_tab, B, S):
    rows = S // GRID_W
    assert rows >= NA_WIN_ROWS
    Lc = ck.shape[1]
    nq, nk, nv = M_Q // 128, M_K // 128, M_V // 128
    return pl.pallas_call(
        functools.partial(_na_kernel, rows=rows),
        grid=(NA_HEADS // 2, B),
        in_specs=[
            pl.BlockSpec((S, 128), lambda h, b: (b, nq + h)),
            pl.BlockSpec((S, 128), lambda h, b: (b, nk + h)),
            pl.BlockSpec((S, 128), lambda h, b: (b, nv + h)),
            pl.BlockSpec((None, Lc, 128), lambda h, b: (b, 0, h)),
            pl.BlockSpec((None, Lc, 128), lambda h, b: (b, 0, h)),
            pl.BlockSpec((None, NA_WIN_ROWS, 2 * GRID_W, NA_WIN_ROWS * GRID_W), lambda h, b: (h, 0, 0, 0)),
        ],
        out_specs=pl.BlockSpec((S, 128), lambda h, b: (b, h)),
        out_shape=jax.ShapeDtypeStruct((B * S, NA_INNER), BF16),
        scratch_shapes=[pltpu.VMEM((Lc, 128), BF16), pltpu.VMEM((Lc, 128), BF16)],
        compiler_params=_cparams(("arbitrary", "arbitrary")),
        name="neighbourhood_attention",
    )(pm, pm, pm, ck, cv, bias_tab)


def _gla_kernel(*refs, L, use_s0, emit_state):
    it = iter(refs)
    q_ref, k_ref, v_ref, r_ref, ps_ref, wg_ref, bg_ref, ng_ref = (next(it) for _ in range(8))
    s0_ref = next(it) if use_s0 else None
    o_ref = next(it)
    so_ref = next(it) if emit_state else None
    acc, st = next(it), next(it)

    Q, H, DK, DV = GLA_Q, GLA_HEADS, GLA_DK_HEAD, GLA_DV_HEAD
    HQ = H * Q
    nc = L // Q

    def zero_body(c, carry):
        acc[c] = jnp.zeros((HQ, DV), F32)
        return carry

    lax.fori_loop(0, nc, zero_body, 0)

    for d in range(2):
        for h in range(H):
            if use_s0:
                st[d, :, h * DK:(h + 1) * DK] = s0_ref[d, h].T
            else:
                st[d, :, h * DK:(h + 1) * DK] = jnp.zeros((DV, DK), F32)

    SC = GLA_SUPER
    G = SC * Q
    R = SC * HQ
    nsc = L // G
    gi = lax.broadcasted_iota(jnp.int32, (G, G), 0)
    gj = lax.broadcasted_iota(jnp.int32, (G, G), 1)
    same_chunk = (gi // Q) == (gj // Q)
    tris = (jnp.logical_and(same_chunk, gj <= gi).astype(BF16), jnp.logical_and(same_chunk, gj >= gi).astype(BF16))
    ends = ((gj == (gi // Q) * Q + (Q - 1)).astype(BF16), (gj == (gi // Q) * Q).astype(BF16))
    row_k = lax.broadcasted_iota(jnp.int32, (R, H * DK), 0)
    lane_k = lax.broadcasted_iota(jnp.int32, (R, H * DK), 1)
    own_k = ((row_k // Q) % H) == (lane_k // DK)
    row_s = lax.broadcasted_iota(jnp.int32, (R, G), 0)
    col_s = lax.broadcasted_iota(jnp.int32, (R, G), 1)
    in_chunk = (row_s // HQ) == (col_s // Q)
    causal = (jnp.logical_and(in_chunk, (col_s % Q) <= (row_s % Q)),
              jnp.logical_and(in_chunk, (col_s % Q) >= (row_s % Q)))
    rep_r = lax.broadcasted_iota(jnp.int32, (G, R), 0)
    rep_c = lax.broadcasted_iota(jnp.int32, (G, R), 1)
    rep = jnp.logical_and(rep_r // Q == rep_c // HQ, rep_r % Q == rep_c % Q).astype(BF16)
    row_a = lax.broadcasted_iota(jnp.int32, (R, R), 0)
    lane_a = lax.broadcasted_iota(jnp.int32, (R, R), 1)
    own_a = (row_a // Q) == (lane_a // Q)

    def stack_heads(x):
        return jnp.concatenate([x[c * Q:(c + 1) * Q] for c in range(SC) for _ in range(H)], axis=0)

    def super_dir(u, d):
        r0 = pl.multiple_of(u * G, G)
        lr = ps_ref[pl.ds(r0, G), :].astype(BF16)
        gpre = (_dot(lr, wg_ref[:, d * GLA_DK:(d + 1) * GLA_DK].astype(BF16))
                + bg_ref[:, d * GLA_DK:(d + 1) * GLA_DK])
        g = (jnp.minimum(gpre, 0.0) - jnp.log1p(jnp.exp(-jnp.abs(gpre)))) / GLA_TAU
        bcum = _sel_dot_l(tris[d], g)
        bend = _sel_dot_l(ends[d], bcum)
        qc = q_ref[pl.ds(r0, G), :].astype(F32) * GLA_QK_SCALE
        kc = k_ref[pl.ds(r0, G), :].astype(F32)
        vc = v_ref[pl.ds(r0, G), :]
        qt = (qc * jnp.exp(bcum)).astype(BF16)
        kt = (kc * jnp.exp(-bcum)).astype(BF16)
        kend = (kc * jnp.exp(bend - bcum)).astype(BF16)
        zk = jnp.zeros((R, H * DK), BF16)
        q16 = jnp.where(own_k, stack_heads(qt), zk)
        k16 = jnp.where(own_k, stack_heads(kend), zk)
        v16 = jnp.concatenate([vc[c * Q:(c + 1) * Q, h * DV:(h + 1) * DV] for c in range(SC) for h in range(H)],
                              axis=0)
        att = jnp.where(causal[d], _dot_nt(q16, kt), 0.0)
        a16 = jnp.where(own_a, _dot(att.astype(BF16), rep), 0.0).astype(BF16)
        o_intra = _dot(a16, v16)
        decay = jnp.exp(bend)
        order = range(SC) if d == 0 else range(SC - 1, -1, -1)
        for c in order:
            q4 = q16[c * HQ:(c + 1) * HQ]
            s_t = st[d]
            o4 = o_intra[c * HQ:(c + 1) * HQ] + _dot_nt(q4, s_t.astype(BF16))
            st[d] = s_t * decay[c * Q:c * Q + 1] + _dot_tn(v16[c * HQ:(c + 1) * HQ], k16[c * HQ:(c + 1) * HQ])
            acc[u * SC + c] += o4

    def scan_body(i, carry):
        super_dir(i, 0)
        super_dir(nsc - 1 - i, 1)
        return carry

    lax.fori_loop(0, nsc, scan_body, 0)

    def fin_body(c, carry):
        r0 = pl.multiple_of(c * Q, Q)
        o = acc[c]
        o = o * lax.rsqrt(jnp.mean(o * o, axis=-1, keepdims=True) + EPS) * ng_ref[...]
        for h in range(H):
            gate = _silu(r_ref[pl.ds(r0, Q), h * DV:(h + 1) * DV].astype(F32))
            o_ref[pl.ds(r0, Q), h * DV:(h + 1) * DV] = (o[h * Q:(h + 1) * Q] * gate).astype(o_ref.dtype)
        return carry

    lax.fori_loop(0, nc, fin_body, 0)

    if emit_state:
        for d in range(2):
            for h in range(H):
                so_ref[d, h] = st[d, :, h * DK:(h + 1) * DK].T


def _gla_branch(pm, ps, w_gate, b_gate, norm_g, s0, B, L, emit_state):
    use_s0 = s0 is not None
    H, DK, DV = GLA_HEADS, GLA_DK_HEAD, GLA_DV_HEAD
    wg = jnp.zeros((S_COLS, 2 * GLA_DK), F32)
    for d in range(2):
        wg = wg.at[S_LR + d * GLA_LOWRANK:S_LR + (d + 1) * GLA_LOWRANK, d * GLA_DK:(d + 1) * GLA_DK].set(w_gate[d])
    bg = b_gate.reshape(1, 2 * GLA_DK)
    in_specs = [
        pl.BlockSpec((L, GLA_DK), lambda b: (b, M_GQ // GLA_DK)),
        pl.BlockSpec((L, GLA_DK), lambda b: (b, M_GK // GLA_DK)),
        pl.BlockSpec((L, GLA_DV), lambda b: (b, M_GV // GLA_DV)),
        pl.BlockSpec((L, GLA_DV), lambda b: (b, M_R // GLA_DV)),
        pl.BlockSpec((L, S_COLS), lambda b: (b, 0)),
        pl.BlockSpec((S_COLS, 2 * GLA_DK), lambda b: (0, 0)),
        pl.BlockSpec((1, 2 * GLA_DK), lambda b: (0, 0)),
        pl.BlockSpec((1, DV), lambda b: (0, 0)),
    ]
    args = [pm, pm, pm, pm, ps, wg, bg, norm_g.reshape(1, DV)]
    state_spec = pl.BlockSpec((None, 2, H, DK, DV), lambda b: (b, 0, 0, 0, 0))
    if use_s0:
        in_specs.append(state_spec)
        args.append(s0)
    out_specs = [pl.BlockSpec((L, GLA_DV), lambda b: (b, 0))]
    out_shape = [jax.ShapeDtypeStruct((B * L, GLA_DV), BF16)]
    if emit_state:
        out_specs.append(state_spec)
        out_shape.append(jax.ShapeDtypeStruct((B, 2, H, DK, DV), F32))
    outs = pl.pallas_call(
        functools.partial(_gla_kernel, L=L, use_s0=use_s0, emit_state=emit_state),
        grid=(B,),
        in_specs=in_specs,
        out_specs=out_specs,
        out_shape=out_shape,
        scratch_shapes=[pltpu.VMEM((L // GLA_Q, H * GLA_Q, DV), F32), pltpu.VMEM((2, DV, H * DK), F32)],
        compiler_params=_cparams(("arbitrary",)),
        name="gla_scan",
    )(*args)
    return outs[0], (outs[1] if emit_state else None)


def _merge_kernel(*refs, aliased):
    (x_ref, yssd_ref, z_ref, ona_ref, ogla_ref, g0_ref, g1_ref, g2_ref,
     gate1_ref, shift2_ref, scale2_ref, sng_ref, n2g_ref, wb_ref, wo_ref, rw_ref, rb_ref) = refs[:17]
    xo_ref, h2_ref, lg_ref = refs[-3:]
    y = yssd_ref[...] * _silu(z_ref[...].astype(F32))
    y = y * lax.rsqrt(jnp.mean(y * y, axis=-1, keepdims=True) + EPS) * sng_ref[...]
    m = jax.nn.sigmoid(g0_ref[...].astype(F32)) * _dot(y.astype(BF16), wb_ref[0])
    m = m + jax.nn.sigmoid(g1_ref[...].astype(F32)) * _dot(ona_ref[...], wb_ref[1])
    m = m + jax.nn.sigmoid(g2_ref[...].astype(F32)) * _dot(ogla_ref[...], wb_ref[2])
    x = x_ref[...] + gate1_ref[...] * _dot(m.astype(BF16), wo_ref[...])
    xo_ref[...] = x
    h2 = x * lax.rsqrt(jnp.mean(x * x, axis=-1, keepdims=True) + EPS) * n2g_ref[...]
    h2 = h2 * (1.0 + scale2_ref[...]) + shift2_ref[...]
    h2_ref[...] = h2
    lg_ref[...] = jnp.dot(h2, rw_ref[...], preferred_element_type=F32,
                          precision=lax.Precision.HIGHEST) + rb_ref[...]


def _merge(x2, y_ssd, pm, o_na, o_gla, mod_l, ssd_norm_g, norm2_g, wb_bf, wo_bf, rw_pad, rb_pad,
           rows_per_mod, mod_row0, t_all, row0, shared):
    T = x2.shape[0]
    tm = min(256, rows_per_mod)
    per = rows_per_mod // tm
    D = D_MODEL
    ng = M_GATES // D
    blk0 = row0 // tm

    def mod_map(col):
        return lambda i: (mod_row0 + i // per, 0, col)

    row = lambda i: (i, 0)
    row_off = lambda i: (blk0 + i, 0)
    const2 = lambda i: (0, 0)
    in_specs = [
        pl.BlockSpec((tm, D), row),
        pl.BlockSpec((tm, D), row),
        pl.BlockSpec((tm, D), lambda i: (i, M_Z // D)),
        pl.BlockSpec((tm, D), row),
        pl.BlockSpec((tm, D), row),
        pl.BlockSpec((tm, D), lambda i: (i, ng)),
        pl.BlockSpec((tm, D), lambda i: (i, ng + 1)),
        pl.BlockSpec((tm, D), lambda i: (i, ng + 2)),
        pl.BlockSpec((None, 1, D), mod_map(2)),
        pl.BlockSpec((None, 1, D), mod_map(3)),
        pl.BlockSpec((None, 1, D), mod_map(4)),
        pl.BlockSpec((1, D), const2),
        pl.BlockSpec((1, D), const2),
        pl.BlockSpec((N_BRANCH, D, D), lambda i: (0, 0, 0)),
        pl.BlockSpec((D, D), const2),
        pl.BlockSpec((D, 128), const2),
        pl.BlockSpec((1, 128), const2),
    ]
    args = [x2, y_ssd, pm, o_na, o_gla, pm, pm, pm, mod_l, mod_l, mod_l,
            ssd_norm_g.reshape(1, D), norm2_g.reshape(1, D), wb_bf, wo_bf, rw_pad, rb_pad]
    aliases = {}
    if shared is not None:
        in_specs += [pl.BlockSpec(memory_space=pl.ANY), pl.BlockSpec(memory_space=pl.ANY)]
        aliases = {len(args): 1, len(args) + 1: 2}
        args += list(shared)
    return pl.pallas_call(
        functools.partial(_merge_kernel, aliased=shared is not None),
        grid=(T // tm,),
        in_specs=in_specs,
        out_specs=[pl.BlockSpec((tm, D), row), pl.BlockSpec((tm, D), row_off), pl.BlockSpec((tm, 128), row_off)],
        out_shape=[jax.ShapeDtypeStruct((T, D), F32), jax.ShapeDtypeStruct((t_all, D), F32),
                   jax.ShapeDtypeStruct((t_all, 128), F32)],
        input_output_aliases=aliases,
        compiler_params=_cparams(("arbitrary",)),
        name="branch_merge",
    )(*args)


def _moe_kernel(be_ref, nused_ref, dst_ref, dstn_ref, h2_hbm,
                wgu_ref, bgu_ref, wdn_ref, bdn_ref, y_hbm, xbuf, obuf, wgu_bf, wdn_bf, gsem, ssem):
    b = pl.program_id(0)
    nused = nused_ref[0]
    slot = b % 2
    n_tok = h2_hbm.shape[0]

    def token_of(row):
        if n_tok & (n_tok - 1) == 0:
            return jnp.bitwise_and(row, n_tok - 1)
        return lax.rem(row, n_tok)

    def start_gather(idx_ref, s):
        def body(i, carry):
            pltpu.make_async_copy(h2_hbm.at[pl.ds(token_of(idx_ref[0, i]), 1), :], xbuf.at[s, pl.ds(i, 1), :],
                                  gsem.at[s]).start()
            return carry
        lax.fori_loop(0, MOE_ROWS, body, 0, unroll=MOE_DMA_UNROLL)

    def wait_gather(s):
        pltpu.make_async_copy(h2_hbm.at[pl.ds(0, MOE_ROWS), :], xbuf.at[s], gsem.at[s]).wait()

    def wait_scatter():
        pltpu.make_async_copy(obuf, y_hbm.at[pl.ds(0, MOE_ROWS), :], ssem.at[0]).wait()

    @pl.when(b == 0)
    def _():
        start_gather(dst_ref, 0)

    @pl.when(b < nused)
    def _():
        wait_gather(slot)

        prev = be_ref[jnp.maximum(b - 1, 0)]

        @pl.when(jnp.logical_or(b == 0, be_ref[b] != prev))
        def _():
            rc = 128

            def cast_body(i, carry):
                r0 = pl.multiple_of(i * rc, rc)
                wgu_bf[pl.ds(r0, rc), :] = wgu_ref[pl.ds(r0, rc), :].astype(BF16)
                wdn_bf[pl.ds(r0, rc), :] = wdn_ref[pl.ds(r0, rc), :].astype(BF16)
                return carry

            lax.fori_loop(0, D_MODEL // rc, cast_body, 0)

        for i in range(MOE_ROWS):
            pltpu.make_async_copy(h2_hbm.at[pl.ds(token_of(dstn_ref[0, i]), 1), :],
                                  xbuf.at[1 - slot, pl.ds(i, 1), :], gsem.at[1 - slot]).start(priority=i % 2)

        x = xbuf[slot].astype(BF16)
        acc = jnp.zeros((MOE_ROWS, D_MODEL), F32) + bdn_ref[...]
        for f in range(D_FF // MOE_FF_TILE):
            c0 = f * MOE_FF_TILE
            glu = _dot(x, wgu_bf[:, c0:c0 + MOE_FF_TILE]) + bgu_ref[:, c0:c0 + MOE_FF_TILE]
            lin = _dot(x, wgu_bf[:, D_FF + c0:D_FF + c0 + MOE_FF_TILE]) + bgu_ref[:, D_FF + c0:D_FF + c0 + MOE_FF_TILE]
            glu = jnp.minimum(glu, SWIGLU_LIMIT)
            lin = jnp.clip(lin, -SWIGLU_LIMIT, SWIGLU_LIMIT)
            act = glu * jax.nn.sigmoid(SWIGLU_ALPHA * glu) * (lin + 1.0)
            acc = acc + _dot(act.astype(BF16), wdn_bf[c0:c0 + MOE_FF_TILE, :])

        @pl.when(b > 0)
        def _():
            wait_scatter()

        obuf[...] = acc

        def sc_body(j, carry):
            for p in range(2):
                i = 2 * j + p
                pltpu.make_async_copy(obuf.at[pl.ds(i, 1), :], y_hbm.at[pl.ds(dst_ref[0, i], 1), :],
                                      ssem.at[0]).start(priority=p)
            return carry

        lax.fori_loop(0, MOE_ROWS // 2, sc_body, 0, unroll=MOE_DMA_UNROLL // 2)

        @pl.when(b == nused - 1)
        def _():
            wait_scatter()
            wait_gather(1 - slot)


def _moe_experts(h2_all, slot_dst, block_e, n_used, w_gu, b_gu, w_dn, b_dn, layer):
    n_blocks = slot_dst.shape[0]
    T, D = h2_all.shape
    smem_blk = lambda f: pl.BlockSpec((None, 1, MOE_ROWS), f, memory_space=pltpu.SMEM)
    grid_spec = pltpu.PrefetchScalarGridSpec(
        num_scalar_prefetch=2,
        grid=(n_blocks,),
        in_specs=[
            smem_blk(lambda b, be, nu: (b, 0, 0)),
            smem_blk(lambda b, be, nu: (jnp.minimum(b + 1, n_blocks - 1), 0, 0)),
            pl.BlockSpec(memory_space=pl.ANY),
            pl.BlockSpec((None, None, D, 2 * D_FF), lambda b, be, nu: (layer, be[b], 0, 0)),
            pl.BlockSpec((None, None, 1, 2 * D_FF), lambda b, be, nu: (layer, be[b], 0, 0)),
            pl.BlockSpec((None, None, D_FF, D), lambda b, be, nu: (layer, be[b], 0, 0)),
            pl.BlockSpec((None, None, 1, D), lambda b, be, nu: (layer, be[b], 0, 0)),
        ],
        out_specs=pl.BlockSpec(memory_space=pl.ANY),
        scratch_shapes=[pltpu.VMEM((2, MOE_ROWS, D), F32), pltpu.VMEM((MOE_ROWS, D), F32),
                        pltpu.VMEM((D, 2 * D_FF), BF16), pltpu.VMEM((D_FF, D), BF16),
                        pltpu.SemaphoreType.DMA((2,)), pltpu.SemaphoreType.DMA((1,))],
    )
    return pl.pallas_call(
        _moe_kernel,
        grid_spec=grid_spec,
        out_shape=jax.ShapeDtypeStruct((TOP_K * T + MOE_ROWS, D), F32),
        compiler_params=_cparams(("arbitrary",)),
        name="moe_experts",
    )(block_e, n_used, slot_dst, slot_dst, h2_all, w_gu,
      b_gu.reshape(DEPTH, N_EXPERTS, 1, 2 * D_FF), w_dn, b_dn.reshape(DEPTH, N_EXPERTS, 1, D))


def _moe_route(logits):
    T = logits.shape[0]
    TK = T * TOP_K
    top_logit, top_idx = lax.top_k(logits, TOP_K)
    top_w = jax.nn.softmax(top_logit, axis=-1)
    flat_e = top_idx.reshape(TK).astype(jnp.int32)
    onehot = (flat_e[:, None] == jnp.arange(N_EXPERTS, dtype=jnp.int32)[None, :]).astype(jnp.int32)
    csum = jnp.cumsum(onehot, axis=0)
    rank = jnp.sum(onehot * csum, axis=1) - 1
    counts = csum[-1]
    padded = (counts + MOE_ROWS - 1) // MOE_ROWS * MOE_ROWS
    pad_end = jnp.cumsum(padded)
    pad_start = pad_end - padded
    dest = jnp.sum(onehot * pad_start[None, :], axis=1) + rank
    n_blocks = -(-(TK + N_EXPERTS * (MOE_ROWS - 1)) // MOE_ROWS)
    n_slots = n_blocks * MOE_ROWS
    flat = jnp.arange(TK, dtype=jnp.int32)
    spare = TK + jnp.arange(n_slots, dtype=jnp.int32) % MOE_ROWS
    slot_dst = spare.at[dest].set((flat % TOP_K) * T + flat // TOP_K)
    blk_start = jnp.arange(n_blocks, dtype=jnp.int32) * MOE_ROWS
    block_e = jnp.sum((blk_start[:, None] >= pad_end[None, :]).astype(jnp.int32), axis=1)
    block_e = jnp.minimum(block_e, N_EXPERTS - 1).astype(jnp.int32)
    n_used = (pad_end[-1] // MOE_ROWS).astype(jnp.int32).reshape(1)
    top_w_pad = jnp.pad(top_w, ((0, 0), (0, 128 - TOP_K)))
    return top_w_pad, slot_dst.reshape(n_blocks, 1, MOE_ROWS), block_e, n_used


def _combine_kernel(x_ref, y0_ref, y1_ref, y2_ref, y3_ref, w_ref, gate_ref, o_ref):
    w = w_ref[...]
    y = w[:, 0:1] * y0_ref[...]
    for k, y_ref in enumerate((y1_ref, y2_ref, y3_ref), start=1):
        y = y + w[:, k:k + 1] * y_ref[...]
    o_ref[...] = x_ref[...] + gate_ref[...] * y


def _moe_combine(x2, y_rows, top_w_pad, mod_l, rows_per_mod, mod_row0, row0, t_all):
    assert TOP_K == 4
    T, D = x2.shape
    tm = min(512, rows_per_mod)
    per = rows_per_mod // tm
    blk0 = row0 // tm
    nblk = t_all // tm

    def y_map(k):
        return lambda i: (k * nblk + blk0 + i, 0)

    return pl.pallas_call(
        _combine_kernel,
        grid=(T // tm,),
        in_specs=[pl.BlockSpec((tm, D), lambda i: (i, 0))]
        + [pl.BlockSpec((tm, D), y_map(k)) for k in range(TOP_K)]
        + [pl.BlockSpec((tm, 128), lambda i: (blk0 + i, 0)),
           pl.BlockSpec((None, 1, D), lambda i: (mod_row0 + i // per, 0, 5))],
        out_specs=pl.BlockSpec((tm, D), lambda i: (i, 0)),
        out_shape=jax.ShapeDtypeStruct((T, D), F32),
        compiler_params=_cparams(("arbitrary",)),
        name="moe_combine",
    )(x2, y_rows, y_rows, y_rows, y_rows, top_w_pad, mod_l)


def _final_norm_kernel(x_ref, g_ref, o_ref):
    x = x_ref[...]
    o_ref[...] = x * lax.rsqrt(jnp.mean(x * x, axis=-1, keepdims=True) + EPS) * g_ref[...]


def _final_norm(x2, g):
    T = x2.shape[0]
    tm = 512
    return pl.pallas_call(
        _final_norm_kernel,
        grid=(T // tm,),
        in_specs=[pl.BlockSpec((tm, D_MODEL), lambda i: (i, 0)), pl.BlockSpec((1, D_MODEL), lambda i: (0, 0))],
        out_specs=pl.BlockSpec((tm, D_MODEL), lambda i: (i, 0)),
        out_shape=jax.ShapeDtypeStruct((T, D_MODEL), F32),
        compiler_params=_cparams(("arbitrary",)),
        name="final_norm",
    )(x2, g.reshape(1, D_MODEL))


def _mixer_half(x2, B, L, mod_l, mod_row0, lw, ctx, t_all, row0, shared):
    latent = ctx is not None
    rows_per_mod = L if latent else B * L
    pm, ps = _in_projection(x2, mod_l, lw["norm1_g"], lw["w_in"], rows_per_mod, mod_row0)
    if latent:
        ck, cv, s_ssd0, s_gla0 = ctx
    else:
        s_ssd0 = s_gla0 = None
    y_ssd, s_ssd = _ssd_branch(pm, ps, lw["ssd_conv_w"], lw["ssd_conv_b"], lw["ssd_dt_bias"], lw["ssd_a_log"],
                               lw["ssd_d"], s_ssd0, B, L, emit_state=not latent)
    if latent:
        o_na = _neighbourhood_attention(pm, ck, cv, lw["na_bias"], B, L)
    else:
        o_na = _context_attention(pm, B, L)
    o_gla, s_gla = _gla_branch(pm, ps, lw["gla_w_gate"], lw["gla_b_gate"], lw["gla_norm_g"], s_gla0, B, L,
                               emit_state=not latent)
    x_new, h2, logits = _merge(x2, y_ssd, pm, o_na, o_gla, mod_l, lw["ssd_norm_g"], lw["norm2_g"],
                               lw["w_branch"], lw["w_out"], lw["router_w"], lw["router_b"], rows_per_mod, mod_row0,
                               t_all, row0, shared)
    return x_new, h2, logits, pm, s_ssd, s_gla


def kernel(x_prompt, x_sample, cache_na_k, cache_na_v, state_ssd, state_gla, c, c_ctx, w_ada, b_ada, norm1_g, norm2_g, w_in, ssd_conv_w, ssd_conv_b, ssd_dt_bias, ssd_a_log, ssd_d, ssd_norm_g, na_rpb, gla_w_gate, gla_b_gate, gla_norm_g, w_branch, w_out, router_w, router_b, moe_w_gu, moe_b_gu, moe_w_dn, moe_b_dn, final_norm_g):
    Bp, Lp, D = x_prompt.shape
    Bs, Ls, _ = x_sample.shape
    Tp, Ts = Bp * Lp, Bs * Ls
    Lc = cache_na_k.shape[2]

    cvecs = jnp.concatenate([c_ctx[None], c, jnp.zeros((8 - 1 - Bs, D), F32)], axis=0)
    mod = _modulation(cvecs, w_ada, b_ada).reshape(DEPTH, 8, 1, 6 * D)

    xp = x_prompt.reshape(Tp, D)
    xs = x_sample.reshape(Ts, D)
    ks_, vs_, sss_, sgs_ = [], [], [], []
    for l in range(DEPTH):
        lw = {
            "norm1_g": norm1_g[l], "norm2_g": norm2_g[l], "w_in": _relayout_w_in(w_in[l]),
            "ssd_conv_w": ssd_conv_w[l], "ssd_conv_b": ssd_conv_b[l], "ssd_dt_bias": ssd_dt_bias[l],
            "ssd_a_log": ssd_a_log[l], "ssd_d": ssd_d[l], "ssd_norm_g": ssd_norm_g[l],
            "na_bias": _na_bias_table(na_rpb[l]),
            "gla_w_gate": gla_w_gate[l], "gla_b_gate": gla_b_gate[l], "gla_norm_g": gla_norm_g[l],
            "w_branch": w_branch[l].astype(BF16), "w_out": w_out[l].astype(BF16),
            "router_w": jnp.pad(router_w[l], ((0, 0), (0, 128 - N_EXPERTS))),
            "router_b": jnp.pad(router_b[l], (0, 128 - N_EXPERTS)).reshape(1, 128),
        }
        mod_l = mod[l]
        t_all = Tp + Ts
        xp, h2a, lga, pmp, s_ssd, s_gla = _mixer_half(xp, Bp, Lp, mod_l, 0, lw, None, t_all, 0, None)
        ctx = (cache_na_k[:, l].reshape(Bs, Lc, NA_INNER), cache_na_v[:, l].reshape(Bs, Lc, NA_INNER),
               state_ssd[:, l], state_gla[:, l])
        xs, h2a, lga, _, _, _ = _mixer_half(xs, Bs, Ls, mod_l, 1, lw, ctx, t_all, Tp, (h2a, lga))
        ks_.append(pmp[:, M_K:M_K + NA_INNER].astype(F32).reshape(Bp, Lp, NA_HEADS, NA_HEAD_DIM))
        vs_.append(pmp[:, M_V:M_V + NA_INNER].astype(F32).reshape(Bp, Lp, NA_HEADS, NA_HEAD_DIM))
        sss_.append(s_ssd)
        sgs_.append(s_gla)

        top_w, slot_dst, block_e, n_used = _moe_route(lga[:, :N_EXPERTS])
        y_rows = _moe_experts(h2a, slot_dst, block_e, n_used, moe_w_gu, moe_b_gu, moe_w_dn, moe_b_dn, l)
        xp = _moe_combine(xp, y_rows, top_w, mod_l, Tp, 0, 0, t_all)
        xs = _moe_combine(xs, y_rows, top_w, mod_l, Ls, 1, Tp, t_all)

    y_prompt = _final_norm(xp, final_norm_g).reshape(Bp, Lp, D)
    y_sample = _final_norm(xs, final_norm_g).reshape(Bs, Ls, D)
    return (y_prompt, y_sample, jnp.stack(ks_, axis=1), jnp.stack(vs_, axis=1),
            jnp.stack(sss_, axis=1), jnp.stack(sgs_, axis=1))
```
